```python
import math
import jax
import jax.numpy as jnp
from jax import lax
import numpy as np

D_MODEL = 1024
BATCH = 16
SEQ = 2048
DEPTH = 2

N_BRANCH = 4
BRANCH_WIDTH = D_MODEL // 4
RWKV_HEADS = 4
RWKV_HEAD_DIM = BRANCH_WIDTH // RWKV_HEADS
RWKV_DECAY_RANK = 32
RWKV_ICLR_RANK = 32
RWKV_GATE_RANK = 64
RWKV_GN_EPS = 64e-5
S5_GROUP = 16
S5_GROUPS = BRANCH_WIDTH // S5_GROUP
S5_STATE = 64
S5_DT_MIN = 1e-3
S5_DT_MAX = 1e-1
FOX_HEADS = 4
FOX_HEAD_DIM = BRANCH_WIDTH // FOX_HEADS
FOX_BLOCK = 128
CONV_WIDTH = 31
PEER_HEADS = 8
PEER_KEYS = 128
PEER_EXPERTS = PEER_KEYS * PEER_KEYS
PEER_QUERY_DIM = 256
PEER_HALF = PEER_QUERY_DIM // 2
PEER_TOPK = 16
PEER_CHUNK = 128
PLE_DIM = 256
LN_EPS = 1e-5
DN_ALPHA = (2 * DEPTH) ** 0.25
DN_BETA = (8 * DEPTH) ** -0.25

RWKV_COLS = 3 * BRANCH_WIDTH + RWKV_DECAY_RANK + RWKV_ICLR_RANK + RWKV_GATE_RANK
S5_COLS = BRANCH_WIDTH
FOX_COLS = 3 * BRANCH_WIDTH + FOX_HEADS
CONV_COLS = 2 * BRANCH_WIDTH
GATE_COLS = N_BRANCH * D_MODEL
IN_COLS = RWKV_COLS + S5_COLS + FOX_COLS + CONV_COLS + GATE_COLS
IN_SPLIT = [RWKV_COLS, RWKV_COLS + S5_COLS, RWKV_COLS + S5_COLS + FOX_COLS,
            RWKV_COLS + S5_COLS + FOX_COLS + CONV_COLS]
RWKV_SPLIT = [BRANCH_WIDTH, 2 * BRANCH_WIDTH, 3 * BRANCH_WIDTH,
              3 * BRANCH_WIDTH + RWKV_DECAY_RANK,
              3 * BRANCH_WIDTH + RWKV_DECAY_RANK + RWKV_ICLR_RANK]
FOX_SPLIT = [BRANCH_WIDTH, 2 * BRANCH_WIDTH, 3 * BRANCH_WIDTH]

kernel_name = "hybrid_rwkv7_s5_fox_conformer_peer_trunk"


def layer_norm(z, g, b, eps=LN_EPS):
    zf = z.astype(jnp.float32)
    mu = zf.mean(-1, keepdims=True)
    var = jnp.square(zf - mu).mean(-1, keepdims=True)
    return ((zf - mu) * lax.rsqrt(var + eps) * g.astype(jnp.float32) + b.astype(jnp.float32)).astype(z.dtype)


def token_shift(z):
    return jnp.pad(z, ((0, 0), (1, 0), (0, 0)))[:, :-1]


def rwkv7_recurrence(r, decay, k, v, a_in, b_in):
    bsz, _, nh, hd = r.shape

    def step(state, inp):
        r_t, w_t, k_t, v_t, a_t, b_t = inp
        sa = jnp.einsum('bhij,bhj->bhi', state, a_t)
        state = (state * w_t[:, :, None, :] + sa[..., None] * b_t[:, :, None, :]
                 + v_t[..., None] * k_t[:, :, None, :])
        return state, jnp.einsum('bhij,bhj->bhi', state, r_t)

    xs = tuple(jnp.moveaxis(t, 1, 0) for t in (r, decay, k, v, a_in, b_in))
    s0 = jnp.zeros((bsz, nh, hd, hd), jnp.float32)
    _, y = lax.scan(step, s0, xs)
    return jnp.moveaxis(y, 0, 1)


def rwkv7_mixer(pr, mu, w0, w2, a0, a2, g2, k_k, k_a, r_k, lnx_g, lnx_b):
    bsz, seq, _ = pr.shape
    f32 = jnp.float32
    z = pr.astype(f32)
    z = z + (token_shift(z) - z) * mu.astype(f32)
    r, k, v, hw, ha, hg = jnp.split(z, RWKV_SPLIT, axis=-1)
    w_log = -jax.nn.softplus(-(w0.astype(f32) + jnp.tanh(hw) @ w2.astype(f32))) - 0.5
    decay = jnp.exp(-jnp.exp(w_log))
    a = jax.nn.sigmoid(a0.astype(f32) + ha @ a2.astype(f32))
    g = jax.nn.sigmoid(hg) @ g2.astype(f32)
    heads = lambda t: t.reshape(bsz, seq, RWKV_HEADS, RWKV_HEAD_DIM)
    kk = heads(k * k_k.astype(f32))
    kk = kk * lax.rsqrt(jnp.maximum(jnp.sum(kk * kk, -1, keepdims=True), 1e-24))
    k = k * (1.0 + (a - 1.0) * k_a.astype(f32))
    r_h, k_h, v_h = heads(r), heads(k), heads(v)
    y = rwkv7_recurrence(r_h, heads(decay), k_h, v_h, -kk, kk * heads(a))
    mean = y.mean(-1, keepdims=True)
    var = jnp.square(y - mean).mean(-1, keepdims=True)
    gn_g = lnx_g.astype(f32).reshape(RWKV_HEADS, RWKV_HEAD_DIM)
    gn_b = lnx_b.astype(f32).reshape(RWKV_HEADS, RWKV_HEAD_DIM)
    y = (y - mean) * lax.rsqrt(var + RWKV_GN_EPS) * gn_g + gn_b
    y = y + jnp.sum(r_h * k_h * r_k.astype(f32), -1, keepdims=True) * v_h
    return (y.reshape(bsz, seq, BRANCH_WIDTH) * g).astype(pr.dtype)


def s5_mixer(u, lam_re, lam_im, b_re, b_im, c_re, c_im, d_skip, log_dt, glu_w, glu_b):
    bsz, seq, _ = u.shape
    f32 = jnp.float32
    uf = u.astype(f32).reshape(bsz, seq, S5_GROUPS, S5_GROUP)
    lam = lax.complex(lam_re.astype(f32), lam_im.astype(f32))
    dt = jnp.exp(log_dt.astype(f32))[:, None]
    lam_bar = jnp.exp(lam * dt)
    b_bar = ((lam_bar - 1.0) / lam)[..., None] * lax.complex(b_re.astype(f32), b_im.astype(f32))
    bu = jnp.einsum('bsgc,gpc->sbgp', uf.astype(jnp.complex64), b_bar)
    a_seq = jnp.broadcast_to(lam_bar[None, None], (seq, bsz, S5_GROUPS, S5_STATE))

    def combine(e1, e2):
        a1, x1 = e1
        a2, x2 = e2
        return a1 * a2, a2 * x1 + x2

    _, states = lax.associative_scan(combine, (a_seq, bu), axis=0)
    c_mat = lax.complex(c_re.astype(f32), c_im.astype(f32))
    y = (jnp.einsum('sbgp,gcp->bsgc', states, c_mat).real
         + d_skip.astype(f32).reshape(S5_GROUPS, S5_GROUP) * uf)
    y = jax.nn.gelu(y.reshape(bsz, seq, BRANCH_WIDTH), approximate=False)
    y = y * jax.nn.sigmoid(y @ glu_w.astype(f32) + glu_b.astype(f32))
    return y.astype(u.dtype)


def fox_mixer(pf, b_f):
    bsz, seq, _ = pf.shape
    q, k, v, f_logit = jnp.split(pf, FOX_SPLIT, axis=-1)
    to_heads = lambda t: t.reshape(bsz, seq, FOX_HEADS, FOX_HEAD_DIM).transpose(0, 2, 1, 3)
    q, k, v = to_heads(q), to_heads(k), to_heads(v)
    log_f = jax.nn.log_sigmoid((f_logit + b_f).astype(jnp.float32))
    c = jnp.cumsum(log_f, axis=1).transpose(0, 2, 1)
    scale = FOX_HEAD_DIM ** -0.5
    outs = []
    for blk in range(seq // FOX_BLOCK):
        q0, q1 = blk * FOX_BLOCK, (blk + 1) * FOX_BLOCK
        s = jnp.einsum('bhqd,bhkd->bhqk', q[:, :, q0:q1], k[:, :, :q1]).astype(jnp.float32) * scale
        s = s + c[:, :, q0:q1, None] - c[:, :, None, :q1]
        causal = jnp.arange(q1)[None, :] <= jnp.arange(q0, q1)[:, None]
        s = jnp.where(causal, s, -jnp.inf)
        probs = jax.nn.softmax(s, axis=-1).astype(v.dtype)
        outs.append(jnp.einsum('bhqk,bhkd->bhqd', probs, v[:, :, :q1]))
    o = jnp.concatenate(outs, axis=2)
    return o.transpose(0, 2, 1, 3).reshape(bsz, seq, BRANCH_WIDTH)


def conv_mixer(pc, conv_w, conv_b, ln_g, ln_b):
    val, gate = jnp.split(pc, 2, axis=-1)
    h = val * jax.nn.sigmoid(gate)
    h = lax.conv_general_dilated(h, conv_w[:, None, :].astype(h.dtype), window_strides=(1,),
                                 padding=((CONV_WIDTH - 1, 0),),
                                 dimension_numbers=('NWC', 'WIO', 'NWC'),
                                 feature_group_count=BRANCH_WIDTH) + conv_b
    return jax.nn.silu(layer_norm(h, ln_g, ln_b))


def peer_ffn(h, wq, k1, k2, u_tab, v_tab):
    bsz, seq, dm = h.shape
    ntok = bsz * seq
    xt = h.reshape(ntok, dm)
    q = (xt @ wq).reshape(ntok, PEER_HEADS, 2, PEER_HALF).astype(jnp.float32)
    s1 = jnp.einsum('thd,nd->thn', q[:, :, 0], k1.astype(jnp.float32))
    s2 = jnp.einsum('thd,nd->thn', q[:, :, 1], k2.astype(jnp.float32))
    v1, i1 = lax.top_k(s1, PEER_TOPK)
    v2, i2 = lax.top_k(s2, PEER_TOPK)
    cand = (v1[..., :, None] + v2[..., None, :]).reshape(ntok, PEER_HEADS, PEER_TOPK * PEER_TOPK)
    score, ci = lax.top_k(cand, PEER_TOPK)
    eid = (jnp.take_along_axis(i1, ci // PEER_TOPK, axis=-1) * PEER_KEYS
           + jnp.take_along_axis(i2, ci % PEER_TOPK, axis=-1))
    gate = jax.nn.softmax(score, axis=-1)
    n_sel = PEER_HEADS * PEER_TOPK
    n_chunks = ntok // PEER_CHUNK

    def chunk(args):
        xc, ec, gc = args
        u = jnp.take(u_tab, ec, axis=0)
        act = jax.nn.gelu(jnp.einsum('cd,ced->ce', xc, u), approximate=False)
        w = (gc * act.astype(jnp.float32)).astype(xc.dtype)
        vv = jnp.take(v_tab, ec, axis=0)
        return jnp.einsum('ce,ced->cd', w, vv)

    out = lax.map(chunk, (xt.reshape(n_chunks, PEER_CHUNK, dm),
                          eid.reshape(n_chunks, PEER_CHUNK, n_sel),
                          gate.reshape(n_chunks, PEER_CHUNK, n_sel)))
    return out.reshape(bsz, seq, dm)


def setup_inputs(seed: int = 0) -> dict:
    key = jax.random.key(seed)
    keys = iter(jax.random.split(key, 48))
    f32 = jnp.float32

    def nrm(shape, scale):
        return jax.random.normal(next(keys), shape, f32) * scale

    def unif(shape, lo, hi):
        return jax.random.uniform(next(keys), shape, f32, lo, hi)

    L, W, D = DEPTH, BRANCH_WIDTH, D_MODEL
    G, P, Cg = S5_GROUPS, S5_STATE, S5_GROUP
    return {
        'x': nrm((BATCH, SEQ, D), 1.0),
        'p': nrm((DEPTH, BATCH, SEQ, PLE_DIM), 1.0),
        'ln_in_g': 1.0 + nrm((D,), 0.02),
        'ln_in_b': nrm((D,), 0.02),
        'w_in': nrm((L, D, IN_COLS), D ** -0.5),
        'rwkv_mu': unif((L, RWKV_COLS), 0.0, 1.0),
        'rwkv_w0': unif((L, W), -6.0, 0.0),
        'rwkv_w2': nrm((L, RWKV_DECAY_RANK, W), 0.5 * RWKV_DECAY_RANK ** -0.5),
        'rwkv_a0': nrm((L, W), 0.5),
        'rwkv_a2': nrm((L, RWKV_ICLR_RANK, W), RWKV_ICLR_RANK ** -0.5),
        'rwkv_g2': nrm((L, RWKV_GATE_RANK, W), RWKV_GATE_RANK ** -0.5),
        'rwkv_kk': 0.85 + nrm((L, W), 0.02),
        'rwkv_ka': 1.0 + nrm((L, W), 0.02),
        'rwkv_rk': nrm((L, RWKV_HEADS, RWKV_HEAD_DIM), 0.1),
        'rwkv_lnx_g': 1.0 + nrm((L, W), 0.02),
        'rwkv_lnx_b': nrm((L, W), 0.02),
        's5_lam_re': -0.5 + nrm((L, G, P), 0.01),
        's5_lam_im': math.pi * jnp.arange(P, dtype=f32) + nrm((L, G, P), 0.01),
        's5_b_re': nrm((L, G, P, Cg), Cg ** -0.5),
        's5_b_im': nrm((L, G, P, Cg), Cg ** -0.5),
        's5_c_re': nrm((L, G, Cg, P), P ** -0.5),
        's5_c_im': nrm((L, G, Cg, P), P ** -0.5),
        's5_d': nrm((L, W), 0.5),
        's5_log_dt': unif((L, G), math.log(S5_DT_MIN), math.log(S5_DT_MAX)),
        's5_glu_w': nrm((L, W, W), W ** -0.5),
        's5_glu_b': nrm((L, W), 0.02),
        'fox_bf': unif((L, FOX_HEADS), 2.0, 6.0),
        'conv_w': nrm((L, CONV_WIDTH, W), CONV_WIDTH ** -0.5),
        'conv_b': nrm((L, W), 0.02),
        'conv_ln_g': 1.0 + nrm((L, W), 0.02),
        'conv_ln_b': nrm((L, W), 0.02),
        'w_branch': nrm((L, N_BRANCH, W, D), W ** -0.5),
        'w_out': nrm((L, D, D), DN_BETA * D ** -0.5),
        'ln1_g': 1.0 + nrm((L, D), 0.02),
        'ln1_b': nrm((L, D), 0.02),
        'peer_wq': nrm((L, D, PEER_HEADS * PEER_QUERY_DIM), D ** -0.5),
        'peer_k1': nrm((L, PEER_KEYS, PEER_HALF), PEER_HALF ** -0.5),
        'peer_k2': nrm((L, PEER_KEYS, PEER_HALF), PEER_HALF ** -0.5),
        'peer_u': nrm((L, PEER_EXPERTS, D), D ** -0.5),
        'peer_v': nrm((L, PEER_EXPERTS, D), DN_BETA),
        'ple_w': nrm((L, PLE_DIM, D), DN_BETA * PLE_DIM ** -0.5),
        'ple_gate_w': nrm((L, D, D), D ** -0.5),
        'ln2_g': 1.0 + nrm((L, D), 0.02),
        'ln2_b': nrm((L, D), 0.02),
    }


def reference(x, p, ln_in_g, ln_in_b, w_in, rwkv_mu, rwkv_w0, rwkv_w2, rwkv_a0, rwkv_a2,
              rwkv_g2, rwkv_kk, rwkv_ka, rwkv_rk, rwkv_lnx_g, rwkv_lnx_b, s5_lam_re, s5_lam_im,
              s5_b_re, s5_b_im, s5_c_re, s5_c_im, s5_d, s5_log_dt, s5_glu_w, s5_glu_b, fox_bf,
              conv_w, conv_b, conv_ln_g, conv_ln_b, w_branch, w_out, ln1_g, ln1_b, peer_wq,
              peer_k1, peer_k2, peer_u, peer_v, ple_w, ple_gate_w, ln2_g, ln2_b):
    h = layer_norm(x, ln_in_g, ln_in_b)
    for i in range(DEPTH):
        proj = h @ w_in[i]
        p_rwkv, p_s5, p_fox, p_conv, p_gate = jnp.split(proj, IN_SPLIT, axis=-1)
        branches = (
            rwkv7_mixer(p_rwkv, rwkv_mu[i], rwkv_w0[i], rwkv_w2[i], rwkv_a0[i], rwkv_a2[i],
                        rwkv_g2[i], rwkv_kk[i], rwkv_ka[i], rwkv_rk[i], rwkv_lnx_g[i], rwkv_lnx_b[i]),
            s5_mixer(p_s5, s5_lam_re[i], s5_lam_im[i], s5_b_re[i], s5_b_im[i], s5_c_re[i],
                     s5_c_im[i], s5_d[i], s5_log_dt[i], s5_glu_w[i], s5_glu_b[i]),
            fox_mixer(p_fox, fox_bf[i]),
            conv_mixer(p_conv, conv_w[i], conv_b[i], conv_ln_g[i], conv_ln_b[i]),
        )
        merged = None
        for b, y_b in enumerate(branches):
            gate_b = jax.nn.sigmoid(p_gate[..., b * D_MODEL:(b + 1) * D_MODEL])
            term = gate_b * (y_b @ w_branch[i, b])
            merged = term if merged is None else merged + term
        mix = merged @ w_out[i]
        h = layer_norm(DN_ALPHA * h + mix, ln1_g[i], ln1_b[i])
        ffn = peer_ffn(h, peer_wq[i], peer_k1[i], peer_k2[i], peer_u[i], peer_v[i])
        ple = (p[i] @ ple_w[i]) * jax.nn.sigmoid(h @ ple_gate_w[i])
        h = layer_norm(DN_ALPHA * h + ffn + ple, ln2_g[i], ln2_b[i])
    return h
```

```python
import functools
import math

import jax
import jax.numpy as jnp
from jax import lax
from jax.experimental import pallas as pl
from jax.experimental.pallas import tpu as pltpu

F32 = jnp.float32
BF16 = jnp.bfloat16
I32 = jnp.int32
U32 = jnp.uint32
HI = lax.Precision.HIGHEST

D_MODEL = 1024
BRANCH = 256
HEADS = 4
HEAD_DIM = 64
RWKV_COLS = 896
S5_GROUPS = 16
S5_GROUP = 16
S5_STATE = 64
S5_N = S5_GROUPS * S5_STATE
CONV_WIDTH = 31
CONV_HALO = 32
PEER_HEADS = 8
PEER_KEYS = 128
PEER_HALF = 128
PEER_TOPK = 16
PEER_SEL = PEER_HEADS * PEER_TOPK
PLE_DIM = 256
RWKV_GN_EPS = 64e-5
LN_EPS = 1e-5
DEPTH = 2
DN_ALPHA = (2 * DEPTH) ** 0.25

SUBLANES = 8
LANES = 128
VMEM_LIMIT = 56 * 1024 * 1024

TM_PROJ = 512
TQ_PRE = 256
TC_SCAN = 32
TC_S5 = 32
TQ_ATT = 256
TK_ATT = 256
TC_CONV = 512
TM_MERGE = 256
TM_ROUTE = 128
G_PEER = 16
TM_OUT = 256


def _cparams(sem, vmem=None):
    return pltpu.CompilerParams(dimension_semantics=sem, vmem_limit_bytes=vmem)


def _ln(z, g, b):
    mu = jnp.mean(z, axis=-1, keepdims=True)
    zc = z - mu
    var = jnp.mean(zc * zc, axis=-1, keepdims=True)
    return zc * lax.rsqrt(var + LN_EPS) * g + b


def _gelu(y):
    return 0.5 * y * (1.0 + lax.erf(y * (1.0 / math.sqrt(2.0))))


def _sigmoid(y):
    return 1.0 / (1.0 + jnp.exp(-y))


def _softplus(y):
    return jnp.maximum(y, 0.0) + jnp.log(1.0 + jnp.exp(-jnp.abs(y)))


def _ln_kernel(x_ref, g_ref, b_ref, o_ref):
    o_ref[...] = _ln(x_ref[...], g_ref[...], b_ref[...])


def _layer_norm(x2, g, b):
    t, d = x2.shape
    tm = TM_PROJ
    return pl.pallas_call(
        _ln_kernel,
        grid=(t // tm,),
        in_specs=[pl.BlockSpec((tm, d), lambda i: (i, 0)),
                  pl.BlockSpec((1, d), lambda i: (0, 0)),
                  pl.BlockSpec((1, d), lambda i: (0, 0))],
        out_specs=pl.BlockSpec((tm, d), lambda i: (i, 0)),
        out_shape=jax.ShapeDtypeStruct((t, d), F32),
        compiler_params=_cparams(("parallel",)),
        name="ln_in",
    )(x2, g.reshape(1, d), b.reshape(1, d))


_P_RWKV = (0, 896)
_P_S5 = (896, 1152)
_P_FQKV = (1152, 1920)
_P_FF = (1920, 2048)
_P_CONV = (2048, 2560)
_P_COLS = 2560


def _proj_kernel(h_ref, w_ref, rw_ref, s5_ref, fq_ref, ff_ref, cv_ref):
    x = h_ref[...].astype(BF16)
    for (lo, hi), o_ref in ((_P_RWKV, rw_ref), (_P_S5, s5_ref), (_P_FQKV, fq_ref),
                            (_P_FF, ff_ref), (_P_CONV, cv_ref)):
        o_ref[...] = jnp.dot(x, w_ref[:, lo:hi], preferred_element_type=F32)


def _project(h, w_packed, bsz, seq):
    t, d = h.shape
    tm = TM_PROJ
    nst = seq // tm
    widths = [hi - lo for lo, hi in (_P_RWKV, _P_S5, _P_FQKV, _P_FF, _P_CONV)]
    tok = lambda i: (i, 0)
    out_shapes = [jax.ShapeDtypeStruct((t, widths[0]), F32),
                  jax.ShapeDtypeStruct((seq, bsz * widths[1]), F32),
                  jax.ShapeDtypeStruct((t, widths[2]), F32),
                  jax.ShapeDtypeStruct((t, widths[3]), F32),
                  jax.ShapeDtypeStruct((t, widths[4]), F32)]
    out_specs = [pl.BlockSpec((tm, widths[0]), tok),
                 pl.BlockSpec((tm, widths[1]), lambda i: (i % nst, i // nst)),
                 pl.BlockSpec((tm, widths[2]), tok),
                 pl.BlockSpec((tm, widths[3]), tok),
                 pl.BlockSpec((tm, widths[4]), tok)]
    return pl.pallas_call(
        _proj_kernel,
        grid=(t // tm,),
        in_specs=[pl.BlockSpec((tm, d), tok),
                  pl.BlockSpec((d, _P_COLS), lambda i: (0, 0))],
        out_specs=out_specs,
        out_shape=out_shapes,
        compiler_params=_cparams(("parallel",), VMEM_LIMIT),
        name="in_proj",
    )(h, w_packed)


def _rwkv_pre_kernel(z_ref, zp_ref, mu_ref, w0_ref, a0_ref, kk_ref, ka_ref, rk_ref, wl_ref, bd_ref,
                     r_out, w_out, k_out, v_out, na_out, b_out, g_out, bv_out):
    j = pl.program_id(1)
    z = z_ref[0]
    prev = zp_ref[0][SUBLANES - 1:SUBLANES, :]
    prev = jnp.where(j == 0, 0.0, prev)
    row = lax.broadcasted_iota(I32, z.shape, 0)
    zs = jnp.where(row == 0, prev, pltpu.roll(z, 1, 0))
    z = z + (zs - z) * mu_ref[...]
    r = z[:, 0:256]
    k = z[:, 256:512]
    v = z[:, 512:768]
    zc = z[:, 768:896]
    lane = lax.broadcasted_iota(I32, zc.shape, 1)
    act = jnp.where(lane < 32, jnp.tanh(zc), jnp.where(lane < 64, zc, _sigmoid(zc)))
    lo = jnp.dot(act, wl_ref[...], precision=HI, preferred_element_type=F32)
    dw = lo[:, 0:256]
    da = lo[:, 256:512]
    g = lo[:, 512:768]
    w_log = -_softplus(-(w0_ref[...] + dw)) - 0.5
    decay = jnp.exp(-jnp.exp(w_log))
    a = _sigmoid(a0_ref[...] + da)
    kkv = k * kk_ref[...]
    bd = bd_ref[...]
    ss = jnp.dot(kkv * kkv, bd, precision=HI, preferred_element_type=F32)
    kkn = kkv * lax.rsqrt(jnp.maximum(ss, 1e-24))
    k2 = k * (1.0 + (a - 1.0) * ka_ref[...])
    bonus = jnp.dot(r * k2 * rk_ref[...], bd, precision=HI, preferred_element_type=F32) * v
    r_out[0] = r
    w_out[0] = decay
    k_out[0] = k2
    v_out[0] = v
    na_out[0] = -kkn
    b_out[0] = kkn * a
    g_out[0] = g
    bv_out[0] = bonus


def _rwkv_pre(p_rwkv, mu, w0, a0, kk, ka, rk, wl, bd, bsz, seq):
    z3 = p_rwkv.reshape(bsz, seq, RWKV_COLS)
    tq = TQ_PRE
    blk = lambda b, j: (b, j, 0)
    par = lambda b, j: (0, 0)
    out_shape = [jax.ShapeDtypeStruct((bsz, seq, BRANCH), F32)] * 8
    out_specs = [pl.BlockSpec((1, tq, BRANCH), blk)] * 8
    return pl.pallas_call(
        _rwkv_pre_kernel,
        grid=(bsz, seq // tq),
        in_specs=[pl.BlockSpec((1, tq, RWKV_COLS), blk),
                  pl.BlockSpec((1, SUBLANES, RWKV_COLS),
                               lambda b, j: (b, jnp.maximum(j * (tq // SUBLANES) - 1, 0), 0)),
                  pl.BlockSpec((1, RWKV_COLS), par),
                  pl.BlockSpec((1, BRANCH), par), pl.BlockSpec((1, BRANCH), par),
                  pl.BlockSpec((1, BRANCH), par), pl.BlockSpec((1, BRANCH), par),
                  pl.BlockSpec((1, BRANCH), par),
                  pl.BlockSpec((LANES, 3 * BRANCH), par),
                  pl.BlockSpec((BRANCH, BRANCH), par)],
        out_specs=out_specs,
        out_shape=out_shape,
        compiler_params=_cparams(("parallel", "parallel")),
        name="rwkv_pre",
    )(z3, z3, mu.reshape(1, -1), w0.reshape(1, -1), a0.reshape(1, -1), kk.reshape(1, -1),
      ka.reshape(1, -1), rk.reshape(1, -1), wl, bd)


_SCAN_IG = HEAD_DIM // 2 // SUBLANES


def _rwkv_scan_kernel(a_ref, w_ref, b_ref, k_ref, r_ref, v_ref, y_ref, s_ref):
    @pl.when(pl.program_id(0) == 0)
    def _():
        s_ref[...] = jnp.zeros_like(s_ref)

    tc = a_ref.shape[0]
    tile = (SUBLANES, LANES)

    def step(t, carry):
        vv = [v_ref[t, ig * SUBLANES:(ig + 1) * SUBLANES, :] for ig in range(_SCAN_IG)]
        sa = [[jnp.zeros(tile, F32), jnp.zeros(tile, F32)] for _ in range(_SCAN_IG)]
        for j in range(HEAD_DIM):
            ab = jnp.broadcast_to(a_ref[t, j:j + 1, :], tile)
            for ig in range(_SCAN_IG):
                sa[ig][j % 2] = sa[ig][j % 2] + s_ref[ig, j] * ab
        sa = [x[0] + x[1] for x in sa]
        yy = [[jnp.zeros(tile, F32), jnp.zeros(tile, F32)] for _ in range(_SCAN_IG)]
        for j in range(HEAD_DIM):
            wb = jnp.broadcast_to(w_ref[t, j:j + 1, :], tile)
            bb = jnp.broadcast_to(b_ref[t, j:j + 1, :], tile)
            kb = jnp.broadcast_to(k_ref[t, j:j + 1, :], tile)
            rb = jnp.broadcast_to(r_ref[t, j:j + 1, :], tile)
            for ig in range(_SCAN_IG):
                s = s_ref[ig, j] * wb + sa[ig] * bb + vv[ig] * kb
                s_ref[ig, j] = s
                yy[ig][j % 2] = yy[ig][j % 2] + s * rb
        for ig in range(_SCAN_IG):
            y_ref[t, ig * SUBLANES:(ig + 1) * SUBLANES, :] = yy[ig][0] + yy[ig][1]
        return carry

    lax.fori_loop(0, tc, step, 0)


def _rwkv_scan(a_t, w_t, b_t, k_t, r_t, v_t):
    seq = a_t.shape[0]
    tc = TC_SCAN
    jspec = pl.BlockSpec((tc, HEAD_DIM, LANES), lambda i: (i, 0, 0))
    ispec = pl.BlockSpec((tc, HEAD_DIM // 2, LANES), lambda i: (i, 0, 0))
    return pl.pallas_call(
        _rwkv_scan_kernel,
        grid=(seq // tc,),
        in_specs=[jspec, jspec, jspec, jspec, jspec, ispec],
        out_specs=ispec,
        out_shape=jax.ShapeDtypeStruct((seq, HEAD_DIM // 2, LANES), F32),
        scratch_shapes=[pltpu.VMEM((_SCAN_IG, HEAD_DIM, SUBLANES, LANES), F32)],
        compiler_params=_cparams(("arbitrary",)),
        name="rwkv_scan",
    )(a_t, w_t, b_t, k_t, r_t, v_t)


def _to_scan_j(x, bsz, seq):
    n_pairs = bsz * HEADS
    xt = x.reshape(bsz, seq, HEADS, HEAD_DIM).transpose(1, 3, 0, 2).reshape(seq, HEAD_DIM, n_pairs)
    pad = LANES // 2 - n_pairs
    if pad:
        xt = jnp.pad(xt, ((0, 0), (0, 0), (0, pad)))
    return jnp.concatenate([xt, xt], axis=-1)


def _to_scan_i(x, bsz, seq):
    n_pairs = bsz * HEADS
    xt = x.reshape(bsz, seq, HEADS, 2, HEAD_DIM // 2).transpose(1, 4, 3, 0, 2)
    xt = xt.reshape(seq, HEAD_DIM // 2, 2, n_pairs)
    pad = LANES // 2 - n_pairs
    if pad:
        xt = jnp.pad(xt, ((0, 0), (0, 0), (0, 0), (0, pad)))
    return xt.reshape(seq, HEAD_DIM // 2, LANES)


def _from_scan_i(y, bsz, seq):
    n_pairs = bsz * HEADS
    yt = y.reshape(seq, HEAD_DIM // 2, 2, LANES // 2)[..., :n_pairs]
    yt = yt.reshape(seq, HEAD_DIM // 2, 2, bsz, HEADS).transpose(3, 0, 4, 2, 1)
    return yt.reshape(bsz * seq, BRANCH)


def _rwkv_post_kernel(y_ref, g_ref, bv_ref, gg_ref, gb_ref, bd_ref, o_ref):
    y = y_ref[...]
    bd = bd_ref[...]
    inv = 1.0 / HEAD_DIM
    mean = jnp.dot(y, bd, precision=HI, preferred_element_type=F32) * inv
    yc = y - mean
    var = jnp.dot(yc * yc, bd, precision=HI, preferred_element_type=F32) * inv
    yn = yc * lax.rsqrt(var + RWKV_GN_EPS) * gg_ref[...] + gb_ref[...]
    o_ref[...] = (yn + bv_ref[...]) * g_ref[...]


def _rwkv_post(y, g, bv, gn_g, gn_b, bd):
    t = y.shape[0]
    tm = TM_PROJ
    tok = lambda i: (i, 0)
    par = lambda i: (0, 0)
    return pl.pallas_call(
        _rwkv_post_kernel,
        grid=(t // tm,),
        in_specs=[pl.BlockSpec((tm, BRANCH), tok)] * 3
                 + [pl.BlockSpec((1, BRANCH), par)] * 2 + [pl.BlockSpec((BRANCH, BRANCH), par)],
        out_specs=pl.BlockSpec((tm, BRANCH), tok),
        out_shape=jax.ShapeDtypeStruct((t, BRANCH), F32),
        compiler_params=_cparams(("parallel",)),
        name="rwkv_post",
    )(y, g, bv, gn_g.reshape(1, -1), gn_b.reshape(1, -1), bd)


def _s5_kernel(u_ref, bre_ref, bim_ref, lre_ref, lim_ref, cc_ref, d_ref, gw_ref, gb_ref, o_ref,
               xre_ref, xim_ref, bure_ref, buim_ref, xs_ref):
    @pl.when(pl.program_id(0) == 0)
    def _():
        xre_ref[...] = jnp.zeros_like(xre_ref)
        xim_ref[...] = jnp.zeros_like(xim_ref)

    tc, nb, w = u_ref.shape
    u2 = u_ref[...].reshape(tc * nb, w)
    bure_ref[...] = jnp.dot(u2, bre_ref[...], precision=HI, preferred_element_type=F32)
    buim_ref[...] = jnp.dot(u2, bim_ref[...], precision=HI, preferred_element_type=F32)
    lre = jnp.broadcast_to(lre_ref[...], (nb, S5_N))
    lim = jnp.broadcast_to(lim_ref[...], (nb, S5_N))

    def step(t, carry):
        xr, xi = carry
        r0 = pl.multiple_of(t * nb, nb)
        nr = lre * xr - lim * xi + bure_ref[pl.ds(r0, nb), :]
        ni = lre * xi + lim * xr + buim_ref[pl.ds(r0, nb), :]
        xs_ref[pl.ds(r0, nb), 0:S5_N] = nr
        xs_ref[pl.ds(r0, nb), S5_N:2 * S5_N] = ni
        return nr, ni

    xr, xi = lax.fori_loop(0, tc, step, (xre_ref[...], xim_ref[...]))
    xre_ref[...] = xr
    xim_ref[...] = xi
    y = jnp.dot(xs_ref[...], cc_ref[...], precision=HI, preferred_element_type=F32) + d_ref[...] * u2
    y = _gelu(y)
    gate = jnp.dot(y, gw_ref[...], precision=HI, preferred_element_type=F32) + gb_ref[...]
    o_ref[...] = (y * _sigmoid(gate)).reshape(tc, nb, w)


def _s5(u_t, bre, bim, lre, lim, cc, d_skip, glu_w, glu_b, bsz, seq):
    u3 = u_t.reshape(seq, bsz, BRANCH)
    tc = TC_S5
    par = lambda i: (0, 0)
    blk = pl.BlockSpec((tc, bsz, BRANCH), lambda i: (i, 0, 0))
    out = pl.pallas_call(
        _s5_kernel,
        grid=(seq // tc,),
        in_specs=[blk,
                  pl.BlockSpec((BRANCH, S5_N), par), pl.BlockSpec((BRANCH, S5_N), par),
                  pl.BlockSpec((1, S5_N), par), pl.BlockSpec((1, S5_N), par),
                  pl.BlockSpec((2 * S5_N, BRANCH), par),
                  pl.BlockSpec((1, BRANCH), par),
                  pl.BlockSpec((BRANCH, BRANCH), par),
                  pl.BlockSpec((1, BRANCH), par)],
        out_specs=blk,
        out_shape=jax.ShapeDtypeStruct((seq, bsz, BRANCH), F32),
        scratch_shapes=[pltpu.VMEM((bsz, S5_N), F32), pltpu.VMEM((bsz, S5_N), F32),
                        pltpu.VMEM((tc * bsz, S5_N), F32), pltpu.VMEM((tc * bsz, S5_N), F32),
                        pltpu.VMEM((tc * bsz, 2 * S5_N), F32)],
        compiler_params=_cparams(("arbitrary",), VMEM_LIMIT),
        name="s5_scan",
    )(u3, bre, bim, lre, lim, cc, d_skip.reshape(1, -1), glu_w, glu_b.reshape(1, -1))
    return out.reshape(seq, bsz * BRANCH)


def _s5_params(lam_re, lam_im, b_re, b_im, c_re, c_im, log_dt):
    lam = lax.complex(lam_re.astype(F32), lam_im.astype(F32))
    dt = jnp.exp(log_dt.astype(F32))[:, None]
    lam_bar = jnp.exp(lam * dt)
    b_bar = ((lam_bar - 1.0) / lam)[..., None] * lax.complex(b_re.astype(F32), b_im.astype(F32))
    eye = jnp.eye(S5_GROUPS, dtype=F32)
    bre = jnp.einsum('gpc,gh->gchp', jnp.real(b_bar), eye).reshape(BRANCH, S5_N)
    bim = jnp.einsum('gpc,gh->gchp', jnp.imag(b_bar), eye).reshape(BRANCH, S5_N)
    cre = jnp.einsum('gcp,gh->gphc', c_re.astype(F32), eye).reshape(S5_N, BRANCH)
    cim = jnp.einsum('gcp,gh->gphc', c_im.astype(F32), eye).reshape(S5_N, BRANCH)
    cc = jnp.concatenate([cre, -cim], axis=0)
    return bre, bim, jnp.real(lam_bar).reshape(1, S5_N), jnp.imag(lam_bar).reshape(1, S5_N), cc


def _fox_cumsum_kernel(f_ref, bias_ref, c_ref):
    x = f_ref[...] + bias_ref[...]
    c = -_softplus(-x)
    n = c.shape[1]
    lane = lax.broadcasted_iota(I32, c.shape, 1)
    sh = 1
    while sh < n:
        c = c + jnp.where(lane >= sh, pltpu.roll(c, sh, 1), 0.0)
        sh *= 2
    c_ref[...] = c


def _fox_cumsum(f_rows, bias_rows):
    rows, seq = f_rows.shape
    return pl.pallas_call(
        _fox_cumsum_kernel,
        out_shape=jax.ShapeDtypeStruct((rows, seq), F32),
        name="fox_cumsum",
    )(f_rows, bias_rows)


def _fox_attn_kernel(q_ref, k_ref, v_ref, cq_ref, ck_ref, o_ref):
    qi = pl.program_id(2)
    tq = q_ref.shape[2]
    tk = ck_ref.shape[3]
    q = (q_ref[0, 0] * (HEAD_DIM ** -0.5)).astype(BF16)
    cq = cq_ref[0, 0]
    qpos = qi * tq + lax.broadcasted_iota(I32, (tq, tk), 0)
    nkb = (qi * tq + tq + tk - 1) // tk

    def body(kb, carry):
        m, l, acc = carry
        k0 = pl.multiple_of(kb * tk, tk)
        kblk = k_ref[0, 0, pl.ds(k0, tk), :].astype(BF16)
        vblk = v_ref[0, 0, pl.ds(k0, tk), :].astype(BF16)
        s = lax.dot_general(q, kblk, (((1,), (1,)), ((), ())), preferred_element_type=F32)
        s = s + cq - ck_ref[0, 0, pl.ds(kb, 1), :]
        kpos = k0 + lax.broadcasted_iota(I32, (tq, tk), 1)
        s = jnp.where(kpos <= qpos, s, -1e30)
        m_new = jnp.maximum(m, jnp.max(s, axis=-1, keepdims=True))
        alpha = jnp.exp(m - m_new)
        p = jnp.exp(s - m_new)
        l = alpha * l + jnp.sum(p, axis=-1, keepdims=True)
        acc = alpha * acc + jnp.dot(p.astype(BF16), vblk, preferred_element_type=F32)
        return m_new, l, acc

    init = (jnp.full((tq, 1), -1e30, F32), jnp.zeros((tq, 1), F32), jnp.zeros((tq, HEAD_DIM), F32))
    m, l, acc = lax.fori_loop(0, nkb, body, init)
    o_ref[0, 0] = acc / l


def _fox_attn(q, k, v, c_col, c_rowb):
    bsz, nh, seq, hd = q.shape
    tq = TQ_ATT
    nkb, tk = c_rowb.shape[2], c_rowb.shape[3]
    full = lambda b, h, i: (b, h, 0, 0)
    return pl.pallas_call(
        _fox_attn_kernel,
        grid=(bsz, nh, seq // tq),
        in_specs=[pl.BlockSpec((1, 1, tq, hd), lambda b, h, i: (b, h, i, 0)),
                  pl.BlockSpec((1, 1, seq, hd), full),
                  pl.BlockSpec((1, 1, seq, hd), full),
                  pl.BlockSpec((1, 1, tq, 1), lambda b, h, i: (b, h, i, 0)),
                  pl.BlockSpec((1, 1, nkb, tk), full)],
        out_specs=pl.BlockSpec((1, 1, tq, hd), lambda b, h, i: (b, h, i, 0)),
        out_shape=jax.ShapeDtypeStruct((bsz, nh, seq, hd), F32),
        compiler_params=_cparams(("parallel", "parallel", "parallel")),
        name="fox_attn",
    )(q, k, v, c_col, c_rowb)


def _conv_kernel(x_ref, xp_ref, w_ref, b_ref, g_ref, be_ref, o_ref, buf_ref):
    j = pl.program_id(1)
    tc = x_ref.shape[1]
    x = x_ref[0]
    buf_ref[CONV_HALO:CONV_HALO + tc, :] = x[:, 0:BRANCH] * _sigmoid(x[:, BRANCH:2 * BRANCH])
    xp = xp_ref[0]
    hp = xp[:, 0:BRANCH] * _sigmoid(xp[:, BRANCH:2 * BRANCH])
    buf_ref[0:CONV_HALO, :] = jnp.where(j == 0, 0.0, hp)
    acc = jnp.zeros((tc, BRANCH), F32) + b_ref[...]
    off = CONV_HALO - (CONV_WIDTH - 1)
    for kk in range(CONV_WIDTH):
        acc = acc + buf_ref[off + kk:off + kk + tc, :] * w_ref[kk:kk + 1, :]
    y = _ln(acc, g_ref[...], be_ref[...])
    o_ref[0] = y * _sigmoid(y)


def _conv_mixer(p_conv, conv_w, conv_b, ln_g, ln_b, bsz, seq):
    x3 = p_conv.reshape(bsz, seq, 2 * BRANCH)
    tc = TC_CONV
    par = lambda b, j: (0, 0)
    out = pl.pallas_call(
        _conv_kernel,
        grid=(bsz, seq // tc),
        in_specs=[pl.BlockSpec((1, tc, 2 * BRANCH), lambda b, j: (b, j, 0)),
                  pl.BlockSpec((1, CONV_HALO, 2 * BRANCH),
                               lambda b, j: (b, jnp.maximum(j * (tc // CONV_HALO) - 1, 0), 0)),
                  pl.BlockSpec((CONV_WIDTH, BRANCH), par),
                  pl.BlockSpec((1, BRANCH), par), pl.BlockSpec((1, BRANCH), par),
                  pl.BlockSpec((1, BRANCH), par)],
        out_specs=pl.BlockSpec((1, tc, BRANCH), lambda b, j: (b, j, 0)),
        out_shape=jax.ShapeDtypeStruct((bsz, seq, BRANCH), F32),
        scratch_shapes=[pltpu.VMEM((CONV_HALO + tc, BRANCH), F32)],
        compiler_params=_cparams(("parallel", "parallel")),
        name="conv_mixer",
    )(x3, x3, conv_w, conv_b.reshape(1, -1), ln_g.reshape(1, -1), ln_b.reshape(1, -1))
    return out.reshape(bsz * seq, BRANCH)


def _merge_kernel(h_ref, y0_ref, y1_ref, y2_ref, y3_ref, wg_ref, wb_ref, wo_ref, g_ref, b_ref, o_ref):
    h = h_ref[...]
    hb = h.astype(BF16)
    merged = None
    for br, y_ref in enumerate((y0_ref, y1_ref, y2_ref, y3_ref)):
        gate = _sigmoid(jnp.dot(hb, wg_ref[:, br * D_MODEL:(br + 1) * D_MODEL], preferred_element_type=F32))
        term = gate * jnp.dot(y_ref[...].astype(BF16), wb_ref[br], preferred_element_type=F32)
        merged = term if merged is None else merged + term
    mix = jnp.dot(merged.astype(BF16), wo_ref[...], preferred_element_type=F32)
    o_ref[...] = _ln(DN_ALPHA * h + mix, g_ref[...], b_ref[...])


def _merge(h, y_rwkv, y_s5_t, y_fox, y_conv, wg, wb, wo, ln_g, ln_b, bsz, seq):
    t, d = h.shape
    tm = TM_MERGE
    nst = seq // tm
    tok = lambda i: (i, 0)
    par2 = lambda i: (0, 0)
    br_spec = pl.BlockSpec((tm, BRANCH), tok)
    return pl.pallas_call(
        _merge_kernel,
        grid=(t // tm,),
        in_specs=[pl.BlockSpec((tm, d), tok),
                  br_spec,
                  pl.BlockSpec((tm, BRANCH), lambda i: (i % nst, i // nst)),
                  br_spec, br_spec,
                  pl.BlockSpec((d, 4 * d), par2),
                  pl.BlockSpec((4, BRANCH, d), lambda i: (0, 0, 0)),
                  pl.BlockSpec((d, d), par2),
                  pl.BlockSpec((1, d), par2), pl.BlockSpec((1, d), par2)],
        out_specs=pl.BlockSpec((tm, d), tok),
        out_shape=jax.ShapeDtypeStruct((t, d), F32),
        compiler_params=_cparams(("parallel",), VMEM_LIMIT),
        name="merge_ln1",
    )(h, y_rwkv, y_s5_t, y_fox, y_conv, wg, wb, wo, ln_g.reshape(1, -1), ln_b.reshape(1, -1))


def _top_rows(s, n_rows, payload=None):
    row = lax.broadcasted_iota(I32, s.shape, 0)
    vals, idxs, pays = [], [], []
    for _ in range(PEER_TOPK):
        m = jnp.max(s, axis=0, keepdims=True)
        idx = jnp.min(jnp.where(s == m, row, n_rows), axis=0, keepdims=True)
        sel = row == idx
        if payload is not None:
            pays.append(jnp.max(jnp.where(sel, payload, -1), axis=0, keepdims=True))
        s = jnp.where(sel, -jnp.inf, s)
        vals.append(m)
        idxs.append(idx)
    return vals, idxs, pays


def _route_kernel(h_ref, wq_ref, keys_ref, eid_ref, gate_ref, q_scr, v_scr, i_scr):
    tm = h_ref.shape[0]
    q = jnp.dot(h_ref[...].astype(BF16), wq_ref[...], preferred_element_type=F32)
    for c in range(2 * PEER_HEADS):
        q_scr[c] = q[:, c * PEER_HALF:(c + 1) * PEER_HALF]

    def sub_body(c, carry):
        st = lax.dot_general(keys_ref[c % 2], q_scr[c], (((1,), (1,)), ((), ())),
                             precision=HI, preferred_element_type=F32)
        vals, idxs, _ = _top_rows(st, PEER_KEYS)
        v_scr[c] = jnp.concatenate(vals, axis=0)
        i_scr[c] = jnp.concatenate(idxs, axis=0)
        return carry

    lax.fori_loop(0, 2 * PEER_HEADS, sub_body, 0)

    def head_body(hh, carry):
        v1 = v_scr[2 * hh]
        v2 = v_scr[2 * hh + 1]
        i1 = i_scr[2 * hh]
        i2 = i_scr[2 * hh + 1]
        cand = jnp.concatenate([v1[a:a + 1, :] + v2 for a in range(PEER_TOPK)], axis=0)
        eids = jnp.concatenate([i1[a:a + 1, :] * PEER_KEYS + i2 for a in range(PEER_TOPK)], axis=0)
        vals, _, pays = _top_rows(cand, PEER_TOPK * PEER_TOPK, payload=eids)
        score = jnp.concatenate(vals, axis=0)
        e = jnp.exp(score - jnp.max(score, axis=0, keepdims=True))
        r0 = pl.multiple_of(hh * PEER_TOPK, PEER_TOPK)
        gate_ref[pl.ds(r0, PEER_TOPK), :] = e / jnp.sum(e, axis=0, keepdims=True)
        eid_ref[pl.ds(r0, PEER_TOPK), :] = jnp.concatenate(pays, axis=0)
        return carry

    lax.fori_loop(0, PEER_HEADS, head_body, 0)


def _route(h, wq, keys):
    t, d = h.shape
    tm = TM_ROUTE
    nq = 2 * PEER_HEADS
    return pl.pallas_call(
        _route_kernel,
        grid=(t // tm,),
        in_specs=[pl.BlockSpec((tm, d), lambda i: (i, 0)),
                  pl.BlockSpec((d, nq * PEER_HALF), lambda i: (0, 0)),
                  pl.BlockSpec((2, PEER_KEYS, PEER_HALF), lambda i: (0, 0, 0))],
        out_specs=[pl.BlockSpec((PEER_SEL, tm), lambda i: (0, i)),
                   pl.BlockSpec((PEER_SEL, tm), lambda i: (0, i))],
        out_shape=[jax.ShapeDtypeStruct((PEER_SEL, t), I32),
                   jax.ShapeDtypeStruct((PEER_SEL, t), F32)],
        scratch_shapes=[pltpu.VMEM((nq, tm, PEER_HALF), F32),
                        pltpu.VMEM((nq, PEER_TOPK, tm), F32),
                        pltpu.VMEM((nq, PEER_TOPK, tm), I32)],
        compiler_params=_cparams(("parallel",), VMEM_LIMIT),
        name="peer_route",
    )(h, wq, keys)


def _peer_kernel(eid_cur_ref, eid_nxt_ref, h_ref, gate_ref, uv_hbm, o_ref, buf, sem):
    step = pl.program_id(0)
    nsteps = pl.num_programs(0)
    slot = step % 2
    g_tok = h_ref.shape[0]
    rows = g_tok * PEER_SEL

    def row_copy(idx_ref, sl, r):
        return pltpu.make_async_copy(uv_hbm.at[pl.ds(idx_ref[0, 0, r], 1)],
                                     buf.at[sl, pl.ds(r, 1)], sem.at[sl])

    def issue(idx_ref, sl):
        def body(r, carry):
            row_copy(idx_ref, sl, r).start()
            return carry
        lax.fori_loop(0, rows, body, 0, unroll=8)

    @pl.when(step == 0)
    def _():
        issue(eid_cur_ref, 0)

    @pl.when(step + 1 < nsteps)
    def _():
        issue(eid_nxt_ref, 1 - slot)

    pltpu.make_async_copy(uv_hbm.at[pl.ds(0, rows)], buf.at[slot], sem.at[slot]).wait()

    words = buf[slot]
    u_rows = pltpu.bitcast(words << 16, F32).astype(BF16)
    v_rows = pltpu.bitcast(words & jnp.uint32(0xFFFF0000), F32).astype(BF16)
    xb = h_ref[...].astype(BF16)
    a_all = lax.dot_general(xb, u_rows, (((1,), (1,)), ((), ())), preferred_element_type=F32)
    own = (lax.broadcasted_iota(I32, (g_tok, rows), 1) // PEER_SEL
           == lax.broadcasted_iota(I32, (g_tok, rows), 0))
    a_own = jnp.where(own, a_all, 0.0)
    act = a_own[:, 0:PEER_SEL]
    for g in range(1, g_tok):
        act = act + a_own[:, g * PEER_SEL:(g + 1) * PEER_SEL]
    wgt = gate_ref[...] * _gelu(act)
    w_all = jnp.where(own, jnp.concatenate([wgt] * g_tok, axis=1), 0.0).astype(BF16)
    o_ref[...] = jnp.dot(w_all, v_rows, preferred_element_type=F32)


def _peer(h, eid_tok, gate_tok, uv_packed):
    t, d = h.shape
    g_tok = G_PEER
    nsteps = t // g_tok
    rows = g_tok * PEER_SEL
    eid3 = eid_tok.reshape(nsteps, 1, rows)
    smem_blk = lambda f: pl.BlockSpec((1, 1, rows), f, memory_space=pltpu.SMEM)
    return pl.pallas_call(
        _peer_kernel,
        grid=(nsteps,),
        in_specs=[smem_blk(lambda i: (i, 0, 0)),
                  smem_blk(lambda i: (jnp.minimum(i + 1, nsteps - 1), 0, 0)),
                  pl.BlockSpec((g_tok, d), lambda i: (i, 0)),
                  pl.BlockSpec((g_tok, PEER_SEL), lambda i: (i, 0)),
                  pl.BlockSpec(memory_space=pl.ANY)],
        out_specs=pl.BlockSpec((g_tok, d), lambda i: (i, 0)),
        out_shape=jax.ShapeDtypeStruct((t, d), F32),
        scratch_shapes=[pltpu.VMEM((2, rows, d), U32), pltpu.SemaphoreType.DMA((2,))],
        compiler_params=_cparams(("arbitrary",), VMEM_LIMIT),
        name="peer_experts",
    )(eid3, eid3, h, gate_tok, uv_packed)


def _pack_tables(u_tab, v_tab):
    ub = lax.bitcast_convert_type(u_tab.astype(BF16), jnp.uint16).astype(U32)
    vb = lax.bitcast_convert_type(v_tab.astype(BF16), jnp.uint16).astype(U32)
    return ub | (vb << 16)


def _out_kernel(h_ref, f_ref, p_ref, wp_ref, wgp_ref, g_ref, b_ref, o_ref):
    h = h_ref[...]
    ple = jnp.dot(p_ref[...].astype(BF16), wp_ref[...], preferred_element_type=F32)
    gate = _sigmoid(jnp.dot(h.astype(BF16), wgp_ref[...], preferred_element_type=F32))
    o_ref[...] = _ln(DN_ALPHA * h + f_ref[...] + ple * gate, g_ref[...], b_ref[...])


def _layer_out(h, ffn, p2, ple_w, ple_gate_w, ln_g, ln_b):
    t, d = h.shape
    tm = TM_OUT
    tok = lambda i: (i, 0)
    par = lambda i: (0, 0)
    return pl.pallas_call(
        _out_kernel,
        grid=(t // tm,),
        in_specs=[pl.BlockSpec((tm, d), tok), pl.BlockSpec((tm, d), tok),
                  pl.BlockSpec((tm, PLE_DIM), tok),
                  pl.BlockSpec((PLE_DIM, d), par), pl.BlockSpec((d, d), par),
                  pl.BlockSpec((1, d), par), pl.BlockSpec((1, d), par)],
        out_specs=pl.BlockSpec((tm, d), tok),
        out_shape=jax.ShapeDtypeStruct((t, d), F32),
        compiler_params=_cparams(("parallel",)),
        name="ple_ln2",
    )(h, ffn, p2, ple_w, ple_gate_w, ln_g.reshape(1, -1), ln_b.reshape(1, -1))


def _block_ones():
    head = jnp.arange(BRANCH) // HEAD_DIM
    return (head[:, None] == head[None, :]).astype(F32)


def _layer(h, p2, bsz, seq, w_in, rwkv_mu, rwkv_w0, rwkv_w2, rwkv_a0, rwkv_a2, rwkv_g2, rwkv_kk,
           rwkv_ka, rwkv_rk, rwkv_lnx_g, rwkv_lnx_b, s5_lam_re, s5_lam_im, s5_b_re, s5_b_im, s5_c_re,
           s5_c_im, s5_d, s5_log_dt, s5_glu_w, s5_glu_b, fox_bf, conv_w, conv_b, conv_ln_g, conv_ln_b,
           w_branch, w_out, ln1_g, ln1_b, peer_wq, peer_k1, peer_k2, peer_u, peer_v, ple_w, ple_gate_w,
           ln2_g, ln2_b):
    t = bsz * seq
    d = D_MODEL
    n_front = RWKV_COLS + BRANCH + 3 * BRANCH + HEADS
    w_front = w_in[:, :n_front]
    w_conv = w_in[:, n_front:n_front + 2 * BRANCH]
    w_gate = w_in[:, n_front + 2 * BRANCH:]
    pad = jnp.zeros((d, _P_FF[1] - _P_FF[0] - HEADS), w_in.dtype)
    w_packed = jnp.concatenate([w_front, pad, w_conv], axis=1).astype(BF16)
    bd = _block_ones()
    wl = jnp.zeros((LANES, 3 * BRANCH), F32)
    wl = wl.at[0:32, 0:BRANCH].set(rwkv_w2.astype(F32))
    wl = wl.at[32:64, BRANCH:2 * BRANCH].set(rwkv_a2.astype(F32))
    wl = wl.at[64:128, 2 * BRANCH:].set(rwkv_g2.astype(F32))

    p_rwkv, p_s5_t, p_fqkv, p_ff, p_conv = _project(h, w_packed, bsz, seq)

    r, dec, k2, v, na, bb, g, bonus = _rwkv_pre(p_rwkv, rwkv_mu, rwkv_w0, rwkv_a0, rwkv_kk, rwkv_ka,
                                                 rwkv_rk.reshape(-1), wl, bd, bsz, seq)
    y_scan = _rwkv_scan(_to_scan_j(na, bsz, seq), _to_scan_j(dec, bsz, seq), _to_scan_j(bb, bsz, seq),
                        _to_scan_j(k2, bsz, seq), _to_scan_j(r, bsz, seq), _to_scan_i(v, bsz, seq))
    y_rwkv = _rwkv_post(_from_scan_i(y_scan, bsz, seq), g.reshape(t, BRANCH), bonus.reshape(t, BRANCH),
                        rwkv_lnx_g, rwkv_lnx_b, bd)

    bre, bim, lre, lim, cc = _s5_params(s5_lam_re, s5_lam_im, s5_b_re, s5_b_im, s5_c_re, s5_c_im, s5_log_dt)
    y_s5_t = _s5(p_s5_t, bre, bim, lre, lim, cc, s5_d, s5_glu_w, s5_glu_b, bsz, seq)

    f_rows = p_ff[:, :HEADS].reshape(bsz, seq, HEADS).transpose(0, 2, 1).reshape(bsz * HEADS, seq)
    bias_rows = jnp.tile(fox_bf.astype(F32), bsz).reshape(bsz * HEADS, 1)
    c = _fox_cumsum(f_rows, bias_rows)
    to_heads = lambda x: x.reshape(bsz, seq, HEADS, HEAD_DIM).transpose(0, 2, 1, 3)
    o = _fox_attn(to_heads(p_fqkv[:, 0:BRANCH]), to_heads(p_fqkv[:, BRANCH:2 * BRANCH]),
                  to_heads(p_fqkv[:, 2 * BRANCH:]),
                  c.reshape(bsz, HEADS, seq, 1), c.reshape(bsz, HEADS, seq // TK_ATT, TK_ATT))
    y_fox = o.transpose(0, 2, 1, 3).reshape(t, BRANCH)

    y_conv = _conv_mixer(p_conv, conv_w, conv_b, conv_ln_g, conv_ln_b, bsz, seq)

    h1 = _merge(h, y_rwkv, y_s5_t, y_fox, y_conv, w_gate.astype(BF16), w_branch.astype(BF16),
                w_out.astype(BF16), ln1_g, ln1_b, bsz, seq)

    keys = jnp.stack([peer_k1, peer_k2]).astype(F32)
    eid_t, gate_t = _route(h1, peer_wq.astype(BF16), keys)
    ffn = _peer(h1, eid_t.T, gate_t.T, _pack_tables(peer_u, peer_v))

    return _layer_out(h1, ffn, p2, ple_w.astype(BF16), ple_gate_w.astype(BF16), ln2_g, ln2_b)


def kernel(x, p, ln_in_g, ln_in_b, w_in, rwkv_mu, rwkv_w0, rwkv_w2, rwkv_a0, rwkv_a2, rwkv_g2, rwkv_kk, rwkv_ka, rwkv_rk, rwkv_lnx_g, rwkv_lnx_b, s5_lam_re, s5_lam_im, s5_b_re, s5_b_im, s5_c_re, s5_c_im, s5_d, s5_log_dt, s5_glu_w, s5_glu_b, fox_bf, conv_w, conv_b, conv_ln_g, conv_ln_b, w_branch, w_out, ln1_g, ln1_b, peer_wq, peer_k1, peer_k2, peer_u, peer_v, ple_w, ple_gate_w, ln2_g, ln2_b):
    bsz, seq, d = x.shape
    t = bsz * seq
    h = _layer_norm(x.reshape(t, d), ln_in_g, ln_in_b)
    per_layer = (w_in, rwkv_mu, rwkv_w0, rwkv_w2, rwkv_a0, rwkv_a2, rwkv_g2, rwkv_kk, rwkv_ka, rwkv_rk,
                 rwkv_lnx_g, rwkv_lnx_b, s5_lam_re, s5_lam_im, s5_b_re, s5_b_im, s5_c_re, s5_c_im, s5_d,
                 s5_log_dt, s5_glu_w, s5_glu_b, fox_bf, conv_w, conv_b, conv_ln_g, conv_ln_b, w_branch,
                 w_out, ln1_g, ln1_b, peer_wq, peer_k1, peer_k2, peer_u, peer_v, ple_w, ple_gate_w,
                 ln2_g, ln2_b)
    for i in range(p.shape[0]):
        h = _layer(h, p[i].reshape(t, PLE_DIM), bsz, seq, *(w[i] for w in per_layer))
    return h.reshape(bsz, seq, d)
```

```python
import functools
import math

import jax
import jax.numpy as jnp
from jax import lax
from jax.experimental import pallas as pl
from jax.experimental.pallas import tpu as pltpu

F32 = jnp.float32
BF16 = jnp.bfloat16
I32 = jnp.int32
U32 = jnp.uint32
HI = lax.Precision.HIGHEST

D_MODEL = 1024
BRANCH = 256
HEADS = 4
HEAD_DIM = 64
RWKV_COLS = 896
S5_GROUPS = 16
S5_GROUP = 16
S5_STATE = 64
S5_N = S5_GROUPS * S5_STATE
CONV_WIDTH = 31
CONV_HALO = 32
PEER_HEADS = 8
PEER_KEYS = 128
PEER_HALF = 128
PEER_TOPK = 16
PEER_SEL = PEER_HEADS * PEER_TOPK
PLE_DIM = 256
RWKV_GN_EPS = 64e-5
LN_EPS = 1e-5
DEPTH = 2
DN_ALPHA = (2 * DEPTH) ** 0.25

SUBLANES = 8
LANES = 128
VMEM_LIMIT = 56 * 1024 * 1024

TM_PROJ = 512
TQ_PRE = 256
TC_SCAN = 32
TC_S5 = 32
TQ_ATT = 256
TK_ATT = 256
TC_CONV = 512
TM_MERGE = 256
TM_ROUTE = 128
G_PEER = 16
TM_OUT = 256


def _cparams(sem, vmem=None):
    return pltpu.CompilerParams(dimension_semantics=sem, vmem_limit_bytes=vmem)


def _ln(z, g, b):
    mu = jnp.mean(z, axis=-1, keepdims=True)
    zc = z - mu
    var = jnp.mean(zc * zc, axis=-1, keepdims=True)
    return zc * lax.rsqrt(var + LN_EPS) * g + b


def _gelu(y):
    return 0.5 * y * (1.0 + lax.erf(y * (1.0 / math.sqrt(2.0))))


def _sigmoid(y):
    return 1.0 / (1.0 + jnp.exp(-y))


def _softplus(y):
    return jnp.maximum(y, 0.0) + jnp.log(1.0 + jnp.exp(-jnp.abs(y)))


def _ln_kernel(x_ref, g_ref, b_ref, o_ref):
    o_ref[...] = _ln(x_ref[...], g_ref[...], b_ref[...])


def _layer_norm(x2, g, b):
    t, d = x2.shape
    tm = TM_PROJ
    return pl.pallas_call(
        _ln_kernel,
        grid=(t // tm,),
        in_specs=[pl.BlockSpec((tm, d), lambda i: (i, 0)),
                  pl.BlockSpec((1, d), lambda i: (0, 0)),
                  pl.BlockSpec((1, d), lambda i: (0, 0))],
        out_specs=pl.BlockSpec((tm, d), lambda i: (i, 0)),
        out_shape=jax.ShapeDtypeStruct((t, d), F32),
        compiler_params=_cparams(("parallel",)),
        name="ln_in",
    )(x2, g.reshape(1, d), b.reshape(1, d))


_P_RWKV = (0, 896)
_P_S5 = (896, 1152)
_P_FQKV = (1152, 1920)
_P_FF = (1920, 2048)
_P_CONV = (2048, 2560)
_P_COLS = 2560


def _proj_kernel(h_ref, w_ref, rw_ref, s5_ref, fq_ref, ff_ref, cv_ref):
    x = h_ref[...].astype(BF16)
    for (lo, hi), o_ref in ((_P_RWKV, rw_ref), (_P_S5, s5_ref), (_P_FQKV, fq_ref),
                            (_P_FF, ff_ref), (_P_CONV, cv_ref)):
        o_ref[...] = jnp.dot(x, w_ref[:, lo:hi], preferred_element_type=F32)


def _project(h, w_packed, bsz, seq):
    t, d = h.shape
    tm = TM_PROJ
    nst = seq // tm
    widths = [hi - lo for lo, hi in (_P_RWKV, _P_S5, _P_FQKV, _P_FF, _P_CONV)]
    tok = lambda i: (i, 0)
    out_shapes = [jax.ShapeDtypeStruct((t, widths[0]), F32),
                  jax.ShapeDtypeStruct((seq, bsz * widths[1]), F32),
                  jax.ShapeDtypeStruct((t, widths[2]), F32),
                  jax.ShapeDtypeStruct((t, widths[3]), F32),
                  jax.ShapeDtypeStruct((t, widths[4]), F32)]
    out_specs = [pl.BlockSpec((tm, widths[0]), tok),
                 pl.BlockSpec((tm, widths[1]), lambda i: (i % nst, i // nst)),
                 pl.BlockSpec((tm, widths[2]), tok),
                 pl.BlockSpec((tm, widths[3]), tok),
                 pl.BlockSpec((tm, widths[4]), tok)]
    return pl.pallas_call(
        _proj_kernel,
        grid=(t // tm,),
        in_specs=[pl.BlockSpec((tm, d), tok),
                  pl.BlockSpec((d, _P_COLS), lambda i: (0, 0))],
        out_specs=out_specs,
        out_shape=out_shapes,
        compiler_params=_cparams(("parallel",), VMEM_LIMIT),
        name="in_proj",
    )(h, w_packed)


def _rwkv_pre_kernel(z_ref, zp_ref, mu_ref, w0_ref, a0_ref, kk_ref, ka_ref, rk_ref, wl_ref, bd_ref,
                     r_out, w_out, k_out, v_out, na_out, b_out, g_out, bv_out):
    j = pl.program_id(1)
    z = z_ref[0]
    prev = zp_ref[0][SUBLANES - 1:SUBLANES, :]
    prev = jnp.where(j == 0, 0.0, prev)
    row = lax.broadcasted_iota(I32, z.shape, 0)
    zs = jnp.where(row == 0, prev, pltpu.roll(z, 1, 0))
    z = z + (zs - z) * mu_ref[...]
    r = z[:, 0:256]
    k = z[:, 256:512]
    v = z[:, 512:768]
    zc = z[:, 768:896]
    lane = lax.broadcasted_iota(I32, zc.shape, 1)
    act = jnp.where(lane < 32, jnp.tanh(zc), jnp.where(lane < 64, zc, _sigmoid(zc)))
    lo = jnp.dot(act, wl_ref[...], precision=HI, preferred_element_type=F32)
    dw = lo[:, 0:256]
    da = lo[:, 256:512]
    g = lo[:, 512:768]
    w_log = -_softplus(-(w0_ref[...] + dw)) - 0.5
    decay = jnp.exp(-jnp.exp(w_log))
    a = _sigmoid(a0_ref[...] + da)
    kkv = k * kk_ref[...]
    bd = bd_ref[...]
    ss = jnp.dot(kkv * kkv, bd, precision=HI, preferred_element_type=F32)
    kkn = kkv * lax.rsqrt(jnp.maximum(ss, 1e-24))
    k2 = k * (1.0 + (a - 1.0) * ka_ref[...])
    bonus = jnp.dot(r * k2 * rk_ref[...], bd, precision=HI, preferred_element_type=F32) * v
    r_out[0] = r
    w_out[0] = decay
    k_out[0] = k2
    v_out[0] = v
    na_out[0] = -kkn
    b_out[0] = kkn * a
    g_out[0] = g
    bv_out[0] = bonus


def _rwkv_pre(p_rwkv, mu, w0, a0, kk, ka, rk, wl, bd, bsz, seq):
    z3 = p_rwkv.reshape(bsz, seq, RWKV_COLS)
    tq = TQ_PRE
    blk = lambda b, j: (b, j, 0)
    par = lambda b, j: (0, 0)
    out_shape = [jax.ShapeDtypeStruct((bsz, seq, BRANCH), F32)] * 8
    out_specs = [pl.BlockSpec((1, tq, BRANCH), blk)] * 8
    return pl.pallas_call(
        _rwkv_pre_kernel,
        grid=(bsz, seq // tq),
        in_specs=[pl.BlockSpec((1, tq, RWKV_COLS), blk),
                  pl.BlockSpec((1, SUBLANES, RWKV_COLS),
                               lambda b, j: (b, jnp.maximum(j * (tq // SUBLANES) - 1, 0), 0)),
                  pl.BlockSpec((1, RWKV_COLS), par),
                  pl.BlockSpec((1, BRANCH), par), pl.BlockSpec((1, BRANCH), par),
                  pl.BlockSpec((1, BRANCH), par), pl.BlockSpec((1, BRANCH), par),
                  pl.BlockSpec((1, BRANCH), par),
                  pl.BlockSpec((LANES, 3 * BRANCH), par),
                  pl.BlockSpec((BRANCH, BRANCH), par)],
        out_specs=out_specs,
        out_shape=out_shape,
        compiler_params=_cparams(("parallel", "parallel")),
        name="rwkv_pre",
    )(z3, z3, mu.reshape(1, -1), w0.reshape(1, -1), a0.reshape(1, -1), kk.reshape(1, -1),
      ka.reshape(1, -1), rk.reshape(1, -1), wl, bd)


_SCAN_IG = HEAD_DIM // 2 // SUBLANES


def _rwkv_scan_kernel(a_ref, w_ref, b_ref, k_ref, r_ref, v_ref, y_ref, s_ref):
    @pl.when(pl.program_id(0) == 0)
    def _():
        s_ref[...] = jnp.zeros_like(s_ref)

    tc = a_ref.shape[0]
    tile = (SUBLANES, LANES)

    def step(t, carry):
        vv = [v_ref[t, ig * SUBLANES:(ig + 1) * SUBLANES, :] for ig in range(_SCAN_IG)]
        sa = [[jnp.zeros(tile, F32), jnp.zeros(tile, F32)] for _ in range(_SCAN_IG)]
        for j in range(HEAD_DIM):
            ab = jnp.broadcast_to(a_ref[t, j:j + 1, :], tile)
            for ig in range(_SCAN_IG):
                sa[ig][j % 2] = sa[ig][j % 2] + s_ref[ig, j] * ab
        sa = [x[0] + x[1] for x in sa]
        yy = [[jnp.zeros(tile, F32), jnp.zeros(tile, F32)] for _ in range(_SCAN_IG)]
        for j in range(HEAD_DIM):
            wb = jnp.broadcast_to(w_ref[t, j:j + 1, :], tile)
            bb = jnp.broadcast_to(b_ref[t, j:j + 1, :], tile)
            kb = jnp.broadcast_to(k_ref[t, j:j + 1, :], tile)
            rb = jnp.broadcast_to(r_ref[t, j:j + 1, :], tile)
            for ig in range(_SCAN_IG):
                s = s_ref[ig, j] * wb + sa[ig] * bb + vv[ig] * kb
                s_ref[ig, j] = s
                yy[ig][j % 2] = yy[ig][j % 2] + s * rb
        for ig in range(_SCAN_IG):
            y_ref[t, ig * SUBLANES:(ig + 1) * SUBLANES, :] = yy[ig][0] + yy[ig][1]
        return carry

    lax.fori_loop(0, tc, step, 0)


def _rwkv_scan(a_t, w_t, b_t, k_t, r_t, v_t):
    seq = a_t.shape[0]
    tc = TC_SCAN
    jspec = pl.BlockSpec((tc, HEAD_DIM, LANES), lambda i: (i, 0, 0))
    ispec = pl.BlockSpec((tc, HEAD_DIM // 2, LANES), lambda i: (i, 0, 0))
    return pl.pallas_call(
        _rwkv_scan_kernel,
        grid=(seq // tc,),
        in_specs=[jspec, jspec, jspec, jspec, jspec, ispec],
        out_specs=ispec,
        out_shape=jax.ShapeDtypeStruct((seq, HEAD_DIM // 2, LANES), F32),
        scratch_shapes=[pltpu.VMEM((_SCAN_IG, HEAD_DIM, SUBLANES, LANES), F32)],
        compiler_params=_cparams(("arbitrary",)),
        name="rwkv_scan",
    )(a_t, w_t, b_t, k_t, r_t, v_t)


def _to_scan_j(x, bsz, seq):
    n_pairs = bsz * HEADS
    xt = x.reshape(bsz, seq, HEADS, HEAD_DIM).transpose(1, 3, 0, 2).reshape(seq, HEAD_DIM, n_pairs)
    pad = LANES // 2 - n_pairs
    if pad:
        xt = jnp.pad(xt, ((0, 0), (0, 0), (0, pad)))
    return jnp.concatenate([xt, xt], axis=-1)


def _to_scan_i(x, bsz, seq):
    n_pairs = bsz * HEADS
    xt = x.reshape(bsz, seq, HEADS, 2, HEAD_DIM // 2).transpose(1, 4, 3, 0, 2)
    xt = xt.reshape(seq, HEAD_DIM // 2, 2, n_pairs)
    pad = LANES // 2 - n_pairs
    if pad:
        xt = jnp.pad(xt, ((0, 0), (0, 0), (0, 0), (0, pad)))
    return xt.reshape(seq, HEAD_DIM // 2, LANES)


def _from_scan_i(y, bsz, seq):
    n_pairs = bsz * HEADS
    yt = y.reshape(seq, HEAD_DIM // 2, 2, LANES // 2)[..., :n_pairs]
    yt = yt.reshape(seq, HEAD_DIM // 2, 2, bsz, HEADS).transpose(3, 0, 4, 2, 1)
    return yt.reshape(bsz * seq, BRANCH)


def _rwkv_post_kernel(y_ref, g_ref, bv_ref, gg_ref, gb_ref, bd_ref, o_ref):
    y = y_ref[...]
    bd = bd_ref[...]
    inv = 1.0 / HEAD_DIM
    mean = jnp.dot(y, bd, precision=HI, preferred_element_type=F32) * inv
    yc = y - mean
    var = jnp.dot(yc * yc, bd, precision=HI, preferred_element_type=F32) * inv
    yn = yc * lax.rsqrt(var + RWKV_GN_EPS) * gg_ref[...] + gb_ref[...]
    o_ref[...] = (yn + bv_ref[...]) * g_ref[...]


def _rwkv_post(y, g, bv, gn_g, gn_b, bd):
    t = y.shape[0]
    tm = TM_PROJ
    tok = lambda i: (i, 0)
    par = lambda i: (0, 0)
    return pl.pallas_call(
        _rwkv_post_kernel,
        grid=(t // tm,),
        in_specs=[pl.BlockSpec((tm, BRANCH), tok)] * 3
                 + [pl.BlockSpec((1, BRANCH), par)] * 2 + [pl.BlockSpec((BRANCH, BRANCH), par)],
        out_specs=pl.BlockSpec((tm, BRANCH), tok),
        out_shape=jax.ShapeDtypeStruct((t, BRANCH), F32),
        compiler_params=_cparams(("parallel",)),
        name="rwkv_post",
    )(y, g, bv, gn_g.reshape(1, -1), gn_b.reshape(1, -1), bd)


def _s5_kernel(u_ref, bre_ref, bim_ref, lre_ref, lim_ref, cc_ref, d_ref, gw_ref, gb_ref, o_ref,
               xre_ref, xim_ref, bure_ref, buim_ref, xs_ref):
    @pl.when(pl.program_id(0) == 0)
    def _():
        xre_ref[...] = jnp.zeros_like(xre_ref)
        xim_ref[...] = jnp.zeros_like(xim_ref)

    tc, nb, w = u_ref.shape
    u2 = u_ref[...].reshape(tc * nb, w)
    bure_ref[...] = jnp.dot(u2, bre_ref[...], precision=HI, preferred_element_type=F32)
    buim_ref[...] = jnp.dot(u2, bim_ref[...], precision=HI, preferred_element_type=F32)
    lre = jnp.broadcast_to(lre_ref[...], (nb, S5_N))
    lim = jnp.broadcast_to(lim_ref[...], (nb, S5_N))

    def step(t, carry):
        xr, xi = carry
        r0 = pl.multiple_of(t * nb, nb)
        nr = lre * xr - lim * xi + bure_ref[pl.ds(r0, nb), :]
        ni = lre * xi + lim * xr + buim_ref[pl.ds(r0, nb), :]
        xs_ref[pl.ds(r0, nb), 0:S5_N] = nr
        xs_ref[pl.ds(r0, nb), S5_N:2 * S5_N] = ni
        return nr, ni

    xr, xi = lax.fori_loop(0, tc, step, (xre_ref[...], xim_ref[...]))
    xre_ref[...] = xr
    xim_ref[...] = xi
    y = jnp.dot(xs_ref[...], cc_ref[...], precision=HI, preferred_element_type=F32) + d_ref[...] * u2
    y = _gelu(y)
    gate = jnp.dot(y, gw_ref[...], precision=HI, preferred_element_type=F32) + gb_ref[...]
    o_ref[...] = (y * _sigmoid(gate)).reshape(tc, nb, w)


def _s5(u_t, bre, bim, lre, lim, cc, d_skip, glu_w, glu_b, bsz, seq):
    u3 = u_t.reshape(seq, bsz, BRANCH)
    tc = TC_S5
    par = lambda i: (0, 0)
    blk = pl.BlockSpec((tc, bsz, BRANCH), lambda i: (i, 0, 0))
    out = pl.pallas_call(
        _s5_kernel,
        grid=(seq // tc,),
        in_specs=[blk,
                  pl.BlockSpec((BRANCH, S5_N), par), pl.BlockSpec((BRANCH, S5_N), par),
                  pl.BlockSpec((1, S5_N), par), pl.BlockSpec((1, S5_N), par),
                  pl.BlockSpec((2 * S5_N, BRANCH), par),
                  pl.BlockSpec((1, BRANCH), par),
                  pl.BlockSpec((BRANCH, BRANCH), par),
                  pl.BlockSpec((1, BRANCH), par)],
        out_specs=blk,
        out_shape=jax.ShapeDtypeStruct((seq, bsz, BRANCH), F32),
        scratch_shapes=[pltpu.VMEM((bsz, S5_N), F32), pltpu.VMEM((bsz, S5_N), F32),
                        pltpu.VMEM((tc * bsz, S5_N), F32), pltpu.VMEM((tc * bsz, S5_N), F32),
                        pltpu.VMEM((tc * bsz, 2 * S5_N), F32)],
        compiler_params=_cparams(("arbitrary",), VMEM_LIMIT),
        name="s5_scan",
    )(u3, bre, bim, lre, lim, cc, d_skip.reshape(1, -1), glu_w, glu_b.reshape(1, -1))
    return out.reshape(seq, bsz * BRANCH)


def _s5_params(lam_re, lam_im, b_re, b_im, c_re, c_im, log_dt):
    lam = lax.complex(lam_re.astype(F32), lam_im.astype(F32))
    dt = jnp.exp(log_dt.astype(F32))[:, None]
    lam_bar = jnp.exp(lam * dt)
    b_bar = ((lam_bar - 1.0) / lam)[..., None] * lax.complex(b_re.astype(F32), b_im.astype(F32))
    eye = jnp.eye(S5_GROUPS, dtype=F32)
    bre = jnp.einsum('gpc,gh->gchp', jnp.real(b_bar), eye).reshape(BRANCH, S5_N)
    bim = jnp.einsum('gpc,gh->gchp', jnp.imag(b_bar), eye).reshape(BRANCH, S5_N)
    cre = jnp.einsum('gcp,gh->gphc', c_re.astype(F32), eye).reshape(S5_N, BRANCH)
    cim = jnp.einsum('gcp,gh->gphc', c_im.astype(F32), eye).reshape(S5_N, BRANCH)
    cc = jnp.concatenate([cre, -cim], axis=0)
    return bre, bim, jnp.real(lam_bar).reshape(1, S5_N), jnp.imag(lam_bar).reshape(1, S5_N), cc


def _fox_cumsum_kernel(f_ref, bias_ref, c_ref):
    x = f_ref[...] + bias_ref[...]
    c = -_softplus(-x)
    n = c.shape[1]
    lane = lax.broadcasted_iota(I32, c.shape, 1)
    sh = 1
    while sh < n:
        c = c + jnp.where(lane >= sh, pltpu.roll(c, sh, 1), 0.0)
        sh *= 2
    c_ref[...] = c


def _fox_cumsum(f_rows, bias_rows):
    rows, seq = f_rows.shape
    return pl.pallas_call(
        _fox_cumsum_kernel,
        out_shape=jax.ShapeDtypeStruct((rows, seq), F32),
        name="fox_cumsum",
    )(f_rows, bias_rows)


def _fox_attn_kernel(q_ref, k_ref, v_ref, cq_ref, ck_ref, o_ref):
    qi = pl.program_id(2)
    tq = q_ref.shape[2]
    tk = ck_ref.shape[3]
    q = (q_ref[0, 0] * (HEAD_DIM ** -0.5)).astype(BF16)
    cq = cq_ref[0, 0]
    qpos = qi * tq + lax.broadcasted_iota(I32, (tq, tk), 0)
    nkb = (qi * tq + tq + tk - 1) // tk

    def body(kb, carry):
        m, l, acc = carry
        k0 = pl.multiple_of(kb * tk, tk)
        kblk = k_ref[0, 0, pl.ds(k0, tk), :].astype(BF16)
        vblk = v_ref[0, 0, pl.ds(k0, tk), :].astype(BF16)
        s = lax.dot_general(q, kblk, (((1,), (1,)), ((), ())), preferred_element_type=F32)
        s = s + cq - ck_ref[0, 0, pl.ds(kb, 1), :]
        kpos = k0 + lax.broadcasted_iota(I32, (tq, tk), 1)
        s = jnp.where(kpos <= qpos, s, -1e30)
        m_new = jnp.maximum(m, jnp.max(s, axis=-1, keepdims=True))
        alpha = jnp.exp(m - m_new)
        p = jnp.exp(s - m_new)
        l = alpha * l + jnp.sum(p, axis=-1, keepdims=True)
        acc = alpha * acc + jnp.dot(p.astype(BF16), vblk, preferred_element_type=F32)
        return m_new, l, acc

    init = (jnp.full((tq, 1), -1e30, F32), jnp.zeros((tq, 1), F32), jnp.zeros((tq, HEAD_DIM), F32))
    m, l, acc = lax.fori_loop(0, nkb, body, init)
    o_ref[0, 0] = acc / l


def _fox_attn(q, k, v, c_col, c_rowb):
    bsz, nh, seq, hd = q.shape
    tq = TQ_ATT
    nkb, tk = c_rowb.shape[2], c_rowb.shape[3]
    full = lambda b, h, i: (b, h, 0, 0)
    return pl.pallas_call(
        _fox_attn_kernel,
        grid=(bsz, nh, seq // tq),
        in_specs=[pl.BlockSpec((1, 1, tq, hd), lambda b, h, i: (b, h, i, 0)),
                  pl.BlockSpec((1, 1, seq, hd), full),
                  pl.BlockSpec((1, 1, seq, hd), full),
                  pl.BlockSpec((1, 1, tq, 1), lambda b, h, i: (b, h, i, 0)),
                  pl.BlockSpec((1, 1, nkb, tk), full)],
        out_specs=pl.BlockSpec((1, 1, tq, hd), lambda b, h, i: (b, h, i, 0)),
        out_shape=jax.ShapeDtypeStruct((bsz, nh, seq, hd), F32),
        compiler_params=_cparams(("parallel", "parallel", "parallel")),
        name="fox_attn",
    )(q, k, v, c_col, c_rowb)


def _conv_kernel(x_ref, xp_ref, w_ref, b_ref, g_ref, be_ref, o_ref, buf_ref):
    j = pl.program_id(1)
    tc = x_ref.shape[1]
    x = x_ref[0]
    buf_ref[CONV_HALO:CONV_HALO + tc, :] = x[:, 0:BRANCH] * _sigmoid(x[:, BRANCH:2 * BRANCH])
    xp = xp_ref[0]
    hp = xp[:, 0:BRANCH] * _sigmoid(xp[:, BRANCH:2 * BRANCH])
    buf_ref[0:CONV_HALO, :] = jnp.where(j == 0, 0.0, hp)
    acc = jnp.zeros((tc, BRANCH), F32) + b_ref[...]
    off = CONV_HALO - (CONV_WIDTH - 1)
    for kk in range(CONV_WIDTH):
        acc = acc + buf_ref[off + kk:off + kk + tc, :] * w_ref[kk:kk + 1, :]
    y = _ln(acc, g_ref[...], be_ref[...])
    o_ref[0] = y * _sigmoid(y)


def _conv_mixer(p_conv, conv_w, conv_b, ln_g, ln_b, bsz, seq):
    x3 = p_conv.reshape(bsz, seq, 2 * BRANCH)
    tc = TC_CONV
    par = lambda b, j: (0, 0)
    out = pl.pallas_call(
        _conv_kernel,
        grid=(bsz, seq // tc),
        in_specs=[pl.BlockSpec((1, tc, 2 * BRANCH), lambda b, j: (b, j, 0)),
                  pl.BlockSpec((1, CONV_HALO, 2 * BRANCH),
                               lambda b, j: (b, jnp.maximum(j * (tc // CONV_HALO) - 1, 0), 0)),
                  pl.BlockSpec((CONV_WIDTH, BRANCH), par),
                  pl.BlockSpec((1, BRANCH), par), pl.BlockSpec((1, BRANCH), par),
                  pl.BlockSpec((1, BRANCH), par)],
        out_specs=pl.BlockSpec((1, tc, BRANCH), lambda b, j: (b, j, 0)),
        out_shape=jax.ShapeDtypeStruct((bsz, seq, BRANCH), F32),
        scratch_shapes=[pltpu.VMEM((CONV_HALO + tc, BRANCH), F32)],
        compiler_params=_cparams(("parallel", "parallel")),
        name="conv_mixer",
    )(x3, x3, conv_w, conv_b.reshape(1, -1), ln_g.reshape(1, -1), ln_b.reshape(1, -1))
    return out.reshape(bsz * seq, BRANCH)


def _merge_kernel(h_ref, y0_ref, y1_ref, y2_ref, y3_ref, wg_ref, wb_ref, wo_ref, g_ref, b_ref, o_ref):
    h = h_ref[...]
    hb = h.astype(BF16)
    merged = None
    for br, y_ref in enumerate((y0_ref, y1_ref, y2_ref, y3_ref)):
        gate = _sigmoid(jnp.dot(hb, wg_ref[:, br * D_MODEL:(br + 1) * D_MODEL], preferred_element_type=F32))
        term = gate * jnp.dot(y_ref[...].astype(BF16), wb_ref[br], preferred_element_type=F32)
        merged = term if merged is None else merged + term
    mix = jnp.dot(merged.astype(BF16), wo_ref[...], preferred_element_type=F32)
    o_ref[...] = _ln(DN_ALPHA * h + mix, g_ref[...], b_ref[...])


def _merge(h, y_rwkv, y_s5_t, y_fox, y_conv, wg, wb, wo, ln_g, ln_b, bsz, seq):
    t, d = h.shape
    tm = TM_MERGE
    nst = seq // tm
    tok = lambda i: (i, 0)
    par2 = lambda i: (0, 0)
    br_spec = pl.BlockSpec((tm, BRANCH), tok)
    return pl.pallas_call(
        _merge_kernel,
        grid=(t // tm,),
        in_specs=[pl.BlockSpec((tm, d), tok),
                  br_spec,
                  pl.BlockSpec((tm, BRANCH), lambda i: (i % nst, i // nst)),
                  br_spec, br_spec,
                  pl.BlockSpec((d, 4 * d), par2),
                  pl.BlockSpec((4, BRANCH, d), lambda i: (0, 0, 0)),
                  pl.BlockSpec((d, d), par2),
                  pl.BlockSpec((1, d), par2), pl.BlockSpec((1, d), par2)],
        out_specs=pl.BlockSpec((tm, d), tok),
        out_shape=jax.ShapeDtypeStruct((t, d), F32),
        compiler_params=_cparams(("parallel",), VMEM_LIMIT),
        name="merge_ln1",
    )(h, y_rwkv, y_s5_t, y_fox, y_conv, wg, wb, wo, ln_g.reshape(1, -1), ln_b.reshape(1, -1))


def _top_rows(s, key, payload=None):
    vals, keys, pays = [], [], []
    for _ in range(PEER_TOPK):
        m = jnp.max(s, axis=0, keepdims=True)
        kmin = jnp.min(jnp.where(s == m, key, jnp.int32(2 ** 30)), axis=0, keepdims=True)
        sel = key == kmin
        if payload is not None:
            pays.append(jnp.max(jnp.where(sel, payload, -1), axis=0, keepdims=True))
        s = jnp.where(sel, -jnp.inf, s)
        vals.append(m)
        keys.append(kmin)
    return vals, keys, pays


_CAND_ROWS = {2: 5, 3: 4, 4: 3, 5: 2, 6: 2, 7: 2}


def _candidates(v1, v2, i1, i2):
    tm = v1.shape[1]
    sub = lax.broadcasted_iota(I32, (SUBLANES, tm), 0)
    bc = lambda x, a: jnp.broadcast_to(x[a:a + 1, :], (SUBLANES, tm))
    vals, keys, eids = [], [], []
    for a, b0 in ((0, 0), (0, SUBLANES), (1, 0)):
        vals.append(bc(v1, a) + v2[b0:b0 + SUBLANES, :])
        keys.append(a * PEER_TOPK + b0 + sub)
        eids.append(bc(i1, a) * PEER_KEYS + i2[b0:b0 + SUBLANES, :])
    for a, nb in _CAND_ROWS.items():
        vals.append(jnp.where(sub < nb, bc(v1, a) + v2[0:SUBLANES, :], -jnp.inf))
        keys.append(a * PEER_TOPK + sub)
        eids.append(bc(i1, a) * PEER_KEYS + i2[0:SUBLANES, :])
    vals.append(v1[SUBLANES:, :] + bc(v2, 0))
    keys.append((SUBLANES + sub) * PEER_TOPK)
    eids.append(i1[SUBLANES:, :] * PEER_KEYS + bc(i2, 0))
    return jnp.concatenate(vals, axis=0), jnp.concatenate(keys, axis=0), jnp.concatenate(eids, axis=0)


_ROUTE_HEADS_PER_ITER = 2


def _route_kernel(h_ref, wq_ref, keys_ref, eid_ref, gate_ref, q_scr):
    tm = h_ref.shape[0]
    q = jnp.dot(h_ref[...].astype(BF16), wq_ref[...], preferred_element_type=F32)
    for c in range(2 * PEER_HEADS):
        q_scr[c] = q[:, c * PEER_HALF:(c + 1) * PEER_HALF]
    row = lax.broadcasted_iota(I32, (PEER_KEYS, tm), 0)

    def body(it, carry):
        for off in range(_ROUTE_HEADS_PER_ITER):
            hh = it * _ROUTE_HEADS_PER_ITER + off
            tops = []
            for half in range(2):
                st = lax.dot_general(keys_ref[half], q_scr[2 * hh + half], (((1,), (1,)), ((), ())),
                                     precision=HI, preferred_element_type=F32)
                vals, idxs, _ = _top_rows(st, row)
                tops.append((jnp.concatenate(vals, axis=0), jnp.concatenate(idxs, axis=0)))
            (v1, i1), (v2, i2) = tops
            cand, key, eids = _candidates(v1, v2, i1, i2)
            vals, _, pays = _top_rows(cand, key, payload=eids)
            score = jnp.concatenate(vals, axis=0)
            e = jnp.exp(score - jnp.max(score, axis=0, keepdims=True))
            r0 = pl.multiple_of(hh * PEER_TOPK, PEER_TOPK)
            gate_ref[pl.ds(r0, PEER_TOPK), :] = e / jnp.sum(e, axis=0, keepdims=True)
            eid_ref[pl.ds(r0, PEER_TOPK), :] = jnp.concatenate(pays, axis=0)
        return carry

    lax.fori_loop(0, PEER_HEADS // _ROUTE_HEADS_PER_ITER, body, 0)


def _route(h, wq, keys):
    t, d = h.shape
    tm = TM_ROUTE
    nq = 2 * PEER_HEADS
    return pl.pallas_call(
        _route_kernel,
        grid=(t // tm,),
        in_specs=[pl.BlockSpec((tm, d), lambda i: (i, 0)),
                  pl.BlockSpec((d, nq * PEER_HALF), lambda i: (0, 0)),
                  pl.BlockSpec((2, PEER_KEYS, PEER_HALF), lambda i: (0, 0, 0))],
        out_specs=[pl.BlockSpec((PEER_SEL, tm), lambda i: (0, i)),
                   pl.BlockSpec((PEER_SEL, tm), lambda i: (0, i))],
        out_shape=[jax.ShapeDtypeStruct((PEER_SEL, t), I32),
                   jax.ShapeDtypeStruct((PEER_SEL, t), F32)],
        scratch_shapes=[pltpu.VMEM((nq, tm, PEER_HALF), F32)],
        compiler_params=_cparams(("parallel",), VMEM_LIMIT),
        name="peer_route",
    )(h, wq, keys)


def _peer_kernel(eid_cur_ref, eid_nxt_ref, x_ref, gate_ref, cmp_ref, exp_ref, uv_hbm, o_ref, buf, sem):
    step = pl.program_id(0)
    nsteps = pl.num_programs(0)
    slot = step % 2
    g_tok = x_ref.shape[0]
    rows = g_tok * PEER_SEL
    nsub = D_MODEL // LANES

    def issue(idx_ref, sl):
        def body(r, carry):
            pltpu.make_async_copy(uv_hbm.at[idx_ref[0, 0, r]], buf.at[sl, r], sem.at[sl]).start()
            return carry
        lax.fori_loop(0, rows, body, 0, unroll=8)

    @pl.when(step == 0)
    def _():
        issue(eid_cur_ref, 0)

    @pl.when(step + 1 < nsteps)
    def _():
        issue(eid_nxt_ref, 1 - slot)

    pltpu.make_async_copy(uv_hbm.at[pl.ds(0, rows)], buf.at[slot], sem.at[slot]).wait()

    def tiles(g):
        return buf[slot, g * PEER_SEL:(g + 1) * PEER_SEL].reshape(PEER_SEL * nsub, LANES)

    diag = (lax.broadcasted_iota(I32, (nsub, PEER_SEL * nsub), 1) % nsub
            == lax.broadcasted_iota(I32, (nsub, PEER_SEL * nsub), 0))
    parts = []
    for g in range(g_tok):
        u_rows = pltpu.bitcast(tiles(g) << 16, F32).astype(BF16)
        q = lax.dot_general(x_ref[g].astype(BF16), u_rows, (((1,), (1,)), ((), ())),
                            preferred_element_type=F32)
        parts.append(jnp.where(diag, q, 0.0))
    part = jnp.dot(jnp.concatenate(parts, axis=0), cmp_ref[...], precision=HI,
                   preferred_element_type=F32)
    act = jnp.sum(part.reshape(g_tok, nsub, PEER_SEL), axis=1)
    wgt = gate_ref[...] * _gelu(act)
    wrep = jnp.dot(wgt, exp_ref[...], precision=HI, preferred_element_type=F32)
    for g in range(g_tok):
        v_rows = pltpu.bitcast(tiles(g) & jnp.uint32(0xFFFF0000), F32).astype(BF16)
        wexp = jnp.where(diag, jnp.broadcast_to(wrep[g:g + 1, :], diag.shape), 0.0).astype(BF16)
        o_ref[g] = jnp.dot(wexp, v_rows, preferred_element_type=F32)


def _peer(h, eid_tok, gate_tok, uv_packed):
    t, d = h.shape
    g_tok = G_PEER
    nsteps = t // g_tok
    rows = g_tok * PEER_SEL
    nsub = d // LANES
    eid3 = eid_tok.reshape(nsteps, 1, rows)
    col = jnp.arange(PEER_SEL * nsub) // nsub
    compress = (col[:, None] == jnp.arange(PEER_SEL)[None, :]).astype(F32)
    smem_blk = lambda f: pl.BlockSpec((1, 1, rows), f, memory_space=pltpu.SMEM)
    tile_blk = pl.BlockSpec((g_tok, nsub, LANES), lambda i: (i, 0, 0))
    out = pl.pallas_call(
        _peer_kernel,
        grid=(nsteps,),
        in_specs=[smem_blk(lambda i: (i, 0, 0)),
                  smem_blk(lambda i: (jnp.minimum(i + 1, nsteps - 1), 0, 0)),
                  tile_blk,
                  pl.BlockSpec((g_tok, PEER_SEL), lambda i: (i, 0)),
                  pl.BlockSpec((PEER_SEL * nsub, PEER_SEL), lambda i: (0, 0)),
                  pl.BlockSpec((PEER_SEL, PEER_SEL * nsub), lambda i: (0, 0)),
                  pl.BlockSpec(memory_space=pl.ANY)],
        out_specs=tile_blk,
        out_shape=jax.ShapeDtypeStruct((t, nsub, LANES), F32),
        scratch_shapes=[pltpu.VMEM((2, rows, nsub, LANES), U32), pltpu.SemaphoreType.DMA((2,))],
        compiler_params=_cparams(("arbitrary",), VMEM_LIMIT),
        name="peer_experts",
    )(eid3, eid3, h.reshape(t, nsub, LANES), gate_tok, compress, compress.T,
      uv_packed.reshape(-1, nsub, LANES))
    return out.reshape(t, d)


def _pack_tables(u_tab, v_tab):
    ub = lax.bitcast_convert_type(u_tab.astype(BF16), jnp.uint16).astype(U32)
    vb = lax.bitcast_convert_type(v_tab.astype(BF16), jnp.uint16).astype(U32)
    return ub | (vb << 16)


def _out_kernel(h_ref, f_ref, p_ref, wp_ref, wgp_ref, g_ref, b_ref, o_ref):
    h = h_ref[...]
    ple = jnp.dot(p_ref[...].astype(BF16), wp_ref[...], preferred_element_type=F32)
    gate = _sigmoid(jnp.dot(h.astype(BF16), wgp_ref[...], preferred_element_type=F32))
    o_ref[...] = _ln(DN_ALPHA * h + f_ref[...] + ple * gate, g_ref[...], b_ref[...])


def _layer_out(h, ffn, p2, ple_w, ple_gate_w, ln_g, ln_b):
    t, d = h.shape
    tm = TM_OUT
    tok = lambda i: (i, 0)
    par = lambda i: (0, 0)
    return pl.pallas_call(
        _out_kernel,
        grid=(t // tm,),
        in_specs=[pl.BlockSpec((tm, d), tok), pl.BlockSpec((tm, d), tok),
                  pl.BlockSpec((tm, PLE_DIM), tok),
                  pl.BlockSpec((PLE_DIM, d), par), pl.BlockSpec((d, d), par),
                  pl.BlockSpec((1, d), par), pl.BlockSpec((1, d), par)],
        out_specs=pl.BlockSpec((tm, d), tok),
        out_shape=jax.ShapeDtypeStruct((t, d), F32),
        compiler_params=_cparams(("parallel",)),
        name="ple_ln2",
    )(h, ffn, p2, ple_w, ple_gate_w, ln_g.reshape(1, -1), ln_b.reshape(1, -1))


def _block_ones():
    head = jnp.arange(BRANCH) // HEAD_DIM
    return (head[:, None] == head[None, :]).astype(F32)


def _layer(h, p2, bsz, seq, w_in, rwkv_mu, rwkv_w0, rwkv_w2, rwkv_a0, rwkv_a2, rwkv_g2, rwkv_kk,
           rwkv_ka, rwkv_rk, rwkv_lnx_g, rwkv_lnx_b, s5_lam_re, s5_lam_im, s5_b_re, s5_b_im, s5_c_re,
           s5_c_im, s5_d, s5_log_dt, s5_glu_w, s5_glu_b, fox_bf, conv_w, conv_b, conv_ln_g, conv_ln_b,
           w_branch, w_out, ln1_g, ln1_b, peer_wq, peer_k1, peer_k2, peer_u, peer_v, ple_w, ple_gate_w,
           ln2_g, ln2_b):
    t = bsz * seq
    d = D_MODEL
    n_front = RWKV_COLS + BRANCH + 3 * BRANCH + HEADS
    w_front = w_in[:, :n_front]
    w_conv = w_in[:, n_front:n_front + 2 * BRANCH]
    w_gate = w_in[:, n_front + 2 * BRANCH:]
    pad = jnp.zeros((d, _P_FF[1] - _P_FF[0] - HEADS), w_in.dtype)
    w_packed = jnp.concatenate([w_front, pad, w_conv], axis=1).astype(BF16)
    bd = _block_ones()
    wl = jnp.zeros((LANES, 3 * BRANCH), F32)
    wl = wl.at[0:32, 0:BRANCH].set(rwkv_w2.astype(F32))
    wl = wl.at[32:64, BRANCH:2 * BRANCH].set(rwkv_a2.astype(F32))
    wl = wl.at[64:128, 2 * BRANCH:].set(rwkv_g2.astype(F32))

    p_rwkv, p_s5_t, p_fqkv, p_ff, p_conv = _project(h, w_packed, bsz, seq)

    r, dec, k2, v, na, bb, g, bonus = _rwkv_pre(p_rwkv, rwkv_mu, rwkv_w0, rwkv_a0, rwkv_kk, rwkv_ka,
                                                 rwkv_rk.reshape(-1), wl, bd, bsz, seq)
    y_scan = _rwkv_scan(_to_scan_j(na, bsz, seq), _to_scan_j(dec, bsz, seq), _to_scan_j(bb, bsz, seq),
                        _to_scan_j(k2, bsz, seq), _to_scan_j(r, bsz, seq), _to_scan_i(v, bsz, seq))
    y_rwkv = _rwkv_post(_from_scan_i(y_scan, bsz, seq), g.reshape(t, BRANCH), bonus.reshape(t, BRANCH),
                        rwkv_lnx_g, rwkv_lnx_b, bd)

    bre, bim, lre, lim, cc = _s5_params(s5_lam_re, s5_lam_im, s5_b_re, s5_b_im, s5_c_re, s5_c_im, s5_log_dt)
    y_s5_t = _s5(p_s5_t, bre, bim, lre, lim, cc, s5_d, s5_glu_w, s5_glu_b, bsz, seq)

    f_rows = p_ff[:, :HEADS].reshape(bsz, seq, HEADS).transpose(0, 2, 1).reshape(bsz * HEADS, seq)
    bias_rows = jnp.tile(fox_bf.astype(F32), bsz).reshape(bsz * HEADS, 1)
    c = _fox_cumsum(f_rows, bias_rows)
    to_heads = lambda x: x.reshape(bsz, seq, HEADS, HEAD_DIM).transpose(0, 2, 1, 3)
    o = _fox_attn(to_heads(p_fqkv[:, 0:BRANCH]), to_heads(p_fqkv[:, BRANCH:2 * BRANCH]),
                  to_heads(p_fqkv[:, 2 * BRANCH:]),
                  c.reshape(bsz, HEADS, seq, 1), c.reshape(bsz, HEADS, seq // TK_ATT, TK_ATT))
    y_fox = o.transpose(0, 2, 1, 3).reshape(t, BRANCH)

    y_conv = _conv_mixer(p_conv, conv_w, conv_b, conv_ln_g, conv_ln_b, bsz, seq)

    h1 = _merge(h, y_rwkv, y_s5_t, y_fox, y_conv, w_gate.astype(BF16), w_branch.astype(BF16),
                w_out.astype(BF16), ln1_g, ln1_b, bsz, seq)

    keys = jnp.stack([peer_k1, peer_k2]).astype(F32)
    eid_t, gate_t = _route(h1, peer_wq.astype(BF16), keys)
    ffn = _peer(h1, eid_t.T, gate_t.T, _pack_tables(peer_u, peer_v))

    return _layer_out(h1, ffn, p2, ple_w.astype(BF16), ple_gate_w.astype(BF16), ln2_g, ln2_b)


def kernel(x, p, ln_in_g, ln_in_b, w_in, rwkv_mu, rwkv_w0, rwkv_w2, rwkv_a0, rwkv_a2, rwkv_g2, rwkv_kk, rwkv_ka, rwkv_rk, rwkv_lnx_g, rwkv_lnx_b, s5_lam_re, s5_lam_im, s5_b_re, s5_b_im, s5_c_re, s5_c_im, s5_d, s5_log_dt, s5_glu_w, s5_glu_b, fox_bf, conv_w, conv_b, conv_ln_g, conv_ln_b, w_branch, w_out, ln1_g, ln1_b, peer_wq, peer_k1, peer_k2, peer_u, peer_v, ple_w, ple_gate_w, ln2_g, ln2_b):
    bsz, seq, d = x.shape
    t = bsz * seq
    h = _layer_norm(x.reshape(t, d), ln_in_g, ln_in_b)
    per_layer = (w_in, rwkv_mu, rwkv_w0, rwkv_w2, rwkv_a0, rwkv_a2, rwkv_g2, rwkv_kk, rwkv_ka, rwkv_rk,
                 rwkv_lnx_g, rwkv_lnx_b, s5_lam_re, s5_lam_im, s5_b_re, s5_b_im, s5_c_re, s5_c_im, s5_d,
                 s5_log_dt, s5_glu_w, s5_glu_b, fox_bf, conv_w, conv_b, conv_ln_g, conv_ln_b, w_branch,
                 w_out, ln1_g, ln1_b, peer_wq, peer_k1, peer_k2, peer_u, peer_v, ple_w, ple_gate_w,
                 ln2_g, ln2_b)
    for i in range(p.shape[0]):
        h = _layer(h, p[i].reshape(t, PLE_DIM), bsz, seq, *(w[i] for w in per_layer))
    return h.reshape(bsz, seq, d)
```

```python
import functools
import math

import jax
import jax.numpy as jnp
from jax import lax
from jax.experimental import pallas as pl
from jax.experimental.pallas import tpu as pltpu

F32 = jnp.float32
BF16 = jnp.bfloat16
I32 = jnp.int32
U32 = jnp.uint32
HI = lax.Precision.HIGHEST

D_MODEL = 1024
BRANCH = 256
HEADS = 4
HEAD_DIM = 64
RWKV_COLS = 896
S5_GROUPS = 16
S5_GROUP = 16
S5_STATE = 64
S5_N = S5_GROUPS * S5_STATE
CONV_WIDTH = 31
CONV_HALO = 32
PEER_HEADS = 8
PEER_KEYS = 128
PEER_HALF = 128
PEER_TOPK = 16
PEER_SEL = PEER_HEADS * PEER_TOPK
PLE_DIM = 256
RWKV_GN_EPS = 64e-5
LN_EPS = 1e-5
DEPTH = 2
DN_ALPHA = (2 * DEPTH) ** 0.25

SUBLANES = 8
LANES = 128
VMEM_LIMIT = 56 * 1024 * 1024

TM_PROJ = 512
TQ_PRE = 256
TC_SCAN = 32
TC_S5 = 32
TQ_ATT = 256
TK_ATT = 256
TC_CONV = 512
TM_MERGE = 256
TM_ROUTE = 128
G_PEER = 16
TM_OUT = 256


def _cparams(sem, vmem=None):
    return pltpu.CompilerParams(dimension_semantics=sem, vmem_limit_bytes=vmem)


def _ln(z, g, b):
    mu = jnp.mean(z, axis=-1, keepdims=True)
    zc = z - mu
    var = jnp.mean(zc * zc, axis=-1, keepdims=True)
    return zc * lax.rsqrt(var + LN_EPS) * g + b


def _gelu(y):
    return 0.5 * y * (1.0 + lax.erf(y * (1.0 / math.sqrt(2.0))))


def _sigmoid(y):
    return 1.0 / (1.0 + jnp.exp(-y))


def _softplus(y):
    return jnp.maximum(y, 0.0) + jnp.log(1.0 + jnp.exp(-jnp.abs(y)))


def _ln_kernel(x_ref, g_ref, b_ref, o_ref):
    o_ref[...] = _ln(x_ref[...], g_ref[...], b_ref[...])


def _layer_norm(x2, g, b):
    t, d = x2.shape
    tm = TM_PROJ
    return pl.pallas_call(
        _ln_kernel,
        grid=(t // tm,),
        in_specs=[pl.BlockSpec((tm, d), lambda i: (i, 0)),
                  pl.BlockSpec((1, d), lambda i: (0, 0)),
                  pl.BlockSpec((1, d), lambda i: (0, 0))],
        out_specs=pl.BlockSpec((tm, d), lambda i: (i, 0)),
        out_shape=jax.ShapeDtypeStruct((t, d), F32),
        compiler_params=_cparams(("parallel",)),
        name="ln_in",
    )(x2, g.reshape(1, d), b.reshape(1, d))


_P_RWKV = (0, 896)
_P_S5 = (896, 1152)
_P_FQKV = (1152, 1920)
_P_FF = (1920, 2048)
_P_CONV = (2048, 2560)
_P_COLS = 2560


def _proj_kernel(h_ref, w_ref, rw_ref, s5_ref, fq_ref, ff_ref, cv_ref):
    x = h_ref[...].astype(BF16)
    for (lo, hi), o_ref in ((_P_RWKV, rw_ref), (_P_S5, s5_ref), (_P_FQKV, fq_ref),
                            (_P_FF, ff_ref), (_P_CONV, cv_ref)):
        o_ref[...] = jnp.dot(x, w_ref[:, lo:hi], preferred_element_type=F32)


def _project(h, w_packed, bsz, seq):
    t, d = h.shape
    tm = TM_PROJ
    nst = seq // tm
    widths = [hi - lo for lo, hi in (_P_RWKV, _P_S5, _P_FQKV, _P_FF, _P_CONV)]
    tok = lambda i: (i, 0)
    out_shapes = [jax.ShapeDtypeStruct((t, widths[0]), F32),
                  jax.ShapeDtypeStruct((seq, bsz * widths[1]), F32),
                  jax.ShapeDtypeStruct((t, widths[2]), F32),
                  jax.ShapeDtypeStruct((t, widths[3]), F32),
                  jax.ShapeDtypeStruct((t, widths[4]), F32)]
    out_specs = [pl.BlockSpec((tm, widths[0]), tok),
                 pl.BlockSpec((tm, widths[1]), lambda i: (i % nst, i // nst)),
                 pl.BlockSpec((tm, widths[2]), tok),
                 pl.BlockSpec((tm, widths[3]), tok),
                 pl.BlockSpec((tm, widths[4]), tok)]
    return pl.pallas_call(
        _proj_kernel,
        grid=(t // tm,),
        in_specs=[pl.BlockSpec((tm, d), tok),
                  pl.BlockSpec((d, _P_COLS), lambda i: (0, 0))],
        out_specs=out_specs,
        out_shape=out_shapes,
        compiler_params=_cparams(("parallel",), VMEM_LIMIT),
        name="in_proj",
    )(h, w_packed)


def _rwkv_pre_kernel(z_ref, zp_ref, mu_ref, w0_ref, a0_ref, kk_ref, ka_ref, rk_ref, wl_ref, bd_ref,
                     r_out, w_out, k_out, v_out, na_out, b_out, g_out, bv_out):
    j = pl.program_id(1)
    z = z_ref[0]
    prev = zp_ref[0][SUBLANES - 1:SUBLANES, :]
    prev = jnp.where(j == 0, 0.0, prev)
    row = lax.broadcasted_iota(I32, z.shape, 0)
    zs = jnp.where(row == 0, prev, pltpu.roll(z, 1, 0))
    z = z + (zs - z) * mu_ref[...]
    r = z[:, 0:256]
    k = z[:, 256:512]
    v = z[:, 512:768]
    zc = z[:, 768:896]
    lane = lax.broadcasted_iota(I32, zc.shape, 1)
    act = jnp.where(lane < 32, jnp.tanh(zc), jnp.where(lane < 64, zc, _sigmoid(zc)))
    lo = jnp.dot(act, wl_ref[...], precision=HI, preferred_element_type=F32)
    dw = lo[:, 0:256]
    da = lo[:, 256:512]
    g = lo[:, 512:768]
    w_log = -_softplus(-(w0_ref[...] + dw)) - 0.5
    decay = jnp.exp(-jnp.exp(w_log))
    a = _sigmoid(a0_ref[...] + da)
    kkv = k * kk_ref[...]
    bd = bd_ref[...]
    ss = jnp.dot(kkv * kkv, bd, precision=HI, preferred_element_type=F32)
    kkn = kkv * lax.rsqrt(jnp.maximum(ss, 1e-24))
    k2 = k * (1.0 + (a - 1.0) * ka_ref[...])
    bonus = jnp.dot(r * k2 * rk_ref[...], bd, precision=HI, preferred_element_type=F32) * v
    r_out[0] = r
    w_out[0] = decay
    k_out[0] = k2
    v_out[0] = v
    na_out[0] = -kkn
    b_out[0] = kkn * a
    g_out[0] = g
    bv_out[0] = bonus


def _rwkv_pre(p_rwkv, mu, w0, a0, kk, ka, rk, wl, bd, bsz, seq):
    z3 = p_rwkv.reshape(bsz, seq, RWKV_COLS)
    tq = TQ_PRE
    blk = lambda b, j: (b, j, 0)
    par = lambda b, j: (0, 0)
    out_shape = [jax.ShapeDtypeStruct((bsz, seq, BRANCH), F32)] * 8
    out_specs = [pl.BlockSpec((1, tq, BRANCH), blk)] * 8
    return pl.pallas_call(
        _rwkv_pre_kernel,
        grid=(bsz, seq // tq),
        in_specs=[pl.BlockSpec((1, tq, RWKV_COLS), blk),
                  pl.BlockSpec((1, SUBLANES, RWKV_COLS),
                               lambda b, j: (b, jnp.maximum(j * (tq // SUBLANES) - 1, 0), 0)),
                  pl.BlockSpec((1, RWKV_COLS), par),
                  pl.BlockSpec((1, BRANCH), par), pl.BlockSpec((1, BRANCH), par),
                  pl.BlockSpec((1, BRANCH), par), pl.BlockSpec((1, BRANCH), par),
                  pl.BlockSpec((1, BRANCH), par),
                  pl.BlockSpec((LANES, 3 * BRANCH), par),
                  pl.BlockSpec((BRANCH, BRANCH), par)],
        out_specs=out_specs,
        out_shape=out_shape,
        compiler_params=_cparams(("parallel", "parallel")),
        name="rwkv_pre",
    )(z3, z3, mu.reshape(1, -1), w0.reshape(1, -1), a0.reshape(1, -1), kk.reshape(1, -1),
      ka.reshape(1, -1), rk.reshape(1, -1), wl, bd)


_SCAN_IG = HEAD_DIM // 2 // SUBLANES


def _rwkv_scan_kernel(a_ref, w_ref, b_ref, k_ref, r_ref, v_ref, y_ref, s_ref):
    @pl.when(pl.program_id(0) == 0)
    def _():
        s_ref[...] = jnp.zeros_like(s_ref)

    tc = a_ref.shape[0]
    tile = (SUBLANES, LANES)

    def step(t, carry):
        vv = [v_ref[t, ig * SUBLANES:(ig + 1) * SUBLANES, :] for ig in range(_SCAN_IG)]
        sa = [[jnp.zeros(tile, F32), jnp.zeros(tile, F32)] for _ in range(_SCAN_IG)]
        for j in range(HEAD_DIM):
            ab = jnp.broadcast_to(a_ref[t, j:j + 1, :], tile)
            for ig in range(_SCAN_IG):
                sa[ig][j % 2] = sa[ig][j % 2] + s_ref[ig, j] * ab
        sa = [x[0] + x[1] for x in sa]
        yy = [[jnp.zeros(tile, F32), jnp.zeros(tile, F32)] for _ in range(_SCAN_IG)]
        for j in range(HEAD_DIM):
            wb = jnp.broadcast_to(w_ref[t, j:j + 1, :], tile)
            bb = jnp.broadcast_to(b_ref[t, j:j + 1, :], tile)
            kb = jnp.broadcast_to(k_ref[t, j:j + 1, :], tile)
            rb = jnp.broadcast_to(r_ref[t, j:j + 1, :], tile)
            for ig in range(_SCAN_IG):
                s = s_ref[ig, j] * wb + sa[ig] * bb + vv[ig] * kb
                s_ref[ig, j] = s
                yy[ig][j % 2] = yy[ig][j % 2] + s * rb
        for ig in range(_SCAN_IG):
            y_ref[t, ig * SUBLANES:(ig + 1) * SUBLANES, :] = yy[ig][0] + yy[ig][1]
        return carry

    lax.fori_loop(0, tc, step, 0)


def _rwkv_scan(a_t, w_t, b_t, k_t, r_t, v_t):
    seq = a_t.shape[0]
    tc = TC_SCAN
    jspec = pl.BlockSpec((tc, HEAD_DIM, LANES), lambda i: (i, 0, 0))
    ispec = pl.BlockSpec((tc, HEAD_DIM // 2, LANES), lambda i: (i, 0, 0))
    return pl.pallas_call(
        _rwkv_scan_kernel,
        grid=(seq // tc,),
        in_specs=[jspec, jspec, jspec, jspec, jspec, ispec],
        out_specs=ispec,
        out_shape=jax.ShapeDtypeStruct((seq, HEAD_DIM // 2, LANES), F32),
        scratch_shapes=[pltpu.VMEM((_SCAN_IG, HEAD_DIM, SUBLANES, LANES), F32)],
        compiler_params=_cparams(("arbitrary",)),
        name="rwkv_scan",
    )(a_t, w_t, b_t, k_t, r_t, v_t)


def _to_scan_j(x, bsz, seq):
    n_pairs = bsz * HEADS
    xt = x.reshape(bsz, seq, HEADS, HEAD_DIM).transpose(1, 3, 0, 2).reshape(seq, HEAD_DIM, n_pairs)
    pad = LANES // 2 - n_pairs
    if pad:
        xt = jnp.pad(xt, ((0, 0), (0, 0), (0, pad)))
    return jnp.concatenate([xt, xt], axis=-1)


def _to_scan_i(x, bsz, seq):
    n_pairs = bsz * HEADS
    xt = x.reshape(bsz, seq, HEADS, 2, HEAD_DIM // 2).transpose(1, 4, 3, 0, 2)
    xt = xt.reshape(seq, HEAD_DIM // 2, 2, n_pairs)
    pad = LANES // 2 - n_pairs
    if pad:
        xt = jnp.pad(xt, ((0, 0), (0, 0), (0, 0), (0, pad)))
    return xt.reshape(seq, HEAD_DIM // 2, LANES)


def _from_scan_i(y, bsz, seq):
    n_pairs = bsz * HEADS
    yt = y.reshape(seq, HEAD_DIM // 2, 2, LANES // 2)[..., :n_pairs]
    yt = yt.reshape(seq, HEAD_DIM // 2, 2, bsz, HEADS).transpose(3, 0, 4, 2, 1)
    return yt.reshape(bsz * seq, BRANCH)


def _rwkv_post_kernel(y_ref, g_ref, bv_ref, gg_ref, gb_ref, bd_ref, o_ref):
    y = y_ref[...]
    bd = bd_ref[...]
    inv = 1.0 / HEAD_DIM
    mean = jnp.dot(y, bd, precision=HI, preferred_element_type=F32) * inv
    yc = y - mean
    var = jnp.dot(yc * yc, bd, precision=HI, preferred_element_type=F32) * inv
    yn = yc * lax.rsqrt(var + RWKV_GN_EPS) * gg_ref[...] + gb_ref[...]
    o_ref[...] = (yn + bv_ref[...]) * g_ref[...]


def _rwkv_post(y, g, bv, gn_g, gn_b, bd):
    t = y.shape[0]
    tm = TM_PROJ
    tok = lambda i: (i, 0)
    par = lambda i: (0, 0)
    return pl.pallas_call(
        _rwkv_post_kernel,
        grid=(t // tm,),
        in_specs=[pl.BlockSpec((tm, BRANCH), tok)] * 3
                 + [pl.BlockSpec((1, BRANCH), par)] * 2 + [pl.BlockSpec((BRANCH, BRANCH), par)],
        out_specs=pl.BlockSpec((tm, BRANCH), tok),
        out_shape=jax.ShapeDtypeStruct((t, BRANCH), F32),
        compiler_params=_cparams(("parallel",)),
        name="rwkv_post",
    )(y, g, bv, gn_g.reshape(1, -1), gn_b.reshape(1, -1), bd)


def _s5_kernel(u_ref, bre_ref, bim_ref, lre_ref, lim_ref, cc_ref, d_ref, gw_ref, gb_ref, o_ref,
               xre_ref, xim_ref, bure_ref, buim_ref, xs_ref):
    @pl.when(pl.program_id(0) == 0)
    def _():
        xre_ref[...] = jnp.zeros_like(xre_ref)
        xim_ref[...] = jnp.zeros_like(xim_ref)

    tc, nb, w = u_ref.shape
    u2 = u_ref[...].reshape(tc * nb, w)
    bure_ref[...] = jnp.dot(u2, bre_ref[...], precision=HI, preferred_element_type=F32)
    buim_ref[...] = jnp.dot(u2, bim_ref[...], precision=HI, preferred_element_type=F32)
    lre = jnp.broadcast_to(lre_ref[...], (nb, S5_N))
    lim = jnp.broadcast_to(lim_ref[...], (nb, S5_N))

    def step(t, carry):
        xr, xi = carry
        r0 = pl.multiple_of(t * nb, nb)
        nr = lre * xr - lim * xi + bure_ref[pl.ds(r0, nb), :]
        ni = lre * xi + lim * xr + buim_ref[pl.ds(r0, nb), :]
        xs_ref[pl.ds(r0, nb), 0:S5_N] = nr
        xs_ref[pl.ds(r0, nb), S5_N:2 * S5_N] = ni
        return nr, ni

    xr, xi = lax.fori_loop(0, tc, step, (xre_ref[...], xim_ref[...]))
    xre_ref[...] = xr
    xim_ref[...] = xi
    y = jnp.dot(xs_ref[...], cc_ref[...], precision=HI, preferred_element_type=F32) + d_ref[...] * u2
    y = _gelu(y)
    gate = jnp.dot(y, gw_ref[...], precision=HI, preferred_element_type=F32) + gb_ref[...]
    o_ref[...] = (y * _sigmoid(gate)).reshape(tc, nb, w)


def _s5(u_t, bre, bim, lre, lim, cc, d_skip, glu_w, glu_b, bsz, seq):
    u3 = u_t.reshape(seq, bsz, BRANCH)
    tc = TC_S5
    par = lambda i: (0, 0)
    blk = pl.BlockSpec((tc, bsz, BRANCH), lambda i: (i, 0, 0))
    out = pl.pallas_call(
        _s5_kernel,
        grid=(seq // tc,),
        in_specs=[blk,
                  pl.BlockSpec((BRANCH, S5_N), par), pl.BlockSpec((BRANCH, S5_N), par),
                  pl.BlockSpec((1, S5_N), par), pl.BlockSpec((1, S5_N), par),
                  pl.BlockSpec((2 * S5_N, BRANCH), par),
                  pl.BlockSpec((1, BRANCH), par),
                  pl.BlockSpec((BRANCH, BRANCH), par),
                  pl.BlockSpec((1, BRANCH), par)],
        out_specs=blk,
        out_shape=jax.ShapeDtypeStruct((seq, bsz, BRANCH), F32),
        scratch_shapes=[pltpu.VMEM((bsz, S5_N), F32), pltpu.VMEM((bsz, S5_N), F32),
                        pltpu.VMEM((tc * bsz, S5_N), F32), pltpu.VMEM((tc * bsz, S5_N), F32),
                        pltpu.VMEM((tc * bsz, 2 * S5_N), F32)],
        compiler_params=_cparams(("arbitrary",), VMEM_LIMIT),
        name="s5_scan",
    )(u3, bre, bim, lre, lim, cc, d_skip.reshape(1, -1), glu_w, glu_b.reshape(1, -1))
    return out.reshape(seq, bsz * BRANCH)


def _s5_params(lam_re, lam_im, b_re, b_im, c_re, c_im, log_dt):
    lam = lax.complex(lam_re.astype(F32), lam_im.astype(F32))
    dt = jnp.exp(log_dt.astype(F32))[:, None]
    lam_bar = jnp.exp(lam * dt)
    b_bar = ((lam_bar - 1.0) / lam)[..., None] * lax.complex(b_re.astype(F32), b_im.astype(F32))
    eye = jnp.eye(S5_GROUPS, dtype=F32)
    bre = jnp.einsum('gpc,gh->gchp', jnp.real(b_bar), eye).reshape(BRANCH, S5_N)
    bim = jnp.einsum('gpc,gh->gchp', jnp.imag(b_bar), eye).reshape(BRANCH, S5_N)
    cre = jnp.einsum('gcp,gh->gphc', c_re.astype(F32), eye).reshape(S5_N, BRANCH)
    cim = jnp.einsum('gcp,gh->gphc', c_im.astype(F32), eye).reshape(S5_N, BRANCH)
    cc = jnp.concatenate([cre, -cim], axis=0)
    return bre, bim, jnp.real(lam_bar).reshape(1, S5_N), jnp.imag(lam_bar).reshape(1, S5_N), cc


def _fox_cumsum_kernel(f_ref, bias_ref, c_ref):
    x = f_ref[...] + bias_ref[...]
    c = -_softplus(-x)
    n = c.shape[1]
    lane = lax.broadcasted_iota(I32, c.shape, 1)
    sh = 1
    while sh < n:
        c = c + jnp.where(lane >= sh, pltpu.roll(c, sh, 1), 0.0)
        sh *= 2
    c_ref[...] = c


def _fox_cumsum(f_rows, bias_rows):
    rows, seq = f_rows.shape
    return pl.pallas_call(
        _fox_cumsum_kernel,
        out_shape=jax.ShapeDtypeStruct((rows, seq), F32),
        name="fox_cumsum",
    )(f_rows, bias_rows)


def _fox_attn_kernel(q_ref, k_ref, v_ref, cq_ref, ck_ref, o_ref):
    qi = pl.program_id(2)
    tq = q_ref.shape[2]
    tk = ck_ref.shape[3]
    q = (q_ref[0, 0] * (HEAD_DIM ** -0.5)).astype(BF16)
    cq = cq_ref[0, 0]
    qpos = qi * tq + lax.broadcasted_iota(I32, (tq, tk), 0)
    nkb = (qi * tq + tq + tk - 1) // tk

    def body(kb, carry):
        m, l, acc = carry
        k0 = pl.multiple_of(kb * tk, tk)
        kblk = k_ref[0, 0, pl.ds(k0, tk), :].astype(BF16)
        vblk = v_ref[0, 0, pl.ds(k0, tk), :].astype(BF16)
        s = lax.dot_general(q, kblk, (((1,), (1,)), ((), ())), preferred_element_type=F32)
        s = s + cq - ck_ref[0, 0, pl.ds(kb, 1), :]
        kpos = k0 + lax.broadcasted_iota(I32, (tq, tk), 1)
        s = jnp.where(kpos <= qpos, s, -1e30)
        m_new = jnp.maximum(m, jnp.max(s, axis=-1, keepdims=True))
        alpha = jnp.exp(m - m_new)
        p = jnp.exp(s - m_new)
        l = alpha * l + jnp.sum(p, axis=-1, keepdims=True)
        acc = alpha * acc + jnp.dot(p.astype(BF16), vblk, preferred_element_type=F32)
        return m_new, l, acc

    init = (jnp.full((tq, 1), -1e30, F32), jnp.zeros((tq, 1), F32), jnp.zeros((tq, HEAD_DIM), F32))
    m, l, acc = lax.fori_loop(0, nkb, body, init)
    o_ref[0, 0] = acc / l


def _fox_attn(q, k, v, c_col, c_rowb):
    bsz, nh, seq, hd = q.shape
    tq = TQ_ATT
    nkb, tk = c_rowb.shape[2], c_rowb.shape[3]
    full = lambda b, h, i: (b, h, 0, 0)
    return pl.pallas_call(
        _fox_attn_kernel,
        grid=(bsz, nh, seq // tq),
        in_specs=[pl.BlockSpec((1, 1, tq, hd), lambda b, h, i: (b, h, i, 0)),
                  pl.BlockSpec((1, 1, seq, hd), full),
                  pl.BlockSpec((1, 1, seq, hd), full),
                  pl.BlockSpec((1, 1, tq, 1), lambda b, h, i: (b, h, i, 0)),
                  pl.BlockSpec((1, 1, nkb, tk), full)],
        out_specs=pl.BlockSpec((1, 1, tq, hd), lambda b, h, i: (b, h, i, 0)),
        out_shape=jax.ShapeDtypeStruct((bsz, nh, seq, hd), F32),
        compiler_params=_cparams(("parallel", "parallel", "parallel")),
        name="fox_attn",
    )(q, k, v, c_col, c_rowb)


def _conv_kernel(x_ref, xp_ref, w_ref, b_ref, g_ref, be_ref, o_ref, buf_ref):
    j = pl.program_id(1)
    tc = x_ref.shape[1]
    x = x_ref[0]
    buf_ref[CONV_HALO:CONV_HALO + tc, :] = x[:, 0:BRANCH] * _sigmoid(x[:, BRANCH:2 * BRANCH])
    xp = xp_ref[0]
    hp = xp[:, 0:BRANCH] * _sigmoid(xp[:, BRANCH:2 * BRANCH])
    buf_ref[0:CONV_HALO, :] = jnp.where(j == 0, 0.0, hp)
    acc = jnp.zeros((tc, BRANCH), F32) + b_ref[...]
    off = CONV_HALO - (CONV_WIDTH - 1)
    for kk in range(CONV_WIDTH):
        acc = acc + buf_ref[off + kk:off + kk + tc, :] * w_ref[kk:kk + 1, :]
    y = _ln(acc, g_ref[...], be_ref[...])
    o_ref[0] = y * _sigmoid(y)


def _conv_mixer(p_conv, conv_w, conv_b, ln_g, ln_b, bsz, seq):
    x3 = p_conv.reshape(bsz, seq, 2 * BRANCH)
    tc = TC_CONV
    par = lambda b, j: (0, 0)
    out = pl.pallas_call(
        _conv_kernel,
        grid=(bsz, seq // tc),
        in_specs=[pl.BlockSpec((1, tc, 2 * BRANCH), lambda b, j: (b, j, 0)),
                  pl.BlockSpec((1, CONV_HALO, 2 * BRANCH),
                               lambda b, j: (b, jnp.maximum(j * (tc // CONV_HALO) - 1, 0), 0)),
                  pl.BlockSpec((CONV_WIDTH, BRANCH), par),
                  pl.BlockSpec((1, BRANCH), par), pl.BlockSpec((1, BRANCH), par),
                  pl.BlockSpec((1, BRANCH), par)],
        out_specs=pl.BlockSpec((1, tc, BRANCH), lambda b, j: (b, j, 0)),
        out_shape=jax.ShapeDtypeStruct((bsz, seq, BRANCH), F32),
        scratch_shapes=[pltpu.VMEM((CONV_HALO + tc, BRANCH), F32)],
        compiler_params=_cparams(("parallel", "parallel")),
        name="conv_mixer",
    )(x3, x3, conv_w, conv_b.reshape(1, -1), ln_g.reshape(1, -1), ln_b.reshape(1, -1))
    return out.reshape(bsz * seq, BRANCH)


def _merge_kernel(h_ref, y0_ref, y1_ref, y2_ref, y3_ref, wg_ref, wb_ref, wo_ref, g_ref, b_ref, o_ref):
    h = h_ref[...]
    hb = h.astype(BF16)
    merged = None
    for br, y_ref in enumerate((y0_ref, y1_ref, y2_ref, y3_ref)):
        gate = _sigmoid(jnp.dot(hb, wg_ref[:, br * D_MODEL:(br + 1) * D_MODEL], preferred_element_type=F32))
        term = gate * jnp.dot(y_ref[...].astype(BF16), wb_ref[br], preferred_element_type=F32)
        merged = term if merged is None else merged + term
    mix = jnp.dot(merged.astype(BF16), wo_ref[...], preferred_element_type=F32)
    o_ref[...] = _ln(DN_ALPHA * h + mix, g_ref[...], b_ref[...])


def _merge(h, y_rwkv, y_s5_t, y_fox, y_conv, wg, wb, wo, ln_g, ln_b, bsz, seq):
    t, d = h.shape
    tm = TM_MERGE
    nst = seq // tm
    tok = lambda i: (i, 0)
    par2 = lambda i: (0, 0)
    br_spec = pl.BlockSpec((tm, BRANCH), tok)
    return pl.pallas_call(
        _merge_kernel,
        grid=(t // tm,),
        in_specs=[pl.BlockSpec((tm, d), tok),
                  br_spec,
                  pl.BlockSpec((tm, BRANCH), lambda i: (i % nst, i // nst)),
                  br_spec, br_spec,
                  pl.BlockSpec((d, 4 * d), par2),
                  pl.BlockSpec((4, BRANCH, d), lambda i: (0, 0, 0)),
                  pl.BlockSpec((d, d), par2),
                  pl.BlockSpec((1, d), par2), pl.BlockSpec((1, d), par2)],
        out_specs=pl.BlockSpec((tm, d), tok),
        out_shape=jax.ShapeDtypeStruct((t, d), F32),
        compiler_params=_cparams(("parallel",), VMEM_LIMIT),
        name="merge_ln1",
    )(h, y_rwkv, y_s5_t, y_fox, y_conv, wg, wb, wo, ln_g.reshape(1, -1), ln_b.reshape(1, -1))


def _top_rows(s, key, payload=None):
    vals, keys, pays = [], [], []
    for _ in range(PEER_TOPK):
        m = jnp.max(s, axis=0, keepdims=True)
        kmin = jnp.min(jnp.where(s == m, key, jnp.int32(2 ** 30)), axis=0, keepdims=True)
        sel = key == kmin
        if payload is not None:
            pays.append(jnp.max(jnp.where(sel, payload, -1), axis=0, keepdims=True))
        s = jnp.where(sel, -jnp.inf, s)
        vals.append(m)
        keys.append(kmin)
    return vals, keys, pays


_CAND_ROWS = {2: 5, 3: 4, 4: 3, 5: 2, 6: 2, 7: 2}


def _candidates(v1, v2, i1, i2):
    tm = v1.shape[1]
    sub = lax.broadcasted_iota(I32, (SUBLANES, tm), 0)
    bc = lambda x, a: jnp.broadcast_to(x[a:a + 1, :], (SUBLANES, tm))
    vals, keys, eids = [], [], []
    for a, b0 in ((0, 0), (0, SUBLANES), (1, 0)):
        vals.append(bc(v1, a) + v2[b0:b0 + SUBLANES, :])
        keys.append(a * PEER_TOPK + b0 + sub)
        eids.append(bc(i1, a) * PEER_KEYS + i2[b0:b0 + SUBLANES, :])
    for a, nb in _CAND_ROWS.items():
        vals.append(jnp.where(sub < nb, bc(v1, a) + v2[0:SUBLANES, :], -jnp.inf))
        keys.append(a * PEER_TOPK + sub)
        eids.append(bc(i1, a) * PEER_KEYS + i2[0:SUBLANES, :])
    vals.append(v1[SUBLANES:, :] + bc(v2, 0))
    keys.append((SUBLANES + sub) * PEER_TOPK)
    eids.append(i1[SUBLANES:, :] * PEER_KEYS + bc(i2, 0))
    return jnp.concatenate(vals, axis=0), jnp.concatenate(keys, axis=0), jnp.concatenate(eids, axis=0)


_ROUTE_HEADS_PER_ITER = 2


def _route_kernel(h_ref, wq_ref, keys_ref, eid_ref, gate_ref, q_scr):
    tm = h_ref.shape[0]
    q = jnp.dot(h_ref[...].astype(BF16), wq_ref[...], preferred_element_type=F32)
    for c in range(2 * PEER_HEADS):
        q_scr[c] = q[:, c * PEER_HALF:(c + 1) * PEER_HALF]
    row = lax.broadcasted_iota(I32, (PEER_KEYS, tm), 0)

    def body(it, carry):
        for off in range(_ROUTE_HEADS_PER_ITER):
            hh = it * _ROUTE_HEADS_PER_ITER + off
            tops = []
            for half in range(2):
                st = lax.dot_general(keys_ref[half], q_scr[2 * hh + half], (((1,), (1,)), ((), ())),
                                     precision=HI, preferred_element_type=F32)
                vals, idxs, _ = _top_rows(st, row)
                tops.append((jnp.concatenate(vals, axis=0), jnp.concatenate(idxs, axis=0)))
            (v1, i1), (v2, i2) = tops
            cand, key, eids = _candidates(v1, v2, i1, i2)
            vals, _, pays = _top_rows(cand, key, payload=eids)
            score = jnp.concatenate(vals, axis=0)
            e = jnp.exp(score - jnp.max(score, axis=0, keepdims=True))
            r0 = pl.multiple_of(hh * PEER_TOPK, PEER_TOPK)
            gate_ref[pl.ds(r0, PEER_TOPK), :] = e / jnp.sum(e, axis=0, keepdims=True)
            eid_ref[pl.ds(r0, PEER_TOPK), :] = jnp.concatenate(pays, axis=0)
        return carry

    lax.fori_loop(0, PEER_HEADS // _ROUTE_HEADS_PER_ITER, body, 0)


def _route(h, wq, keys):
    t, d = h.shape
    tm = TM_ROUTE
    nq = 2 * PEER_HEADS
    return pl.pallas_call(
        _route_kernel,
        grid=(t // tm,),
        in_specs=[pl.BlockSpec((tm, d), lambda i: (i, 0)),
                  pl.BlockSpec((d, nq * PEER_HALF), lambda i: (0, 0)),
                  pl.BlockSpec((2, PEER_KEYS, PEER_HALF), lambda i: (0, 0, 0))],
        out_specs=[pl.BlockSpec((PEER_SEL, tm), lambda i: (0, i)),
                   pl.BlockSpec((PEER_SEL, tm), lambda i: (0, i))],
        out_shape=[jax.ShapeDtypeStruct((PEER_SEL, t), I32),
                   jax.ShapeDtypeStruct((PEER_SEL, t), F32)],
        scratch_shapes=[pltpu.VMEM((nq, tm, PEER_HALF), F32)],
        compiler_params=_cparams(("parallel",), VMEM_LIMIT),
        name="peer_route",
    )(h, wq, keys)


def _peer_kernel(eid_cur_ref, eid_nxt_ref, x_ref, gate_ref, cmp_ref, exp_ref, uv_hbm, o_ref, buf, sem):
    step = pl.program_id(0)
    nsteps = pl.num_programs(0)
    slot = step % 2
    g_tok = x_ref.shape[0]
    rows = g_tok * PEER_SEL
    nsub = D_MODEL // LANES

    other = 1 - slot

    def slot_wait(sl):
        pltpu.make_async_copy(uv_hbm.at[pl.ds(0, rows)], buf.at[sl], sem.at[sl]).wait()

    @pl.when(step == 0)
    def _():
        def body(r, carry):
            pltpu.make_async_copy(uv_hbm.at[eid_cur_ref[0, 0, r]], buf.at[0, r], sem.at[0]).start()
            return carry
        lax.fori_loop(0, rows, body, 0, unroll=8)

    slot_wait(slot)

    def prefetch(r0, n):
        for i in range(n):
            r = r0 + i
            pltpu.make_async_copy(uv_hbm.at[eid_nxt_ref[0, 0, r]], buf.at[other, r],
                                  sem.at[other]).start(priority=i % 2)

    per_piece = rows // (2 * g_tok)

    def tiles(g):
        return buf[slot, g * PEER_SEL:(g + 1) * PEER_SEL].reshape(PEER_SEL * nsub, LANES)

    diag = (lax.broadcasted_iota(I32, (nsub, PEER_SEL * nsub), 1) % nsub
            == lax.broadcasted_iota(I32, (nsub, PEER_SEL * nsub), 0))
    parts = []
    for g in range(g_tok):
        u_rows = pltpu.bitcast(tiles(g) << 16, F32).astype(BF16)
        q = lax.dot_general(x_ref[g].astype(BF16), u_rows, (((1,), (1,)), ((), ())),
                            preferred_element_type=F32)
        parts.append(jnp.where(diag, q, 0.0))
        prefetch(g * per_piece, per_piece)
    part = jnp.dot(jnp.concatenate(parts, axis=0), cmp_ref[...], precision=HI,
                   preferred_element_type=F32)
    act = jnp.sum(part.reshape(g_tok, nsub, PEER_SEL), axis=1)
    wgt = gate_ref[...] * _gelu(act)
    wrep = jnp.dot(wgt, exp_ref[...], precision=HI, preferred_element_type=F32)
    for g in range(g_tok):
        v_rows = pltpu.bitcast(tiles(g) & jnp.uint32(0xFFFF0000), F32).astype(BF16)
        wexp = jnp.where(diag, jnp.broadcast_to(wrep[g:g + 1, :], diag.shape), 0.0).astype(BF16)
        o_ref[g] = jnp.dot(wexp, v_rows, preferred_element_type=F32)
        prefetch((g_tok + g) * per_piece, per_piece)

    @pl.when(step == nsteps - 1)
    def _():
        slot_wait(other)


def _peer(h, eid_tok, gate_tok, uv_packed):
    t, d = h.shape
    g_tok = G_PEER
    nsteps = t // g_tok
    rows = g_tok * PEER_SEL
    nsub = d // LANES
    eid3 = eid_tok.reshape(nsteps, 1, rows)
    col = jnp.arange(PEER_SEL * nsub) // nsub
    compress = (col[:, None] == jnp.arange(PEER_SEL)[None, :]).astype(F32)
    smem_blk = lambda f: pl.BlockSpec((1, 1, rows), f, memory_space=pltpu.SMEM)
    tile_blk = pl.BlockSpec((g_tok, nsub, LANES), lambda i: (i, 0, 0))
    out = pl.pallas_call(
        _peer_kernel,
        grid=(nsteps,),
        in_specs=[smem_blk(lambda i: (i, 0, 0)),
                  smem_blk(lambda i: (jnp.minimum(i + 1, nsteps - 1), 0, 0)),
                  tile_blk,
                  pl.BlockSpec((g_tok, PEER_SEL), lambda i: (i, 0)),
                  pl.BlockSpec((PEER_SEL * nsub, PEER_SEL), lambda i: (0, 0)),
                  pl.BlockSpec((PEER_SEL, PEER_SEL * nsub), lambda i: (0, 0)),
                  pl.BlockSpec(memory_space=pl.ANY)],
        out_specs=tile_blk,
        out_shape=jax.ShapeDtypeStruct((t, nsub, LANES), F32),
        scratch_shapes=[pltpu.VMEM((2, rows, nsub, LANES), U32), pltpu.SemaphoreType.DMA((2,))],
        compiler_params=_cparams(("arbitrary",), VMEM_LIMIT),
        name="peer_experts",
    )(eid3, eid3, h.reshape(t, nsub, LANES), gate_tok, compress, compress.T,
      uv_packed.reshape(-1, nsub, LANES))
    return out.reshape(t, d)


def _pack_tables(u_tab, v_tab):
    ub = lax.bitcast_convert_type(u_tab.astype(BF16), jnp.uint16).astype(U32)
    vb = lax.bitcast_convert_type(v_tab.astype(BF16), jnp.uint16).astype(U32)
    return ub | (vb << 16)


def _out_kernel(h_ref, f_ref, p_ref, wp_ref, wgp_ref, g_ref, b_ref, o_ref):
    h = h_ref[...]
    ple = jnp.dot(p_ref[...].astype(BF16), wp_ref[...], preferred_element_type=F32)
    gate = _sigmoid(jnp.dot(h.astype(BF16), wgp_ref[...], preferred_element_type=F32))
    o_ref[...] = _ln(DN_ALPHA * h + f_ref[...] + ple * gate, g_ref[...], b_ref[...])


def _layer_out(h, ffn, p2, ple_w, ple_gate_w, ln_g, ln_b):
    t, d = h.shape
    tm = TM_OUT
    tok = lambda i: (i, 0)
    par = lambda i: (0, 0)
    return pl.pallas_call(
        _out_kernel,
        grid=(t // tm,),
        in_specs=[pl.BlockSpec((tm, d), tok), pl.BlockSpec((tm, d), tok),
                  pl.BlockSpec((tm, PLE_DIM), tok),
                  pl.BlockSpec((PLE_DIM, d), par), pl.BlockSpec((d, d), par),
                  pl.BlockSpec((1, d), par), pl.BlockSpec((1, d), par)],
        out_specs=pl.BlockSpec((tm, d), tok),
        out_shape=jax.ShapeDtypeStruct((t, d), F32),
        compiler_params=_cparams(("parallel",)),
        name="ple_ln2",
    )(h, ffn, p2, ple_w, ple_gate_w, ln_g.reshape(1, -1), ln_b.reshape(1, -1))


def _block_ones():
    head = jnp.arange(BRANCH) // HEAD_DIM
    return (head[:, None] == head[None, :]).astype(F32)


def _layer(h, p2, bsz, seq, w_in, rwkv_mu, rwkv_w0, rwkv_w2, rwkv_a0, rwkv_a2, rwkv_g2, rwkv_kk,
           rwkv_ka, rwkv_rk, rwkv_lnx_g, rwkv_lnx_b, s5_lam_re, s5_lam_im, s5_b_re, s5_b_im, s5_c_re,
           s5_c_im, s5_d, s5_log_dt, s5_glu_w, s5_glu_b, fox_bf, conv_w, conv_b, conv_ln_g, conv_ln_b,
           w_branch, w_out, ln1_g, ln1_b, peer_wq, peer_k1, peer_k2, peer_u, peer_v, ple_w, ple_gate_w,
           ln2_g, ln2_b):
    t = bsz * seq
    d = D_MODEL
    n_front = RWKV_COLS + BRANCH + 3 * BRANCH + HEADS
    w_front = w_in[:, :n_front]
    w_conv = w_in[:, n_front:n_front + 2 * BRANCH]
    w_gate = w_in[:, n_front + 2 * BRANCH:]
    pad = jnp.zeros((d, _P_FF[1] - _P_FF[0] - HEADS), w_in.dtype)
    w_packed = jnp.concatenate([w_front, pad, w_conv], axis=1).astype(BF16)
    bd = _block_ones()
    wl = jnp.zeros((LANES, 3 * BRANCH), F32)
    wl = wl.at[0:32, 0:BRANCH].set(rwkv_w2.astype(F32))
    wl = wl.at[32:64, BRANCH:2 * BRANCH].set(rwkv_a2.astype(F32))
    wl = wl.at[64:128, 2 * BRANCH:].set(rwkv_g2.astype(F32))

    p_rwkv, p_s5_t, p_fqkv, p_ff, p_conv = _project(h, w_packed, bsz, seq)

    r, dec, k2, v, na, bb, g, bonus = _rwkv_pre(p_rwkv, rwkv_mu, rwkv_w0, rwkv_a0, rwkv_kk, rwkv_ka,
                                                 rwkv_rk.reshape(-1), wl, bd, bsz, seq)
    y_scan = _rwkv_scan(_to_scan_j(na, bsz, seq), _to_scan_j(dec, bsz, seq), _to_scan_j(bb, bsz, seq),
                        _to_scan_j(k2, bsz, seq), _to_scan_j(r, bsz, seq), _to_scan_i(v, bsz, seq))
    y_rwkv = _rwkv_post(_from_scan_i(y_scan, bsz, seq), g.reshape(t, BRANCH), bonus.reshape(t, BRANCH),
                        rwkv_lnx_g, rwkv_lnx_b, bd)

    bre, bim, lre, lim, cc = _s5_params(s5_lam_re, s5_lam_im, s5_b_re, s5_b_im, s5_c_re, s5_c_im, s5_log_dt)
    y_s5_t = _s5(p_s5_t, bre, bim, lre, lim, cc, s5_d, s5_glu_w, s5_glu_b, bsz, seq)

    f_rows = p_ff[:, :HEADS].reshape(bsz, seq, HEADS).transpose(0, 2, 1).reshape(bsz * HEADS, seq)
    bias_rows = jnp.tile(fox_bf.astype(F32), bsz).reshape(bsz * HEADS, 1)
    c = _fox_cumsum(f_rows, bias_rows)
    to_heads = lambda x: x.reshape(bsz, seq, HEADS, HEAD_DIM).transpose(0, 2, 1, 3)
    o = _fox_attn(to_heads(p_fqkv[:, 0:BRANCH]), to_heads(p_fqkv[:, BRANCH:2 * BRANCH]),
                  to_heads(p_fqkv[:, 2 * BRANCH:]),
                  c.reshape(bsz, HEADS, seq, 1), c.reshape(bsz, HEADS, seq // TK_ATT, TK_ATT))
    y_fox = o.transpose(0, 2, 1, 3).reshape(t, BRANCH)

    y_conv = _conv_mixer(p_conv, conv_w, conv_b, conv_ln_g, conv_ln_b, bsz, seq)

    h1 = _merge(h, y_rwkv, y_s5_t, y_fox, y_conv, w_gate.astype(BF16), w_branch.astype(BF16),
                w_out.astype(BF16), ln1_g, ln1_b, bsz, seq)

    keys = jnp.stack([peer_k1, peer_k2]).astype(F32)
    eid_t, gate_t = _route(h1, peer_wq.astype(BF16), keys)
    ffn = _peer(h1, eid_t.T, gate_t.T, _pack_tables(peer_u, peer_v))

    return _layer_out(h1, ffn, p2, ple_w.astype(BF16), ple_gate_w.astype(BF16), ln2_g, ln2_b)


def kernel(x, p, ln_in_g, ln_in_b, w_in, rwkv_mu, rwkv_w0, rwkv_w2, rwkv_a0, rwkv_a2, rwkv_g2, rwkv_kk, rwkv_ka, rwkv_rk, rwkv_lnx_g, rwkv_lnx_b, s5_lam_re, s5_lam_im, s5_b_re, s5_b_im, s5_c_re, s5_c_im, s5_d, s5_log_dt, s5_glu_w, s5_glu_b, fox_bf, conv_w, conv_b, conv_ln_g, conv_ln_b, w_branch, w_out, ln1_g, ln1_b, peer_wq, peer_k1, peer_k2, peer_u, peer_v, ple_w, ple_gate_w, ln2_g, ln2_b):
    bsz, seq, d = x.shape
    t = bsz * seq
    h = _layer_norm(x.reshape(t, d), ln_in_g, ln_in_b)
    per_layer = (w_in, rwkv_mu, rwkv_w0, rwkv_w2, rwkv_a0, rwkv_a2, rwkv_g2, rwkv_kk, rwkv_ka, rwkv_rk,
                 rwkv_lnx_g, rwkv_lnx_b, s5_lam_re, s5_lam_im, s5_b_re, s5_b_im, s5_c_re, s5_c_im, s5_d,
                 s5_log_dt, s5_glu_w, s5_glu_b, fox_bf, conv_w, conv_b, conv_ln_g, conv_ln_b, w_branch,
                 w_out, ln1_g, ln1_b, peer_wq, peer_k1, peer_k2, peer_u, peer_v, ple_w, ple_gate_w,
                 ln2_g, ln2_b)
    for i in range(p.shape[0]):
        h = _layer(h, p[i].reshape(t, PLE_DIM), bsz, seq, *(w[i] for w in per_layer))
    return h.reshape(bsz, seq, d)
```

```python
import functools
import math

import jax
import jax.numpy as jnp
from jax import lax
from jax.experimental import pallas as pl
from jax.experimental.pallas import tpu as pltpu

F32 = jnp.float32
BF16 = jnp.bfloat16
I32 = jnp.int32
U32 = jnp.uint32
HI = lax.Precision.HIGHEST

D_MODEL = 1024
BRANCH = 256
HEADS = 4
HEAD_DIM = 64
RWKV_COLS = 896
S5_GROUPS = 16
S5_GROUP = 16
S5_STATE = 64
S5_N = S5_GROUPS * S5_STATE
CONV_WIDTH = 31
CONV_HALO = 32
PEER_HEADS = 8
PEER_KEYS = 128
PEER_HALF = 128
PEER_TOPK = 16
PEER_SEL = PEER_HEADS * PEER_TOPK
PLE_DIM = 256
RWKV_GN_EPS = 64e-5
LN_EPS = 1e-5
DEPTH = 2
DN_ALPHA = (2 * DEPTH) ** 0.25

SUBLANES = 8
LANES = 128
VMEM_LIMIT = 56 * 1024 * 1024

TM_PROJ = 512
TQ_PRE = 256
TC_SCAN = 32
TC_S5 = 32
TQ_ATT = 256
TK_ATT = 256
TC_CONV = 512
TM_MERGE = 256
TM_ROUTE = 128
G_PEER = 16
TM_OUT = 256


def _cparams(sem, vmem=None):
    return pltpu.CompilerParams(dimension_semantics=sem, vmem_limit_bytes=vmem)


def _ln(z, g, b):
    mu = jnp.mean(z, axis=-1, keepdims=True)
    zc = z - mu
    var = jnp.mean(zc * zc, axis=-1, keepdims=True)
    return zc * lax.rsqrt(var + LN_EPS) * g + b


def _gelu(y):
    return 0.5 * y * (1.0 + lax.erf(y * (1.0 / math.sqrt(2.0))))


def _sigmoid(y):
    return 1.0 / (1.0 + jnp.exp(-y))


def _softplus(y):
    return jnp.maximum(y, 0.0) + jnp.log(1.0 + jnp.exp(-jnp.abs(y)))


def _ln_kernel(x_ref, g_ref, b_ref, o_ref):
    o_ref[...] = _ln(x_ref[...], g_ref[...], b_ref[...])


def _layer_norm(x2, g, b):
    t, d = x2.shape
    tm = TM_PROJ
    return pl.pallas_call(
        _ln_kernel,
        grid=(t // tm,),
        in_specs=[pl.BlockSpec((tm, d), lambda i: (i, 0)),
                  pl.BlockSpec((1, d), lambda i: (0, 0)),
                  pl.BlockSpec((1, d), lambda i: (0, 0))],
        out_specs=pl.BlockSpec((tm, d), lambda i: (i, 0)),
        out_shape=jax.ShapeDtypeStruct((t, d), F32),
        compiler_params=_cparams(("parallel",)),
        name="ln_in",
    )(x2, g.reshape(1, d), b.reshape(1, d))


_P_RWKV = (0, 896)
_P_S5 = (896, 1152)
_P_FQKV = (1152, 1920)
_P_FF = (1920, 2048)
_P_CONV = (2048, 2560)
_P_COLS = 2560


def _proj_kernel(h_ref, w_ref, rw_ref, s5_ref, fq_ref, ff_ref, cv_ref):
    x = h_ref[...].astype(BF16)
    for (lo, hi), o_ref in ((_P_RWKV, rw_ref), (_P_S5, s5_ref), (_P_FQKV, fq_ref),
                            (_P_FF, ff_ref), (_P_CONV, cv_ref)):
        o_ref[...] = jnp.dot(x, w_ref[:, lo:hi], preferred_element_type=F32)


def _project(h, w_packed, bsz, seq):
    t, d = h.shape
    tm = TM_PROJ
    nst = seq // tm
    widths = [hi - lo for lo, hi in (_P_RWKV, _P_S5, _P_FQKV, _P_FF, _P_CONV)]
    tok = lambda i: (i, 0)
    out_shapes = [jax.ShapeDtypeStruct((t, widths[0]), F32),
                  jax.ShapeDtypeStruct((seq, bsz * widths[1]), F32),
                  jax.ShapeDtypeStruct((t, widths[2]), F32),
                  jax.ShapeDtypeStruct((t, widths[3]), F32),
                  jax.ShapeDtypeStruct((t, widths[4]), F32)]
    out_specs = [pl.BlockSpec((tm, widths[0]), tok),
                 pl.BlockSpec((tm, widths[1]), lambda i: (i % nst, i // nst)),
                 pl.BlockSpec((tm, widths[2]), tok),
                 pl.BlockSpec((tm, widths[3]), tok),
                 pl.BlockSpec((tm, widths[4]), tok)]
    return pl.pallas_call(
        _proj_kernel,
        grid=(t // tm,),
        in_specs=[pl.BlockSpec((tm, d), tok),
                  pl.BlockSpec((d, _P_COLS), lambda i: (0, 0))],
        out_specs=out_specs,
        out_shape=out_shapes,
        compiler_params=_cparams(("parallel",), VMEM_LIMIT),
        name="in_proj",
    )(h, w_packed)


def _rwkv_pre_kernel(z_ref, zp_ref, mu_ref, w0_ref, a0_ref, kk_ref, ka_ref, rk_ref, wl_ref, bd_ref,
                     r_out, w_out, k_out, v_out, na_out, b_out, g_out, bv_out):
    j = pl.program_id(1)
    z = z_ref[0]
    prev = zp_ref[0][SUBLANES - 1:SUBLANES, :]
    prev = jnp.where(j == 0, 0.0, prev)
    row = lax.broadcasted_iota(I32, z.shape, 0)
    zs = jnp.where(row == 0, prev, pltpu.roll(z, 1, 0))
    z = z + (zs - z) * mu_ref[...]
    r = z[:, 0:256]
    k = z[:, 256:512]
    v = z[:, 512:768]
    zc = z[:, 768:896]
    lane = lax.broadcasted_iota(I32, zc.shape, 1)
    act = jnp.where(lane < 32, jnp.tanh(zc), jnp.where(lane < 64, zc, _sigmoid(zc)))
    lo = jnp.dot(act, wl_ref[...], precision=HI, preferred_element_type=F32)
    dw = lo[:, 0:256]
    da = lo[:, 256:512]
    g = lo[:, 512:768]
    w_log = -_softplus(-(w0_ref[...] + dw)) - 0.5
    decay = jnp.exp(-jnp.exp(w_log))
    a = _sigmoid(a0_ref[...] + da)
    kkv = k * kk_ref[...]
    bd = bd_ref[...]
    ss = jnp.dot(kkv * kkv, bd, precision=HI, preferred_element_type=F32)
    kkn = kkv * lax.rsqrt(jnp.maximum(ss, 1e-24))
    k2 = k * (1.0 + (a - 1.0) * ka_ref[...])
    bonus = jnp.dot(r * k2 * rk_ref[...], bd, precision=HI, preferred_element_type=F32) * v
    r_out[0] = r
    w_out[0] = decay
    k_out[0] = k2
    v_out[0] = v
    na_out[0] = -kkn
    b_out[0] = kkn * a
    g_out[0] = g
    bv_out[0] = bonus


def _rwkv_pre(p_rwkv, mu, w0, a0, kk, ka, rk, wl, bd, bsz, seq):
    z3 = p_rwkv.reshape(bsz, seq, RWKV_COLS)
    tq = TQ_PRE
    blk = lambda b, j: (b, j, 0)
    par = lambda b, j: (0, 0)
    out_shape = [jax.ShapeDtypeStruct((bsz, seq, BRANCH), F32)] * 8
    out_specs = [pl.BlockSpec((1, tq, BRANCH), blk)] * 8
    return pl.pallas_call(
        _rwkv_pre_kernel,
        grid=(bsz, seq // tq),
        in_specs=[pl.BlockSpec((1, tq, RWKV_COLS), blk),
                  pl.BlockSpec((1, SUBLANES, RWKV_COLS),
                               lambda b, j: (b, jnp.maximum(j * (tq // SUBLANES) - 1, 0), 0)),
                  pl.BlockSpec((1, RWKV_COLS), par),
                  pl.BlockSpec((1, BRANCH), par), pl.BlockSpec((1, BRANCH), par),
                  pl.BlockSpec((1, BRANCH), par), pl.BlockSpec((1, BRANCH), par),
                  pl.BlockSpec((1, BRANCH), par),
                  pl.BlockSpec((LANES, 3 * BRANCH), par),
                  pl.BlockSpec((BRANCH, BRANCH), par)],
        out_specs=out_specs,
        out_shape=out_shape,
        compiler_params=_cparams(("parallel", "parallel")),
        name="rwkv_pre",
    )(z3, z3, mu.reshape(1, -1), w0.reshape(1, -1), a0.reshape(1, -1), kk.reshape(1, -1),
      ka.reshape(1, -1), rk.reshape(1, -1), wl, bd)


_SCAN_IG = HEAD_DIM // 2 // SUBLANES


def _rwkv_scan_kernel(a_ref, w_ref, b_ref, k_ref, r_ref, v_ref, y_ref, s_ref):
    @pl.when(pl.program_id(0) == 0)
    def _():
        s_ref[...] = jnp.zeros_like(s_ref)

    tc = a_ref.shape[0]
    tile = (SUBLANES, LANES)

    def step(t, carry):
        vv = [v_ref[t, ig * SUBLANES:(ig + 1) * SUBLANES, :] for ig in range(_SCAN_IG)]
        sa = [[jnp.zeros(tile, F32), jnp.zeros(tile, F32)] for _ in range(_SCAN_IG)]
        for j in range(HEAD_DIM):
            ab = jnp.broadcast_to(a_ref[t, j:j + 1, :], tile)
            for ig in range(_SCAN_IG):
                sa[ig][j % 2] = sa[ig][j % 2] + s_ref[ig, j] * ab
        sa = [x[0] + x[1] for x in sa]
        yy = [[jnp.zeros(tile, F32), jnp.zeros(tile, F32)] for _ in range(_SCAN_IG)]
        for j in range(HEAD_DIM):
            wb = jnp.broadcast_to(w_ref[t, j:j + 1, :], tile)
            bb = jnp.broadcast_to(b_ref[t, j:j + 1, :], tile)
            kb = jnp.broadcast_to(k_ref[t, j:j + 1, :], tile)
            rb = jnp.broadcast_to(r_ref[t, j:j + 1, :], tile)
            for ig in range(_SCAN_IG):
                s = s_ref[ig, j] * wb + sa[ig] * bb + vv[ig] * kb
                s_ref[ig, j] = s
                yy[ig][j % 2] = yy[ig][j % 2] + s * rb
        for ig in range(_SCAN_IG):
            y_ref[t, ig * SUBLANES:(ig + 1) * SUBLANES, :] = yy[ig][0] + yy[ig][1]
        return carry

    lax.fori_loop(0, tc, step, 0)


def _rwkv_scan(a_t, w_t, b_t, k_t, r_t, v_t):
    seq = a_t.shape[0]
    tc = TC_SCAN
    jspec = pl.BlockSpec((tc, HEAD_DIM, LANES), lambda i: (i, 0, 0))
    ispec = pl.BlockSpec((tc, HEAD_DIM // 2, LANES), lambda i: (i, 0, 0))
    return pl.pallas_call(
        _rwkv_scan_kernel,
        grid=(seq // tc,),
        in_specs=[jspec, jspec, jspec, jspec, jspec, ispec],
        out_specs=ispec,
        out_shape=jax.ShapeDtypeStruct((seq, HEAD_DIM // 2, LANES), F32),
        scratch_shapes=[pltpu.VMEM((_SCAN_IG, HEAD_DIM, SUBLANES, LANES), F32)],
        compiler_params=_cparams(("arbitrary",)),
        name="rwkv_scan",
    )(a_t, w_t, b_t, k_t, r_t, v_t)


def _to_scan_j(x, bsz, seq):
    n_pairs = bsz * HEADS
    xt = x.reshape(bsz, seq, HEADS, HEAD_DIM).transpose(1, 3, 0, 2).reshape(seq, HEAD_DIM, n_pairs)
    pad = LANES // 2 - n_pairs
    if pad:
        xt = jnp.pad(xt, ((0, 0), (0, 0), (0, pad)))
    return jnp.concatenate([xt, xt], axis=-1)


def _to_scan_i(x, bsz, seq):
    n_pairs = bsz * HEADS
    xt = x.reshape(bsz, seq, HEADS, 2, HEAD_DIM // 2).transpose(1, 4, 3, 0, 2)
    xt = xt.reshape(seq, HEAD_DIM // 2, 2, n_pairs)
    pad = LANES // 2 - n_pairs
    if pad:
        xt = jnp.pad(xt, ((0, 0), (0, 0), (0, 0), (0, pad)))
    return xt.reshape(seq, HEAD_DIM // 2, LANES)


def _from_scan_i(y, bsz, seq):
    n_pairs = bsz * HEADS
    yt = y.reshape(seq, HEAD_DIM // 2, 2, LANES // 2)[..., :n_pairs]
    yt = yt.reshape(seq, HEAD_DIM // 2, 2, bsz, HEADS).transpose(3, 0, 4, 2, 1)
    return yt.reshape(bsz * seq, BRANCH)


def _rwkv_post_kernel(y_ref, g_ref, bv_ref, gg_ref, gb_ref, bd_ref, o_ref):
    y = y_ref[...]
    bd = bd_ref[...]
    inv = 1.0 / HEAD_DIM
    mean = jnp.dot(y, bd, precision=HI, preferred_element_type=F32) * inv
    yc = y - mean
    var = jnp.dot(yc * yc, bd, precision=HI, preferred_element_type=F32) * inv
    yn = yc * lax.rsqrt(var + RWKV_GN_EPS) * gg_ref[...] + gb_ref[...]
    o_ref[...] = (yn + bv_ref[...]) * g_ref[...]


def _rwkv_post(y, g, bv, gn_g, gn_b, bd):
    t = y.shape[0]
    tm = TM_PROJ
    tok = lambda i: (i, 0)
    par = lambda i: (0, 0)
    return pl.pallas_call(
        _rwkv_post_kernel,
        grid=(t // tm,),
        in_specs=[pl.BlockSpec((tm, BRANCH), tok)] * 3
                 + [pl.BlockSpec((1, BRANCH), par)] * 2 + [pl.BlockSpec((BRANCH, BRANCH), par)],
        out_specs=pl.BlockSpec((tm, BRANCH), tok),
        out_shape=jax.ShapeDtypeStruct((t, BRANCH), F32),
        compiler_params=_cparams(("parallel",)),
        name="rwkv_post",
    )(y, g, bv, gn_g.reshape(1, -1), gn_b.reshape(1, -1), bd)


def _s5_kernel(u_ref, bre_ref, bim_ref, lre_ref, lim_ref, cc_ref, d_ref, gw_ref, gb_ref, o_ref,
               xre_ref, xim_ref, bure_ref, buim_ref, xs_ref):
    @pl.when(pl.program_id(0) == 0)
    def _():
        xre_ref[...] = jnp.zeros_like(xre_ref)
        xim_ref[...] = jnp.zeros_like(xim_ref)

    tc, nb, w = u_ref.shape
    u2 = u_ref[...].reshape(tc * nb, w)
    ub = u2.astype(BF16)
    bure_ref[...] = jnp.dot(ub, bre_ref[...], preferred_element_type=F32)
    buim_ref[...] = jnp.dot(ub, bim_ref[...], preferred_element_type=F32)
    lre = jnp.broadcast_to(lre_ref[...], (nb, S5_N))
    lim = jnp.broadcast_to(lim_ref[...], (nb, S5_N))

    def step(t, carry):
        xr, xi = carry
        r0 = pl.multiple_of(t * nb, nb)
        nr = lre * xr - lim * xi + bure_ref[pl.ds(r0, nb), :]
        ni = lre * xi + lim * xr + buim_ref[pl.ds(r0, nb), :]
        xs_ref[pl.ds(r0, nb), 0:S5_N] = nr
        xs_ref[pl.ds(r0, nb), S5_N:2 * S5_N] = ni
        return nr, ni

    xr, xi = lax.fori_loop(0, tc, step, (xre_ref[...], xim_ref[...]))
    xre_ref[...] = xr
    xim_ref[...] = xi
    y = jnp.dot(xs_ref[...].astype(BF16), cc_ref[...], preferred_element_type=F32) + d_ref[...] * u2
    y = _gelu(y)
    gate = jnp.dot(y.astype(BF16), gw_ref[...], preferred_element_type=F32) + gb_ref[...]
    o_ref[...] = (y * _sigmoid(gate)).reshape(tc, nb, w)


def _s5(u_t, bre, bim, lre, lim, cc, d_skip, glu_w, glu_b, bsz, seq):
    u3 = u_t.reshape(seq, bsz, BRANCH)
    tc = TC_S5
    par = lambda i: (0, 0)
    blk = pl.BlockSpec((tc, bsz, BRANCH), lambda i: (i, 0, 0))
    out = pl.pallas_call(
        _s5_kernel,
        grid=(seq // tc,),
        in_specs=[blk,
                  pl.BlockSpec((BRANCH, S5_N), par), pl.BlockSpec((BRANCH, S5_N), par),
                  pl.BlockSpec((1, S5_N), par), pl.BlockSpec((1, S5_N), par),
                  pl.BlockSpec((2 * S5_N, BRANCH), par),
                  pl.BlockSpec((1, BRANCH), par),
                  pl.BlockSpec((BRANCH, BRANCH), par),
                  pl.BlockSpec((1, BRANCH), par)],
        out_specs=blk,
        out_shape=jax.ShapeDtypeStruct((seq, bsz, BRANCH), F32),
        scratch_shapes=[pltpu.VMEM((bsz, S5_N), F32), pltpu.VMEM((bsz, S5_N), F32),
                        pltpu.VMEM((tc * bsz, S5_N), F32), pltpu.VMEM((tc * bsz, S5_N), F32),
                        pltpu.VMEM((tc * bsz, 2 * S5_N), F32)],
        compiler_params=_cparams(("arbitrary",), VMEM_LIMIT),
        name="s5_scan",
    )(u3, bre.astype(BF16), bim.astype(BF16), lre, lim, cc.astype(BF16), d_skip.reshape(1, -1),
      glu_w.astype(BF16), glu_b.reshape(1, -1))
    return out.reshape(seq, bsz * BRANCH)


def _s5_params(lam_re, lam_im, b_re, b_im, c_re, c_im, log_dt):
    lam = lax.complex(lam_re.astype(F32), lam_im.astype(F32))
    dt = jnp.exp(log_dt.astype(F32))[:, None]
    lam_bar = jnp.exp(lam * dt)
    b_bar = ((lam_bar - 1.0) / lam)[..., None] * lax.complex(b_re.astype(F32), b_im.astype(F32))
    eye = jnp.eye(S5_GROUPS, dtype=F32)
    bre = jnp.einsum('gpc,gh->gchp', jnp.real(b_bar), eye).reshape(BRANCH, S5_N)
    bim = jnp.einsum('gpc,gh->gchp', jnp.imag(b_bar), eye).reshape(BRANCH, S5_N)
    cre = jnp.einsum('gcp,gh->gphc', c_re.astype(F32), eye).reshape(S5_N, BRANCH)
    cim = jnp.einsum('gcp,gh->gphc', c_im.astype(F32), eye).reshape(S5_N, BRANCH)
    cc = jnp.concatenate([cre, -cim], axis=0)
    return bre, bim, jnp.real(lam_bar).reshape(1, S5_N), jnp.imag(lam_bar).reshape(1, S5_N), cc


def _fox_cumsum_kernel(f_ref, bias_ref, c_ref):
    x = f_ref[...] + bias_ref[...]
    c = -_softplus(-x)
    n = c.shape[1]
    lane = lax.broadcasted_iota(I32, c.shape, 1)
    sh = 1
    while sh < n:
        c = c + jnp.where(lane >= sh, pltpu.roll(c, sh, 1), 0.0)
        sh *= 2
    c_ref[...] = c


def _fox_cumsum(f_rows, bias_rows):
    rows, seq = f_rows.shape
    return pl.pallas_call(
        _fox_cumsum_kernel,
        out_shape=jax.ShapeDtypeStruct((rows, seq), F32),
        name="fox_cumsum",
    )(f_rows, bias_rows)


def _fox_attn_kernel(q_ref, k_ref, v_ref, cq_ref, ck_ref, o_ref):
    qi = pl.program_id(1)
    nh, tq = q_ref.shape[1], q_ref.shape[2]
    tk = ck_ref.shape[3]
    qs = [(q_ref[0, h] * (HEAD_DIM ** -0.5)).astype(BF16) for h in range(nh)]
    cqs = [cq_ref[0, h] for h in range(nh)]
    qpos = qi * tq + lax.broadcasted_iota(I32, (tq, tk), 0)
    nkb = (qi * tq + tq + tk - 1) // tk

    def body(kb, carry):
        k0 = pl.multiple_of(kb * tk, tk)
        causal = k0 + lax.broadcasted_iota(I32, (tq, tk), 1) <= qpos
        out = []
        for h in range(nh):
            m, acc = carry[h]
            kblk = k_ref[0, h, pl.ds(k0, tk), :].astype(BF16)
            vblk = v_ref[0, h, pl.ds(k0, tk), :].astype(BF16)
            s = lax.dot_general(qs[h], kblk, (((1,), (1,)), ((), ())), preferred_element_type=F32)
            s = s + cqs[h] - ck_ref[0, h, pl.ds(kb, 1), :]
            s = jnp.where(causal, s, -1e30)
            m_new = jnp.maximum(m, jnp.max(s, axis=-1, keepdims=True))
            p = jnp.exp(s - m_new)
            acc = jnp.exp(m - m_new) * acc + jnp.dot(p.astype(BF16), vblk, preferred_element_type=F32)
            out.append((m_new, acc))
        return tuple(out)

    init = tuple((jnp.full((tq, 1), -1e30, F32), jnp.zeros((tq, v_ref.shape[3]), F32)) for _ in range(nh))
    res = lax.fori_loop(0, nkb, body, init)
    for h in range(nh):
        acc = res[h][1]
        o_ref[0, h] = acc[:, 0:HEAD_DIM] / acc[:, HEAD_DIM:HEAD_DIM + 1]


def _fox_attn(q, k, v, c_col, c_rowb):
    bsz, nh, seq, hd = q.shape
    tq = TQ_ATT
    nkb, tk = c_rowb.shape[2], c_rowb.shape[3]
    full = lambda b, i: (b, 0, 0, 0)
    tile = lambda b, i: (b, 0, i, 0)
    return pl.pallas_call(
        _fox_attn_kernel,
        grid=(bsz, seq // tq),
        in_specs=[pl.BlockSpec((1, nh, tq, hd), tile),
                  pl.BlockSpec((1, nh, seq, hd), full),
                  pl.BlockSpec((1, nh, seq, v.shape[3]), full),
                  pl.BlockSpec((1, nh, tq, 1), tile),
                  pl.BlockSpec((1, nh, nkb, tk), full)],
        out_specs=pl.BlockSpec((1, nh, tq, hd), tile),
        out_shape=jax.ShapeDtypeStruct((bsz, nh, seq, hd), F32),
        compiler_params=_cparams(("parallel", "parallel"), VMEM_LIMIT),
        name="fox_attn",
    )(q, k, v, c_col, c_rowb)


def _conv_kernel(x_ref, xp_ref, w_ref, b_ref, g_ref, be_ref, o_ref, buf_ref):
    j = pl.program_id(1)
    tc = x_ref.shape[1]
    x = x_ref[0]
    buf_ref[CONV_HALO:CONV_HALO + tc, :] = x[:, 0:BRANCH] * _sigmoid(x[:, BRANCH:2 * BRANCH])
    xp = xp_ref[0]
    hp = xp[:, 0:BRANCH] * _sigmoid(xp[:, BRANCH:2 * BRANCH])
    buf_ref[0:CONV_HALO, :] = jnp.where(j == 0, 0.0, hp)
    acc = jnp.zeros((tc, BRANCH), F32) + b_ref[...]
    off = CONV_HALO - (CONV_WIDTH - 1)
    for kk in range(CONV_WIDTH):
        acc = acc + buf_ref[off + kk:off + kk + tc, :] * w_ref[kk:kk + 1, :]
    y = _ln(acc, g_ref[...], be_ref[...])
    o_ref[0] = y * _sigmoid(y)


def _conv_mixer(p_conv, conv_w, conv_b, ln_g, ln_b, bsz, seq):
    x3 = p_conv.reshape(bsz, seq, 2 * BRANCH)
    tc = TC_CONV
    par = lambda b, j: (0, 0)
    out = pl.pallas_call(
        _conv_kernel,
        grid=(bsz, seq // tc),
        in_specs=[pl.BlockSpec((1, tc, 2 * BRANCH), lambda b, j: (b, j, 0)),
                  pl.BlockSpec((1, CONV_HALO, 2 * BRANCH),
                               lambda b, j: (b, jnp.maximum(j * (tc // CONV_HALO) - 1, 0), 0)),
                  pl.BlockSpec((CONV_WIDTH, BRANCH), par),
                  pl.BlockSpec((1, BRANCH), par), pl.BlockSpec((1, BRANCH), par),
                  pl.BlockSpec((1, BRANCH), par)],
        out_specs=pl.BlockSpec((1, tc, BRANCH), lambda b, j: (b, j, 0)),
        out_shape=jax.ShapeDtypeStruct((bsz, seq, BRANCH), F32),
        scratch_shapes=[pltpu.VMEM((CONV_HALO + tc, BRANCH), F32)],
        compiler_params=_cparams(("parallel", "parallel")),
        name="conv_mixer",
    )(x3, x3, conv_w, conv_b.reshape(1, -1), ln_g.reshape(1, -1), ln_b.reshape(1, -1))
    return out.reshape(bsz * seq, BRANCH)


def _merge_kernel(h_ref, y0_ref, y1_ref, y2_ref, y3_ref, wg_ref, wb_ref, wo_ref, g_ref, b_ref, o_ref):
    h = h_ref[...]
    hb = h.astype(BF16)
    merged = None
    for br, y_ref in enumerate((y0_ref, y1_ref, y2_ref, y3_ref)):
        gate = _sigmoid(jnp.dot(hb, wg_ref[:, br * D_MODEL:(br + 1) * D_MODEL], preferred_element_type=F32))
        term = gate * jnp.dot(y_ref[...].astype(BF16), wb_ref[br], preferred_element_type=F32)
        merged = term if merged is None else merged + term
    mix = jnp.dot(merged.astype(BF16), wo_ref[...], preferred_element_type=F32)
    o_ref[...] = _ln(DN_ALPHA * h + mix, g_ref[...], b_ref[...])


def _merge(h, y_rwkv, y_s5_t, y_fox, y_conv, wg, wb, wo, ln_g, ln_b, bsz, seq):
    t, d = h.shape
    tm = TM_MERGE
    nst = seq // tm
    tok = lambda i: (i, 0)
    par2 = lambda i: (0, 0)
    br_spec = pl.BlockSpec((tm, BRANCH), tok)
    return pl.pallas_call(
        _merge_kernel,
        grid=(t // tm,),
        in_specs=[pl.BlockSpec((tm, d), tok),
                  br_spec,
                  pl.BlockSpec((tm, BRANCH), lambda i: (i % nst, i // nst)),
                  br_spec, br_spec,
                  pl.BlockSpec((d, 4 * d), par2),
                  pl.BlockSpec((4, BRANCH, d), lambda i: (0, 0, 0)),
                  pl.BlockSpec((d, d), par2),
                  pl.BlockSpec((1, d), par2), pl.BlockSpec((1, d), par2)],
        out_specs=pl.BlockSpec((tm, d), tok),
        out_shape=jax.ShapeDtypeStruct((t, d), F32),
        compiler_params=_cparams(("parallel",), VMEM_LIMIT),
        name="merge_ln1",
    )(h, y_rwkv, y_s5_t, y_fox, y_conv, wg, wb, wo, ln_g.reshape(1, -1), ln_b.reshape(1, -1))


def _top_rows(s, key, payload=None):
    vals, keys, pays = [], [], []
    for _ in range(PEER_TOPK):
        m = jnp.max(s, axis=0, keepdims=True)
        kmin = jnp.min(jnp.where(s == m, key, jnp.int32(2 ** 30)), axis=0, keepdims=True)
        sel = key == kmin
        if payload is not None:
            pays.append(jnp.max(jnp.where(sel, payload, -1), axis=0, keepdims=True))
        s = jnp.where(sel, -jnp.inf, s)
        vals.append(m)
        keys.append(kmin)
    return vals, keys, pays


_CAND_ROWS = {2: 5, 3: 4, 4: 3, 5: 2, 6: 2, 7: 2}


def _candidates(v1, v2, i1, i2):
    tm = v1.shape[1]
    sub = lax.broadcasted_iota(I32, (SUBLANES, tm), 0)
    bc = lambda x, a: jnp.broadcast_to(x[a:a + 1, :], (SUBLANES, tm))
    vals, keys, eids = [], [], []
    for a, b0 in ((0, 0), (0, SUBLANES), (1, 0)):
        vals.append(bc(v1, a) + v2[b0:b0 + SUBLANES, :])
        keys.append(a * PEER_TOPK + b0 + sub)
        eids.append(bc(i1, a) * PEER_KEYS + i2[b0:b0 + SUBLANES, :])
    for a, nb in _CAND_ROWS.items():
        vals.append(jnp.where(sub < nb, bc(v1, a) + v2[0:SUBLANES, :], -jnp.inf))
        keys.append(a * PEER_TOPK + sub)
        eids.append(bc(i1, a) * PEER_KEYS + i2[0:SUBLANES, :])
    vals.append(v1[SUBLANES:, :] + bc(v2, 0))
    keys.append((SUBLANES + sub) * PEER_TOPK)
    eids.append(i1[SUBLANES:, :] * PEER_KEYS + bc(i2, 0))
    return jnp.concatenate(vals, axis=0), jnp.concatenate(keys, axis=0), jnp.concatenate(eids, axis=0)


_ROUTE_HEADS_PER_ITER = 4


def _route_kernel(h_ref, wq_ref, keys_ref, eid_ref, gate_ref, q_scr):
    tm = h_ref.shape[0]
    q = jnp.dot(h_ref[...].astype(BF16), wq_ref[...], preferred_element_type=F32)
    for c in range(2 * PEER_HEADS):
        q_scr[c] = q[:, c * PEER_HALF:(c + 1) * PEER_HALF]
    row = lax.broadcasted_iota(I32, (PEER_KEYS, tm), 0)

    def body(it, carry):
        for off in range(_ROUTE_HEADS_PER_ITER):
            hh = it * _ROUTE_HEADS_PER_ITER + off
            tops = []
            for half in range(2):
                st = lax.dot_general(keys_ref[half], q_scr[2 * hh + half], (((1,), (1,)), ((), ())),
                                     precision=HI, preferred_element_type=F32)
                vals, idxs, _ = _top_rows(st, row)
                tops.append((jnp.concatenate(vals, axis=0), jnp.concatenate(idxs, axis=0)))
            (v1, i1), (v2, i2) = tops
            cand, key, eids = _candidates(v1, v2, i1, i2)
            vals, _, pays = _top_rows(cand, key, payload=eids)
            score = jnp.concatenate(vals, axis=0)
            e = jnp.exp(score - jnp.max(score, axis=0, keepdims=True))
            r0 = pl.multiple_of(hh * PEER_TOPK, PEER_TOPK)
            gate_ref[pl.ds(r0, PEER_TOPK), :] = e / jnp.sum(e, axis=0, keepdims=True)
            eid_ref[pl.ds(r0, PEER_TOPK), :] = jnp.concatenate(pays, axis=0)
        return carry

    lax.fori_loop(0, PEER_HEADS // _ROUTE_HEADS_PER_ITER, body, 0)


def _route(h, wq, keys):
    t, d = h.shape
    tm = TM_ROUTE
    nq = 2 * PEER_HEADS
    return pl.pallas_call(
        _route_kernel,
        grid=(t // tm,),
        in_specs=[pl.BlockSpec((tm, d), lambda i: (i, 0)),
                  pl.BlockSpec((d, nq * PEER_HALF), lambda i: (0, 0)),
                  pl.BlockSpec((2, PEER_KEYS, PEER_HALF), lambda i: (0, 0, 0))],
        out_specs=[pl.BlockSpec((PEER_SEL, tm), lambda i: (0, i)),
                   pl.BlockSpec((PEER_SEL, tm), lambda i: (0, i))],
        out_shape=[jax.ShapeDtypeStruct((PEER_SEL, t), I32),
                   jax.ShapeDtypeStruct((PEER_SEL, t), F32)],
        scratch_shapes=[pltpu.VMEM((nq, tm, PEER_HALF), F32)],
        compiler_params=_cparams(("parallel",), VMEM_LIMIT),
        name="peer_route",
    )(h, wq, keys)


def _peer_kernel(eid_cur_ref, eid_nxt_ref, x_ref, gate_ref, cmp_ref, exp_ref, uv_hbm, o_ref, buf, sem):
    step = pl.program_id(0)
    nsteps = pl.num_programs(0)
    g_tok = G_PEER
    rows = g_tok * PEER_SEL
    nsub = D_MODEL // LANES
    per_piece = rows // (2 * g_tok)

    def slot_wait(sl):
        pltpu.make_async_copy(uv_hbm.at[pl.ds(0, rows)], buf.at[sl], sem.at[sl]).wait()

    @pl.when(step == 0)
    def _():
        def body(r, carry):
            pltpu.make_async_copy(uv_hbm.at[eid_cur_ref[0, 0, r]], buf.at[0, r], sem.at[0]).start()
            return carry
        lax.fori_loop(0, rows, body, 0, unroll=8)

    diag = (lax.broadcasted_iota(I32, (nsub, PEER_SEL * nsub), 1) % nsub
            == lax.broadcasted_iota(I32, (nsub, PEER_SEL * nsub), 0))

    def group(sl, idx_ref, idx_off):
        tok0 = sl * g_tok

        def prefetch(r0, n):
            for i in range(n):
                r = r0 + i
                pltpu.make_async_copy(uv_hbm.at[idx_ref[0, 0, idx_off + r]], buf.at[1 - sl, r],
                                      sem.at[1 - sl]).start(priority=i % 2)

        def tiles(g):
            return buf[sl, g * PEER_SEL:(g + 1) * PEER_SEL].reshape(PEER_SEL * nsub, LANES)

        slot_wait(sl)
        parts = []
        for g in range(g_tok):
            u_rows = pltpu.bitcast(tiles(g) << 16, F32).astype(BF16)
            q = lax.dot_general(x_ref[tok0 + g].astype(BF16), u_rows, (((1,), (1,)), ((), ())),
                                preferred_element_type=F32)
            parts.append(jnp.where(diag, q, 0.0))
            prefetch(g * per_piece, per_piece)
        part = jnp.dot(jnp.concatenate(parts, axis=0), cmp_ref[...], precision=HI,
                       preferred_element_type=F32)
        act = jnp.sum(part.reshape(g_tok, nsub, PEER_SEL), axis=1)
        wgt = gate_ref[tok0:tok0 + g_tok, :] * _gelu(act)
        wrep = jnp.dot(wgt, exp_ref[...], precision=HI, preferred_element_type=F32)
        for g in range(g_tok):
            v_rows = pltpu.bitcast(tiles(g) & jnp.uint32(0xFFFF0000), F32).astype(BF16)
            wexp = jnp.where(diag, jnp.broadcast_to(wrep[g:g + 1, :], diag.shape), 0.0).astype(BF16)
            o_ref[tok0 + g] = jnp.dot(wexp, v_rows, preferred_element_type=F32)
            prefetch((g_tok + g) * per_piece, per_piece)

    group(0, eid_cur_ref, rows)
    group(1, eid_nxt_ref, 0)

    @pl.when(step == nsteps - 1)
    def _():
        slot_wait(0)


def _peer(h, eid_tok, gate_tok, uv_packed):
    t, d = h.shape
    g_step = 2 * G_PEER
    nsteps = t // g_step
    rows = G_PEER * PEER_SEL
    nsub = d // LANES
    eid3 = eid_tok.reshape(nsteps, 1, 2 * rows)
    ncol = PEER_SEL * nsub
    col = jnp.arange(ncol) // nsub
    compress = (col[:, None] == jnp.arange(PEER_SEL)[None, :]).astype(F32)
    smem_blk = lambda f: pl.BlockSpec((1, 1, 2 * rows), f, memory_space=pltpu.SMEM)
    tile_blk = pl.BlockSpec((g_step, nsub, LANES), lambda i: (i, 0, 0))
    out = pl.pallas_call(
        _peer_kernel,
        grid=(nsteps,),
        in_specs=[smem_blk(lambda i: (i, 0, 0)),
                  smem_blk(lambda i: (jnp.minimum(i + 1, nsteps - 1), 0, 0)),
                  tile_blk,
                  pl.BlockSpec((g_step, PEER_SEL), lambda i: (i, 0)),
                  pl.BlockSpec((ncol, PEER_SEL), lambda i: (0, 0)),
                  pl.BlockSpec((PEER_SEL, ncol), lambda i: (0, 0)),
                  pl.BlockSpec(memory_space=pl.ANY)],
        out_specs=tile_blk,
        out_shape=jax.ShapeDtypeStruct((t, nsub, LANES), F32),
        scratch_shapes=[pltpu.VMEM((2, rows, nsub, LANES), U32), pltpu.SemaphoreType.DMA((2,))],
        compiler_params=_cparams(("arbitrary",), VMEM_LIMIT),
        name="peer_experts",
    )(eid3, eid3, h.reshape(t, nsub, LANES), gate_tok, compress, compress.T,
      uv_packed.reshape(-1, nsub, LANES))
    return out.reshape(t, d)


def _pack_tables(u_tab, v_tab):
    ub = lax.bitcast_convert_type(u_tab.astype(BF16), jnp.uint16).astype(U32)
    vb = lax.bitcast_convert_type(v_tab.astype(BF16), jnp.uint16).astype(U32)
    return ub | (vb << 16)


def _out_kernel(h_ref, f_ref, p_ref, wp_ref, wgp_ref, g_ref, b_ref, o_ref):
    h = h_ref[...]
    ple = jnp.dot(p_ref[...].astype(BF16), wp_ref[...], preferred_element_type=F32)
    gate = _sigmoid(jnp.dot(h.astype(BF16), wgp_ref[...], preferred_element_type=F32))
    o_ref[...] = _ln(DN_ALPHA * h + f_ref[...] + ple * gate, g_ref[...], b_ref[...])


def _layer_out(h, ffn, p2, ple_w, ple_gate_w, ln_g, ln_b):
    t, d = h.shape
    tm = TM_OUT
    tok = lambda i: (i, 0)
    par = lambda i: (0, 0)
    return pl.pallas_call(
        _out_kernel,
        grid=(t // tm,),
        in_specs=[pl.BlockSpec((tm, d), tok), pl.BlockSpec((tm, d), tok),
                  pl.BlockSpec((tm, PLE_DIM), tok),
                  pl.BlockSpec((PLE_DIM, d), par), pl.BlockSpec((d, d), par),
                  pl.BlockSpec((1, d), par), pl.BlockSpec((1, d), par)],
        out_specs=pl.BlockSpec((tm, d), tok),
        out_shape=jax.ShapeDtypeStruct((t, d), F32),
        compiler_params=_cparams(("parallel",)),
        name="ple_ln2",
    )(h, ffn, p2, ple_w, ple_gate_w, ln_g.reshape(1, -1), ln_b.reshape(1, -1))


def _block_ones():
    head = jnp.arange(BRANCH) // HEAD_DIM
    return (head[:, None] == head[None, :]).astype(F32)


def _layer(h, p2, bsz, seq, w_in, rwkv_mu, rwkv_w0, rwkv_w2, rwkv_a0, rwkv_a2, rwkv_g2, rwkv_kk,
           rwkv_ka, rwkv_rk, rwkv_lnx_g, rwkv_lnx_b, s5_lam_re, s5_lam_im, s5_b_re, s5_b_im, s5_c_re,
           s5_c_im, s5_d, s5_log_dt, s5_glu_w, s5_glu_b, fox_bf, conv_w, conv_b, conv_ln_g, conv_ln_b,
           w_branch, w_out, ln1_g, ln1_b, peer_wq, peer_k1, peer_k2, peer_u, peer_v, ple_w, ple_gate_w,
           ln2_g, ln2_b):
    t = bsz * seq
    d = D_MODEL
    n_front = RWKV_COLS + BRANCH + 3 * BRANCH + HEADS
    w_front = w_in[:, :n_front]
    w_conv = w_in[:, n_front:n_front + 2 * BRANCH]
    w_gate = w_in[:, n_front + 2 * BRANCH:]
    pad = jnp.zeros((d, _P_FF[1] - _P_FF[0] - HEADS), w_in.dtype)
    w_packed = jnp.concatenate([w_front, pad, w_conv], axis=1).astype(BF16)
    bd = _block_ones()
    wl = jnp.zeros((LANES, 3 * BRANCH), F32)
    wl = wl.at[0:32, 0:BRANCH].set(rwkv_w2.astype(F32))
    wl = wl.at[32:64, BRANCH:2 * BRANCH].set(rwkv_a2.astype(F32))
    wl = wl.at[64:128, 2 * BRANCH:].set(rwkv_g2.astype(F32))

    p_rwkv, p_s5_t, p_fqkv, p_ff, p_conv = _project(h, w_packed, bsz, seq)

    r, dec, k2, v, na, bb, g, bonus = _rwkv_pre(p_rwkv, rwkv_mu, rwkv_w0, rwkv_a0, rwkv_kk, rwkv_ka,
                                                 rwkv_rk.reshape(-1), wl, bd, bsz, seq)
    y_scan = _rwkv_scan(_to_scan_j(na, bsz, seq), _to_scan_j(dec, bsz, seq), _to_scan_j(bb, bsz, seq),
                        _to_scan_j(k2, bsz, seq), _to_scan_j(r, bsz, seq), _to_scan_i(v, bsz, seq))
    y_rwkv = _rwkv_post(_from_scan_i(y_scan, bsz, seq), g.reshape(t, BRANCH), bonus.reshape(t, BRANCH),
                        rwkv_lnx_g, rwkv_lnx_b, bd)

    bre, bim, lre, lim, cc = _s5_params(s5_lam_re, s5_lam_im, s5_b_re, s5_b_im, s5_c_re, s5_c_im, s5_log_dt)
    y_s5_t = _s5(p_s5_t, bre, bim, lre, lim, cc, s5_d, s5_glu_w, s5_glu_b, bsz, seq)

    f_rows = p_ff[:, :HEADS].reshape(bsz, seq, HEADS).transpose(0, 2, 1).reshape(bsz * HEADS, seq)
    bias_rows = jnp.tile(fox_bf.astype(F32), bsz).reshape(bsz * HEADS, 1)
    c = _fox_cumsum(f_rows, bias_rows)
    to_heads = lambda x: x.reshape(bsz, seq, HEADS, HEAD_DIM).transpose(0, 2, 1, 3)
    v_heads = to_heads(p_fqkv[:, 2 * BRANCH:])
    ones_col = (jnp.arange(LANES - HEAD_DIM) == 0).astype(F32)
    v_aug = jnp.concatenate([v_heads, jnp.broadcast_to(ones_col, v_heads.shape[:3] + (LANES - HEAD_DIM,))], -1)
    o = _fox_attn(to_heads(p_fqkv[:, 0:BRANCH]), to_heads(p_fqkv[:, BRANCH:2 * BRANCH]), v_aug,
                  c.reshape(bsz, HEADS, seq, 1), c.reshape(bsz, HEADS, seq // TK_ATT, TK_ATT))
    y_fox = o.transpose(0, 2, 1, 3).reshape(t, BRANCH)

    y_conv = _conv_mixer(p_conv, conv_w, conv_b, conv_ln_g, conv_ln_b, bsz, seq)

    h1 = _merge(h, y_rwkv, y_s5_t, y_fox, y_conv, w_gate.astype(BF16), w_branch.astype(BF16),
                w_out.astype(BF16), ln1_g, ln1_b, bsz, seq)

    keys = jnp.stack([peer_k1, peer_k2]).astype(F32)
    eid_t, gate_t = _route(h1, peer_wq.astype(BF16), keys)
    ffn = _peer(h1, eid_t.T, gate_t.T, _pack_tables(peer_u, peer_v))

    return _layer_out(h1, ffn, p2, ple_w.astype(BF16), ple_gate_w.astype(BF16), ln2_g, ln2_b)


def kernel(x, p, ln_in_g, ln_in_b, w_in, rwkv_mu, rwkv_w0, rwkv_w2, rwkv_a0, rwkv_a2, rwkv_g2, rwkv_kk, rwkv_ka, rwkv_rk, rwkv_lnx_g, rwkv_lnx_b, s5_lam_re, s5_lam_im, s5_b_re, s5_b_im, s5_c_re, s5_c_im, s5_d, s5_log_dt, s5_glu_w, s5_glu_b, fox_bf, conv_w, conv_b, conv_ln_g, conv_ln_b, w_branch, w_out, ln1_g, ln1_b, peer_wq, peer_k1, peer_k2, peer_u, peer_v, ple_w, ple_gate_w, ln2_g, ln2_b):
    bsz, seq, d = x.shape
    t = bsz * seq
    h = _layer_norm(x.reshape(t, d), ln_in_g, ln_in_b)
    per_layer = (w_in, rwkv_mu, rwkv_w0, rwkv_w2, rwkv_a0, rwkv_a2, rwkv_g2, rwkv_kk, rwkv_ka, rwkv_rk,
                 rwkv_lnx_g, rwkv_lnx_b, s5_lam_re, s5_lam_im, s5_b_re, s5_b_im, s5_c_re, s5_c_im, s5_d,
                 s5_log_dt, s5_glu_w, s5_glu_b, fox_bf, conv_w, conv_b, conv_ln_g, conv_ln_b, w_branch,
                 w_out, ln1_g, ln1_b, peer_wq, peer_k1, peer_k2, peer_u, peer_v, ple_w, ple_gate_w,
                 ln2_g, ln2_b)
    for i in range(p.shape[0]):
        h = _layer(h, p[i].reshape(t, PLE_DIM), bsz, seq, *(w[i] for w in per_layer))
    return h.reshape(bsz, seq, d)
```

```python
import functools
import math

import jax
import jax.numpy as jnp
from jax import lax
from jax.experimental import pallas as pl
from jax.experimental.pallas import tpu as pltpu

F32 = jnp.float32
BF16 = jnp.bfloat16
I32 = jnp.int32
U32 = jnp.uint32
HI = lax.Precision.HIGHEST

D_MODEL = 1024
BRANCH = 256
HEADS = 4
HEAD_DIM = 64
RWKV_COLS = 896
S5_GROUPS = 16
S5_GROUP = 16
S5_STATE = 64
S5_N = S5_GROUPS * S5_STATE
CONV_WIDTH = 31
CONV_HALO = 32
PEER_HEADS = 8
PEER_KEYS = 128
PEER_HALF = 128
PEER_TOPK = 16
PEER_SEL = PEER_HEADS * PEER_TOPK
PLE_DIM = 256
RWKV_GN_EPS = 64e-5
LN_EPS = 1e-5
DEPTH = 2
DN_ALPHA = (2 * DEPTH) ** 0.25

SUBLANES = 8
LANES = 128
VMEM_LIMIT = 56 * 1024 * 1024

TM_PROJ = 512
TQ_PRE = 256
TC_SCAN = 32
TC_S5 = 32
TQ_ATT = 256
TK_ATT = 256
TC_CONV = 512
TM_MERGE = 256
TM_ROUTE = 128
G_PEER = 16
TM_OUT = 256


def _cparams(sem, vmem=None):
    return pltpu.CompilerParams(dimension_semantics=sem, vmem_limit_bytes=vmem)


def _ln(z, g, b):
    mu = jnp.mean(z, axis=-1, keepdims=True)
    zc = z - mu
    var = jnp.mean(zc * zc, axis=-1, keepdims=True)
    return zc * lax.rsqrt(var + LN_EPS) * g + b


def _gelu(y):
    return 0.5 * y * (1.0 + lax.erf(y * (1.0 / math.sqrt(2.0))))


def _sigmoid(y):
    return 1.0 / (1.0 + jnp.exp(-y))


def _softplus(y):
    return jnp.maximum(y, 0.0) + jnp.log(1.0 + jnp.exp(-jnp.abs(y)))


def _ln_kernel(x_ref, g_ref, b_ref, o_ref):
    o_ref[...] = _ln(x_ref[...], g_ref[...], b_ref[...])


def _layer_norm(x2, g, b):
    t, d = x2.shape
    tm = TM_PROJ
    return pl.pallas_call(
        _ln_kernel,
        grid=(t // tm,),
        in_specs=[pl.BlockSpec((tm, d), lambda i: (i, 0)),
                  pl.BlockSpec((1, d), lambda i: (0, 0)),
                  pl.BlockSpec((1, d), lambda i: (0, 0))],
        out_specs=pl.BlockSpec((tm, d), lambda i: (i, 0)),
        out_shape=jax.ShapeDtypeStruct((t, d), F32),
        compiler_params=_cparams(("parallel",)),
        name="ln_in",
    )(x2, g.reshape(1, d), b.reshape(1, d))


_P_RWKV = (0, 896)
_P_S5 = (896, 1152)
_P_FQKV = (1152, 1920)
_P_FF = (1920, 2048)
_P_CONV = (2048, 2560)
_P_COLS = 2560


def _proj_kernel(h_ref, w_ref, rw_ref, s5_ref, q_ref, k_ref, v_ref, ff_ref, cv_ref):
    x = h_ref[...].astype(BF16)
    for (lo, hi), o_ref in ((_P_RWKV, rw_ref), (_P_S5, s5_ref), (_P_FF, ff_ref), (_P_CONV, cv_ref)):
        o_ref[...] = jnp.dot(x, w_ref[:, lo:hi], preferred_element_type=F32)
    qkv = jnp.dot(x, w_ref[:, _P_FQKV[0]:_P_FQKV[1]], preferred_element_type=F32)
    tm = qkv.shape[0]
    ones_col = (lax.broadcasted_iota(I32, (tm, LANES - HEAD_DIM), 1) == 0).astype(F32)
    for hh in range(HEADS):
        q_ref[0, hh] = qkv[:, hh * HEAD_DIM:(hh + 1) * HEAD_DIM]
        k_ref[0, hh] = qkv[:, BRANCH + hh * HEAD_DIM:BRANCH + (hh + 1) * HEAD_DIM]
        v_ref[0, hh] = jnp.concatenate(
            [qkv[:, 2 * BRANCH + hh * HEAD_DIM:2 * BRANCH + (hh + 1) * HEAD_DIM], ones_col], axis=1)


def _project(h, w_packed, bsz, seq):
    t, d = h.shape
    tm = TM_PROJ
    nst = seq // tm
    w_rwkv, w_s5, w_ff, w_conv = [hi - lo for lo, hi in (_P_RWKV, _P_S5, _P_FF, _P_CONV)]
    tok = lambda i: (i, 0)
    heads = lambda i: (i // nst, 0, i % nst, 0)
    out_shapes = [jax.ShapeDtypeStruct((t, w_rwkv), F32),
                  jax.ShapeDtypeStruct((seq, bsz * w_s5), F32),
                  jax.ShapeDtypeStruct((bsz, HEADS, seq, HEAD_DIM), F32),
                  jax.ShapeDtypeStruct((bsz, HEADS, seq, HEAD_DIM), F32),
                  jax.ShapeDtypeStruct((bsz, HEADS, seq, LANES), F32),
                  jax.ShapeDtypeStruct((t, w_ff), F32),
                  jax.ShapeDtypeStruct((t, w_conv), F32)]
    out_specs = [pl.BlockSpec((tm, w_rwkv), tok),
                 pl.BlockSpec((tm, w_s5), lambda i: (i % nst, i // nst)),
                 pl.BlockSpec((1, HEADS, tm, HEAD_DIM), heads),
                 pl.BlockSpec((1, HEADS, tm, HEAD_DIM), heads),
                 pl.BlockSpec((1, HEADS, tm, LANES), heads),
                 pl.BlockSpec((tm, w_ff), tok),
                 pl.BlockSpec((tm, w_conv), tok)]
    return pl.pallas_call(
        _proj_kernel,
        grid=(t // tm,),
        in_specs=[pl.BlockSpec((tm, d), tok),
                  pl.BlockSpec((d, _P_COLS), lambda i: (0, 0))],
        out_specs=out_specs,
        out_shape=out_shapes,
        compiler_params=_cparams(("parallel",), VMEM_LIMIT),
        name="in_proj",
    )(h, w_packed)


def _rwkv_pre_kernel(z_ref, zp_ref, mu_ref, w0_ref, a0_ref, kk_ref, ka_ref, rk_ref, wl_ref, bd_ref,
                     r_out, w_out, k_out, v_out, na_out, b_out, g_out, bv_out):
    j = pl.program_id(1)
    z = z_ref[0]
    prev = zp_ref[0][SUBLANES - 1:SUBLANES, :]
    prev = jnp.where(j == 0, 0.0, prev)
    row = lax.broadcasted_iota(I32, z.shape, 0)
    zs = jnp.where(row == 0, prev, pltpu.roll(z, 1, 0))
    z = z + (zs - z) * mu_ref[...]
    r = z[:, 0:256]
    k = z[:, 256:512]
    v = z[:, 512:768]
    zc = z[:, 768:896]
    lane = lax.broadcasted_iota(I32, zc.shape, 1)
    act = jnp.where(lane < 32, jnp.tanh(zc), jnp.where(lane < 64, zc, _sigmoid(zc)))
    lo = jnp.dot(act, wl_ref[...], precision=HI, preferred_element_type=F32)
    dw = lo[:, 0:256]
    da = lo[:, 256:512]
    g = lo[:, 512:768]
    w_log = -_softplus(-(w0_ref[...] + dw)) - 0.5
    decay = jnp.exp(-jnp.exp(w_log))
    a = _sigmoid(a0_ref[...] + da)
    kkv = k * kk_ref[...]
    bd = bd_ref[...]
    ss = jnp.dot(kkv * kkv, bd, precision=HI, preferred_element_type=F32)
    kkn = kkv * lax.rsqrt(jnp.maximum(ss, 1e-24))
    k2 = k * (1.0 + (a - 1.0) * ka_ref[...])
    bonus = jnp.dot(r * k2 * rk_ref[...], bd, precision=HI, preferred_element_type=F32) * v
    r_out[0] = r
    w_out[0] = decay
    k_out[0] = k2
    v_out[0] = v
    na_out[0] = -kkn
    b_out[0] = kkn * a
    g_out[0] = g
    bv_out[0] = bonus


def _rwkv_pre(p_rwkv, mu, w0, a0, kk, ka, rk, wl, bd, bsz, seq):
    z3 = p_rwkv.reshape(bsz, seq, RWKV_COLS)
    tq = TQ_PRE
    blk = lambda b, j: (b, j, 0)
    par = lambda b, j: (0, 0)
    out_shape = [jax.ShapeDtypeStruct((bsz, seq, BRANCH), F32)] * 8
    out_specs = [pl.BlockSpec((1, tq, BRANCH), blk)] * 8
    return pl.pallas_call(
        _rwkv_pre_kernel,
        grid=(bsz, seq // tq),
        in_specs=[pl.BlockSpec((1, tq, RWKV_COLS), blk),
                  pl.BlockSpec((1, SUBLANES, RWKV_COLS),
                               lambda b, j: (b, jnp.maximum(j * (tq // SUBLANES) - 1, 0), 0)),
                  pl.BlockSpec((1, RWKV_COLS), par),
                  pl.BlockSpec((1, BRANCH), par), pl.BlockSpec((1, BRANCH), par),
                  pl.BlockSpec((1, BRANCH), par), pl.BlockSpec((1, BRANCH), par),
                  pl.BlockSpec((1, BRANCH), par),
                  pl.BlockSpec((LANES, 3 * BRANCH), par),
                  pl.BlockSpec((BRANCH, BRANCH), par)],
        out_specs=out_specs,
        out_shape=out_shape,
        compiler_params=_cparams(("parallel", "parallel")),
        name="rwkv_pre",
    )(z3, z3, mu.reshape(1, -1), w0.reshape(1, -1), a0.reshape(1, -1), kk.reshape(1, -1),
      ka.reshape(1, -1), rk.reshape(1, -1), wl, bd)


_SCAN_IG = HEAD_DIM // 2 // SUBLANES


def _rwkv_scan_kernel(a_ref, w_ref, b_ref, k_ref, r_ref, v_ref, y_ref, s_ref):
    @pl.when(pl.program_id(0) == 0)
    def _():
        s_ref[...] = jnp.zeros_like(s_ref)

    tc = a_ref.shape[0]
    tile = (SUBLANES, LANES)

    def step(t, carry):
        vv = [v_ref[t, ig * SUBLANES:(ig + 1) * SUBLANES, :] for ig in range(_SCAN_IG)]
        sa = [[jnp.zeros(tile, F32), jnp.zeros(tile, F32)] for _ in range(_SCAN_IG)]
        for j in range(HEAD_DIM):
            ab = jnp.broadcast_to(a_ref[t, j:j + 1, :], tile)
            for ig in range(_SCAN_IG):
                sa[ig][j % 2] = sa[ig][j % 2] + s_ref[ig, j] * ab
        sa = [x[0] + x[1] for x in sa]
        yy = [[jnp.zeros(tile, F32), jnp.zeros(tile, F32)] for _ in range(_SCAN_IG)]
        for j in range(HEAD_DIM):
            wb = jnp.broadcast_to(w_ref[t, j:j + 1, :], tile)
            bb = jnp.broadcast_to(b_ref[t, j:j + 1, :], tile)
            kb = jnp.broadcast_to(k_ref[t, j:j + 1, :], tile)
            rb = jnp.broadcast_to(r_ref[t, j:j + 1, :], tile)
            for ig in range(_SCAN_IG):
                s = s_ref[ig, j] * wb + sa[ig] * bb + vv[ig] * kb
                s_ref[ig, j] = s
                yy[ig][j % 2] = yy[ig][j % 2] + s * rb
        for ig in range(_SCAN_IG):
            y_ref[t, ig * SUBLANES:(ig + 1) * SUBLANES, :] = yy[ig][0] + yy[ig][1]
        return carry

    lax.fori_loop(0, tc, step, 0)


def _rwkv_scan(a_t, w_t, b_t, k_t, r_t, v_t):
    seq = a_t.shape[0]
    tc = TC_SCAN
    jspec = pl.BlockSpec((tc, HEAD_DIM, LANES), lambda i: (i, 0, 0))
    ispec = pl.BlockSpec((tc, HEAD_DIM // 2, LANES), lambda i: (i, 0, 0))
    return pl.pallas_call(
        _rwkv_scan_kernel,
        grid=(seq // tc,),
        in_specs=[jspec, jspec, jspec, jspec, jspec, ispec],
        out_specs=ispec,
        out_shape=jax.ShapeDtypeStruct((seq, HEAD_DIM // 2, LANES), F32),
        scratch_shapes=[pltpu.VMEM((_SCAN_IG, HEAD_DIM, SUBLANES, LANES), F32)],
        compiler_params=_cparams(("arbitrary",)),
        name="rwkv_scan",
    )(a_t, w_t, b_t, k_t, r_t, v_t)


def _to_scan_j(x, bsz, seq):
    n_pairs = bsz * HEADS
    xt = x.reshape(bsz, seq, HEADS, HEAD_DIM).transpose(1, 3, 0, 2).reshape(seq, HEAD_DIM, n_pairs)
    pad = LANES // 2 - n_pairs
    if pad:
        xt = jnp.pad(xt, ((0, 0), (0, 0), (0, pad)))
    return jnp.concatenate([xt, xt], axis=-1)


def _to_scan_i(x, bsz, seq):
    n_pairs = bsz * HEADS
    xt = x.reshape(bsz, seq, HEADS, 2, HEAD_DIM // 2).transpose(1, 4, 3, 0, 2)
    xt = xt.reshape(seq, HEAD_DIM // 2, 2, n_pairs)
    pad = LANES // 2 - n_pairs
    if pad:
        xt = jnp.pad(xt, ((0, 0), (0, 0), (0, 0), (0, pad)))
    return xt.reshape(seq, HEAD_DIM // 2, LANES)


def _from_scan_i(y, bsz, seq):
    n_pairs = bsz * HEADS
    yt = y.reshape(seq, HEAD_DIM // 2, 2, LANES // 2)[..., :n_pairs]
    yt = yt.reshape(seq, HEAD_DIM // 2, 2, bsz, HEADS).transpose(3, 0, 4, 2, 1)
    return yt.reshape(bsz * seq, BRANCH)


def _rwkv_post_kernel(y_ref, g_ref, bv_ref, gg_ref, gb_ref, bd_ref, o_ref):
    y = y_ref[...]
    bd = bd_ref[...]
    inv = 1.0 / HEAD_DIM
    mean = jnp.dot(y, bd, precision=HI, preferred_element_type=F32) * inv
    yc = y - mean
    var = jnp.dot(yc * yc, bd, precision=HI, preferred_element_type=F32) * inv
    yn = yc * lax.rsqrt(var + RWKV_GN_EPS) * gg_ref[...] + gb_ref[...]
    o_ref[...] = (yn + bv_ref[...]) * g_ref[...]


def _rwkv_post(y, g, bv, gn_g, gn_b, bd):
    t = y.shape[0]
    tm = TM_PROJ
    tok = lambda i: (i, 0)
    par = lambda i: (0, 0)
    return pl.pallas_call(
        _rwkv_post_kernel,
        grid=(t // tm,),
        in_specs=[pl.BlockSpec((tm, BRANCH), tok)] * 3
                 + [pl.BlockSpec((1, BRANCH), par)] * 2 + [pl.BlockSpec((BRANCH, BRANCH), par)],
        out_specs=pl.BlockSpec((tm, BRANCH), tok),
        out_shape=jax.ShapeDtypeStruct((t, BRANCH), F32),
        compiler_params=_cparams(("parallel",)),
        name="rwkv_post",
    )(y, g, bv, gn_g.reshape(1, -1), gn_b.reshape(1, -1), bd)


def _s5_kernel(u_ref, bre_ref, bim_ref, lre_ref, lim_ref, cc_ref, d_ref, gw_ref, gb_ref, o_ref,
               xre_ref, xim_ref, bure_ref, buim_ref, xs_ref):
    @pl.when(pl.program_id(0) == 0)
    def _():
        xre_ref[...] = jnp.zeros_like(xre_ref)
        xim_ref[...] = jnp.zeros_like(xim_ref)

    tc, nb, w = u_ref.shape
    u2 = u_ref[...].reshape(tc * nb, w)
    ub = u2.astype(BF16)
    bure_ref[...] = jnp.dot(ub, bre_ref[...], preferred_element_type=F32)
    buim_ref[...] = jnp.dot(ub, bim_ref[...], preferred_element_type=F32)
    lre = jnp.broadcast_to(lre_ref[...], (nb, S5_N))
    lim = jnp.broadcast_to(lim_ref[...], (nb, S5_N))

    def step(t, carry):
        xr, xi = carry
        r0 = pl.multiple_of(t * nb, nb)
        nr = lre * xr - lim * xi + bure_ref[pl.ds(r0, nb), :]
        ni = lre * xi + lim * xr + buim_ref[pl.ds(r0, nb), :]
        xs_ref[pl.ds(r0, nb), 0:S5_N] = nr
        xs_ref[pl.ds(r0, nb), S5_N:2 * S5_N] = ni
        return nr, ni

    xr, xi = lax.fori_loop(0, tc, step, (xre_ref[...], xim_ref[...]))
    xre_ref[...] = xr
    xim_ref[...] = xi
    y = jnp.dot(xs_ref[...].astype(BF16), cc_ref[...], preferred_element_type=F32) + d_ref[...] * u2
    y = _gelu(y)
    gate = jnp.dot(y.astype(BF16), gw_ref[...], preferred_element_type=F32) + gb_ref[...]
    o_ref[...] = (y * _sigmoid(gate)).reshape(tc, nb, w)


def _s5(u_t, bre, bim, lre, lim, cc, d_skip, glu_w, glu_b, bsz, seq):
    u3 = u_t.reshape(seq, bsz, BRANCH)
    tc = TC_S5
    par = lambda i: (0, 0)
    blk = pl.BlockSpec((tc, bsz, BRANCH), lambda i: (i, 0, 0))
    out = pl.pallas_call(
        _s5_kernel,
        grid=(seq // tc,),
        in_specs=[blk,
                  pl.BlockSpec((BRANCH, S5_N), par), pl.BlockSpec((BRANCH, S5_N), par),
                  pl.BlockSpec((1, S5_N), par), pl.BlockSpec((1, S5_N), par),
                  pl.BlockSpec((2 * S5_N, BRANCH), par),
                  pl.BlockSpec((1, BRANCH), par),
                  pl.BlockSpec((BRANCH, BRANCH), par),
                  pl.BlockSpec((1, BRANCH), par)],
        out_specs=blk,
        out_shape=jax.ShapeDtypeStruct((seq, bsz, BRANCH), F32),
        scratch_shapes=[pltpu.VMEM((bsz, S5_N), F32), pltpu.VMEM((bsz, S5_N), F32),
                        pltpu.VMEM((tc * bsz, S5_N), F32), pltpu.VMEM((tc * bsz, S5_N), F32),
                        pltpu.VMEM((tc * bsz, 2 * S5_N), F32)],
        compiler_params=_cparams(("arbitrary",), VMEM_LIMIT),
        name="s5_scan",
    )(u3, bre.astype(BF16), bim.astype(BF16), lre, lim, cc.astype(BF16), d_skip.reshape(1, -1),
      glu_w.astype(BF16), glu_b.reshape(1, -1))
    return out.reshape(seq, bsz * BRANCH)


def _s5_params(lam_re, lam_im, b_re, b_im, c_re, c_im, log_dt):
    lam = lax.complex(lam_re.astype(F32), lam_im.astype(F32))
    dt = jnp.exp(log_dt.astype(F32))[:, None]
    lam_bar = jnp.exp(lam * dt)
    b_bar = ((lam_bar - 1.0) / lam)[..., None] * lax.complex(b_re.astype(F32), b_im.astype(F32))
    eye = jnp.eye(S5_GROUPS, dtype=F32)
    bre = jnp.einsum('gpc,gh->gchp', jnp.real(b_bar), eye).reshape(BRANCH, S5_N)
    bim = jnp.einsum('gpc,gh->gchp', jnp.imag(b_bar), eye).reshape(BRANCH, S5_N)
    cre = jnp.einsum('gcp,gh->gphc', c_re.astype(F32), eye).reshape(S5_N, BRANCH)
    cim = jnp.einsum('gcp,gh->gphc', c_im.astype(F32), eye).reshape(S5_N, BRANCH)
    cc = jnp.concatenate([cre, -cim], axis=0)
    return bre, bim, jnp.real(lam_bar).reshape(1, S5_N), jnp.imag(lam_bar).reshape(1, S5_N), cc


def _fox_cumsum_kernel(f_ref, bias_ref, c_ref):
    x = f_ref[...] + bias_ref[...]
    c = -_softplus(-x)
    n = c.shape[1]
    lane = lax.broadcasted_iota(I32, c.shape, 1)
    sh = 1
    while sh < n:
        c = c + jnp.where(lane >= sh, pltpu.roll(c, sh, 1), 0.0)
        sh *= 2
    c_ref[...] = c


def _fox_cumsum(f_rows, bias_rows):
    rows, seq = f_rows.shape
    return pl.pallas_call(
        _fox_cumsum_kernel,
        out_shape=jax.ShapeDtypeStruct((rows, seq), F32),
        name="fox_cumsum",
    )(f_rows, bias_rows)


def _fox_attn_kernel(q_ref, k_ref, v_ref, cq_ref, ck_ref, o_ref):
    qi = pl.program_id(1)
    nh, tq = q_ref.shape[1], q_ref.shape[2]
    tk = ck_ref.shape[3]
    qs = [(q_ref[0, h] * (HEAD_DIM ** -0.5)).astype(BF16) for h in range(nh)]
    cqs = [cq_ref[0, h] for h in range(nh)]
    qpos = qi * tq + lax.broadcasted_iota(I32, (tq, tk), 0)
    nkb = (qi * tq + tq + tk - 1) // tk

    def body(kb, carry):
        k0 = pl.multiple_of(kb * tk, tk)
        causal = k0 + lax.broadcasted_iota(I32, (tq, tk), 1) <= qpos
        out = []
        for h in range(nh):
            m, acc = carry[h]
            kblk = k_ref[0, h, pl.ds(k0, tk), :].astype(BF16)
            vblk = v_ref[0, h, pl.ds(k0, tk), :].astype(BF16)
            s = lax.dot_general(qs[h], kblk, (((1,), (1,)), ((), ())), preferred_element_type=F32)
            s = s + cqs[h] - ck_ref[0, h, pl.ds(kb, 1), :]
            s = jnp.where(causal, s, -1e30)
            m_new = jnp.maximum(m, jnp.max(s, axis=-1, keepdims=True))
            p = jnp.exp(s - m_new)
            acc = jnp.exp(m - m_new) * acc + jnp.dot(p.astype(BF16), vblk, preferred_element_type=F32)
            out.append((m_new, acc))
        return tuple(out)

    init = tuple((jnp.full((tq, 1), -1e30, F32), jnp.zeros((tq, v_ref.shape[3]), F32)) for _ in range(nh))
    res = lax.fori_loop(0, nkb, body, init)
    for h in range(nh):
        acc = res[h][1]
        o_ref[0, h] = acc[:, 0:HEAD_DIM] / acc[:, HEAD_DIM:HEAD_DIM + 1]


def _fox_attn(q, k, v, c_col, c_rowb):
    bsz, nh, seq, hd = q.shape
    tq = TQ_ATT
    nkb, tk = c_rowb.shape[2], c_rowb.shape[3]
    full = lambda b, i: (b, 0, 0, 0)
    tile = lambda b, i: (b, 0, i, 0)
    return pl.pallas_call(
        _fox_attn_kernel,
        grid=(bsz, seq // tq),
        in_specs=[pl.BlockSpec((1, nh, tq, hd), tile),
                  pl.BlockSpec((1, nh, seq, hd), full),
                  pl.BlockSpec((1, nh, seq, v.shape[3]), full),
                  pl.BlockSpec((1, nh, tq, 1), tile),
                  pl.BlockSpec((1, nh, nkb, tk), full)],
        out_specs=pl.BlockSpec((1, nh, tq, hd), tile),
        out_shape=jax.ShapeDtypeStruct((bsz, nh, seq, hd), F32),
        compiler_params=_cparams(("parallel", "parallel"), VMEM_LIMIT),
        name="fox_attn",
    )(q, k, v, c_col, c_rowb)


def _conv_kernel(x_ref, xp_ref, w_ref, b_ref, g_ref, be_ref, o_ref, buf_ref):
    j = pl.program_id(1)
    tc = x_ref.shape[1]
    x = x_ref[0]
    buf_ref[CONV_HALO:CONV_HALO + tc, :] = x[:, 0:BRANCH] * _sigmoid(x[:, BRANCH:2 * BRANCH])
    xp = xp_ref[0]
    hp = xp[:, 0:BRANCH] * _sigmoid(xp[:, BRANCH:2 * BRANCH])
    buf_ref[0:CONV_HALO, :] = jnp.where(j == 0, 0.0, hp)
    acc = jnp.zeros((tc, BRANCH), F32) + b_ref[...]
    off = CONV_HALO - (CONV_WIDTH - 1)
    for kk in range(CONV_WIDTH):
        acc = acc + buf_ref[off + kk:off + kk + tc, :] * w_ref[kk:kk + 1, :]
    y = _ln(acc, g_ref[...], be_ref[...])
    o_ref[0] = y * _sigmoid(y)


def _conv_mixer(p_conv, conv_w, conv_b, ln_g, ln_b, bsz, seq):
    x3 = p_conv.reshape(bsz, seq, 2 * BRANCH)
    tc = TC_CONV
    par = lambda b, j: (0, 0)
    out = pl.pallas_call(
        _conv_kernel,
        grid=(bsz, seq // tc),
        in_specs=[pl.BlockSpec((1, tc, 2 * BRANCH), lambda b, j: (b, j, 0)),
                  pl.BlockSpec((1, CONV_HALO, 2 * BRANCH),
                               lambda b, j: (b, jnp.maximum(j * (tc // CONV_HALO) - 1, 0), 0)),
                  pl.BlockSpec((CONV_WIDTH, BRANCH), par),
                  pl.BlockSpec((1, BRANCH), par), pl.BlockSpec((1, BRANCH), par),
                  pl.BlockSpec((1, BRANCH), par)],
        out_specs=pl.BlockSpec((1, tc, BRANCH), lambda b, j: (b, j, 0)),
        out_shape=jax.ShapeDtypeStruct((bsz, seq, BRANCH), F32),
        scratch_shapes=[pltpu.VMEM((CONV_HALO + tc, BRANCH), F32)],
        compiler_params=_cparams(("parallel", "parallel")),
        name="conv_mixer",
    )(x3, x3, conv_w, conv_b.reshape(1, -1), ln_g.reshape(1, -1), ln_b.reshape(1, -1))
    return out.reshape(bsz * seq, BRANCH)


def _merge_kernel(h_ref, y0_ref, y1_ref, y2_ref, y3_ref, wg_ref, wb_ref, wo_ref, g_ref, b_ref, o_ref):
    h = h_ref[...]
    hb = h.astype(BF16)
    merged = None
    for br, y_ref in enumerate((y0_ref, y1_ref, y2_ref, y3_ref)):
        gate = _sigmoid(jnp.dot(hb, wg_ref[:, br * D_MODEL:(br + 1) * D_MODEL], preferred_element_type=F32))
        if br == 2:
            proj = None
            for hh in range(HEADS):
                part = jnp.dot(y_ref[0, hh].astype(BF16), wb_ref[br, hh * HEAD_DIM:(hh + 1) * HEAD_DIM, :],
                               preferred_element_type=F32)
                proj = part if proj is None else proj + part
        else:
            proj = jnp.dot(y_ref[...].astype(BF16), wb_ref[br], preferred_element_type=F32)
        term = gate * proj
        merged = term if merged is None else merged + term
    mix = jnp.dot(merged.astype(BF16), wo_ref[...], preferred_element_type=F32)
    o_ref[...] = _ln(DN_ALPHA * h + mix, g_ref[...], b_ref[...])


def _merge(h, y_rwkv, y_s5_t, y_fox, y_conv, wg, wb, wo, ln_g, ln_b, bsz, seq):
    t, d = h.shape
    tm = TM_MERGE
    nst = seq // tm
    tok = lambda i: (i, 0)
    par2 = lambda i: (0, 0)
    br_spec = pl.BlockSpec((tm, BRANCH), tok)
    return pl.pallas_call(
        _merge_kernel,
        grid=(t // tm,),
        in_specs=[pl.BlockSpec((tm, d), tok),
                  br_spec,
                  pl.BlockSpec((tm, BRANCH), lambda i: (i % nst, i // nst)),
                  pl.BlockSpec((1, HEADS, tm, HEAD_DIM), lambda i: (i // nst, 0, i % nst, 0)),
                  br_spec,
                  pl.BlockSpec((d, 4 * d), par2),
                  pl.BlockSpec((4, BRANCH, d), lambda i: (0, 0, 0)),
                  pl.BlockSpec((d, d), par2),
                  pl.BlockSpec((1, d), par2), pl.BlockSpec((1, d), par2)],
        out_specs=pl.BlockSpec((tm, d), tok),
        out_shape=jax.ShapeDtypeStruct((t, d), F32),
        compiler_params=_cparams(("parallel",), VMEM_LIMIT),
        name="merge_ln1",
    )(h, y_rwkv, y_s5_t, y_fox, y_conv, wg, wb, wo, ln_g.reshape(1, -1), ln_b.reshape(1, -1))


def _top_rows(s, key, payload=None):
    vals, keys, pays = [], [], []
    for _ in range(PEER_TOPK):
        m = jnp.max(s, axis=0, keepdims=True)
        kmin = jnp.min(jnp.where(s == m, key, jnp.int32(2 ** 30)), axis=0, keepdims=True)
        sel = key == kmin
        if payload is not None:
            pays.append(jnp.max(jnp.where(sel, payload, -1), axis=0, keepdims=True))
        s = jnp.where(sel, -jnp.inf, s)
        vals.append(m)
        keys.append(kmin)
    return vals, keys, pays


_CAND_ROWS = {2: 5, 3: 4, 4: 3, 5: 2, 6: 2, 7: 2}


def _candidates(v1, v2, i1, i2):
    tm = v1.shape[1]
    sub = lax.broadcasted_iota(I32, (SUBLANES, tm), 0)
    bc = lambda x, a: jnp.broadcast_to(x[a:a + 1, :], (SUBLANES, tm))
    vals, keys, eids = [], [], []
    for a, b0 in ((0, 0), (0, SUBLANES), (1, 0)):
        vals.append(bc(v1, a) + v2[b0:b0 + SUBLANES, :])
        keys.append(a * PEER_TOPK + b0 + sub)
        eids.append(bc(i1, a) * PEER_KEYS + i2[b0:b0 + SUBLANES, :])
    for a, nb in _CAND_ROWS.items():
        vals.append(jnp.where(sub < nb, bc(v1, a) + v2[0:SUBLANES, :], -jnp.inf))
        keys.append(a * PEER_TOPK + sub)
        eids.append(bc(i1, a) * PEER_KEYS + i2[0:SUBLANES, :])
    vals.append(v1[SUBLANES:, :] + bc(v2, 0))
    keys.append((SUBLANES + sub) * PEER_TOPK)
    eids.append(i1[SUBLANES:, :] * PEER_KEYS + bc(i2, 0))
    return jnp.concatenate(vals, axis=0), jnp.concatenate(keys, axis=0), jnp.concatenate(eids, axis=0)


_ROUTE_HEADS_PER_ITER = 4


def _route_kernel(h_ref, wq_ref, keys_ref, eid_ref, gate_ref, q_scr, eid_scr, gate_scr):
    tm = h_ref.shape[0]
    q = jnp.dot(h_ref[...].astype(BF16), wq_ref[...], preferred_element_type=F32)
    for c in range(2 * PEER_HEADS):
        q_scr[c] = q[:, c * PEER_HALF:(c + 1) * PEER_HALF]
    row = lax.broadcasted_iota(I32, (PEER_KEYS, tm), 0)

    def body(it, carry):
        for off in range(_ROUTE_HEADS_PER_ITER):
            hh = it * _ROUTE_HEADS_PER_ITER + off
            tops = []
            for half in range(2):
                st = lax.dot_general(keys_ref[half], q_scr[2 * hh + half], (((1,), (1,)), ((), ())),
                                     precision=HI, preferred_element_type=F32)
                vals, idxs, _ = _top_rows(st, row)
                tops.append((jnp.concatenate(vals, axis=0), jnp.concatenate(idxs, axis=0)))
            (v1, i1), (v2, i2) = tops
            cand, key, eids = _candidates(v1, v2, i1, i2)
            vals, _, pays = _top_rows(cand, key, payload=eids)
            score = jnp.concatenate(vals, axis=0)
            e = jnp.exp(score - jnp.max(score, axis=0, keepdims=True))
            r0 = pl.multiple_of(hh * PEER_TOPK, PEER_TOPK)
            gate_scr[pl.ds(r0, PEER_TOPK), :] = e / jnp.sum(e, axis=0, keepdims=True)
            eid_scr[pl.ds(r0, PEER_TOPK), :] = jnp.concatenate(pays, axis=0)
        return carry

    lax.fori_loop(0, PEER_HEADS // _ROUTE_HEADS_PER_ITER, body, 0)
    gate_ref[...] = gate_scr[...].T
    eid_ref[...] = eid_scr[...].T


def _route(h, wq, keys):
    t, d = h.shape
    tm = TM_ROUTE
    nq = 2 * PEER_HEADS
    return pl.pallas_call(
        _route_kernel,
        grid=(t // tm,),
        in_specs=[pl.BlockSpec((tm, d), lambda i: (i, 0)),
                  pl.BlockSpec((d, nq * PEER_HALF), lambda i: (0, 0)),
                  pl.BlockSpec((2, PEER_KEYS, PEER_HALF), lambda i: (0, 0, 0))],
        out_specs=[pl.BlockSpec((tm, PEER_SEL), lambda i: (i, 0)),
                   pl.BlockSpec((tm, PEER_SEL), lambda i: (i, 0))],
        out_shape=[jax.ShapeDtypeStruct((t, PEER_SEL), I32),
                   jax.ShapeDtypeStruct((t, PEER_SEL), F32)],
        scratch_shapes=[pltpu.VMEM((nq, tm, PEER_HALF), F32),
                        pltpu.VMEM((PEER_SEL, tm), I32),
                        pltpu.VMEM((PEER_SEL, tm), F32)],
        compiler_params=_cparams(("parallel",), VMEM_LIMIT),
        name="peer_route",
    )(h, wq, keys)


def _peer_kernel(eid_cur_ref, eid_nxt_ref, x_ref, gate_ref, cmp_ref, exp_ref, uv_hbm, o_ref, buf, sem):
    step = pl.program_id(0)
    nsteps = pl.num_programs(0)
    g_tok = G_PEER
    rows = g_tok * PEER_SEL
    nsub = D_MODEL // LANES
    per_piece = rows // (2 * g_tok)

    def slot_wait(sl):
        pltpu.make_async_copy(uv_hbm.at[pl.ds(0, rows)], buf.at[sl], sem.at[sl]).wait()

    @pl.when(step == 0)
    def _():
        def body(r, carry):
            pltpu.make_async_copy(uv_hbm.at[eid_cur_ref[0, 0, r]], buf.at[0, r], sem.at[0]).start()
            return carry
        lax.fori_loop(0, rows, body, 0, unroll=8)

    diag = (lax.broadcasted_iota(I32, (nsub, PEER_SEL * nsub), 1) % nsub
            == lax.broadcasted_iota(I32, (nsub, PEER_SEL * nsub), 0))

    def group(sl, idx_ref, idx_off):
        tok0 = sl * g_tok

        def prefetch(r0, n):
            for i in range(n):
                r = r0 + i
                pltpu.make_async_copy(uv_hbm.at[idx_ref[0, 0, idx_off + r]], buf.at[1 - sl, r],
                                      sem.at[1 - sl]).start(priority=i % 2)

        def tiles(g):
            return buf[sl, g * PEER_SEL:(g + 1) * PEER_SEL].reshape(PEER_SEL * nsub, LANES)

        slot_wait(sl)
        parts = []
        for g in range(g_tok):
            u_rows = pltpu.bitcast(tiles(g) << 16, F32).astype(BF16)
            x_row = x_ref[tok0 + g:tok0 + g + 1, :]
            x_tile = jnp.concatenate([x_row[:, s * LANES:(s + 1) * LANES] for s in range(nsub)], axis=0)
            q = lax.dot_general(x_tile.astype(BF16), u_rows, (((1,), (1,)), ((), ())),
                                preferred_element_type=F32)
            parts.append(jnp.where(diag, q, 0.0))
            prefetch(g * per_piece, per_piece)
        part = jnp.dot(jnp.concatenate(parts, axis=0), cmp_ref[...], precision=HI,
                       preferred_element_type=F32)
        act = jnp.sum(part.reshape(g_tok, nsub, PEER_SEL), axis=1)
        wgt = gate_ref[tok0:tok0 + g_tok, :] * _gelu(act)
        wrep = jnp.dot(wgt, exp_ref[...], precision=HI, preferred_element_type=F32)
        for g in range(g_tok):
            v_rows = pltpu.bitcast(tiles(g) & jnp.uint32(0xFFFF0000), F32).astype(BF16)
            wexp = jnp.where(diag, jnp.broadcast_to(wrep[g:g + 1, :], diag.shape), 0.0).astype(BF16)
            o_tile = jnp.dot(wexp, v_rows, preferred_element_type=F32)
            for s in range(nsub):
                o_ref[tok0 + g:tok0 + g + 1, s * LANES:(s + 1) * LANES] = o_tile[s:s + 1, :]
            prefetch((g_tok + g) * per_piece, per_piece)

    group(0, eid_cur_ref, rows)
    group(1, eid_nxt_ref, 0)

    @pl.when(step == nsteps - 1)
    def _():
        slot_wait(0)


def _peer(h, eid_tok, gate_tok, uv_packed):
    t, d = h.shape
    g_step = 2 * G_PEER
    nsteps = t // g_step
    rows = G_PEER * PEER_SEL
    nsub = d // LANES
    eid3 = eid_tok.reshape(nsteps, 1, 2 * rows)
    ncol = PEER_SEL * nsub
    col = jnp.arange(ncol) // nsub
    compress = (col[:, None] == jnp.arange(PEER_SEL)[None, :]).astype(F32)
    smem_blk = lambda f: pl.BlockSpec((1, 1, 2 * rows), f, memory_space=pltpu.SMEM)
    tile_blk = pl.BlockSpec((g_step, d), lambda i: (i, 0))
    return pl.pallas_call(
        _peer_kernel,
        grid=(nsteps,),
        in_specs=[smem_blk(lambda i: (i, 0, 0)),
                  smem_blk(lambda i: (jnp.minimum(i + 1, nsteps - 1), 0, 0)),
                  tile_blk,
                  pl.BlockSpec((g_step, PEER_SEL), lambda i: (i, 0)),
                  pl.BlockSpec((ncol, PEER_SEL), lambda i: (0, 0)),
                  pl.BlockSpec((PEER_SEL, ncol), lambda i: (0, 0)),
                  pl.BlockSpec(memory_space=pl.ANY)],
        out_specs=tile_blk,
        out_shape=jax.ShapeDtypeStruct((t, d), F32),
        scratch_shapes=[pltpu.VMEM((2, rows, nsub, LANES), U32), pltpu.SemaphoreType.DMA((2,))],
        compiler_params=_cparams(("arbitrary",), VMEM_LIMIT),
        name="peer_experts",
    )(eid3, eid3, h, gate_tok, compress, compress.T, uv_packed.reshape(-1, nsub, LANES))


def _pack_tables(u_tab, v_tab):
    ub = lax.bitcast_convert_type(u_tab.astype(BF16), jnp.uint16).astype(U32)
    vb = lax.bitcast_convert_type(v_tab.astype(BF16), jnp.uint16).astype(U32)
    return ub | (vb << 16)


def _out_kernel(h_ref, f_ref, p_ref, wp_ref, wgp_ref, g_ref, b_ref, o_ref):
    h = h_ref[...]
    ple = jnp.dot(p_ref[...].astype(BF16), wp_ref[...], preferred_element_type=F32)
    gate = _sigmoid(jnp.dot(h.astype(BF16), wgp_ref[...], preferred_element_type=F32))
    o_ref[...] = _ln(DN_ALPHA * h + f_ref[...] + ple * gate, g_ref[...], b_ref[...])


def _layer_out(h, ffn, p2, ple_w, ple_gate_w, ln_g, ln_b):
    t, d = h.shape
    tm = TM_OUT
    tok = lambda i: (i, 0)
    par = lambda i: (0, 0)
    return pl.pallas_call(
        _out_kernel,
        grid=(t // tm,),
        in_specs=[pl.BlockSpec((tm, d), tok), pl.BlockSpec((tm, d), tok),
                  pl.BlockSpec((tm, PLE_DIM), tok),
                  pl.BlockSpec((PLE_DIM, d), par), pl.BlockSpec((d, d), par),
                  pl.BlockSpec((1, d), par), pl.BlockSpec((1, d), par)],
        out_specs=pl.BlockSpec((tm, d), tok),
        out_shape=jax.ShapeDtypeStruct((t, d), F32),
        compiler_params=_cparams(("parallel",)),
        name="ple_ln2",
    )(h, ffn, p2, ple_w, ple_gate_w, ln_g.reshape(1, -1), ln_b.reshape(1, -1))


def _block_ones():
    head = jnp.arange(BRANCH) // HEAD_DIM
    return (head[:, None] == head[None, :]).astype(F32)


def _layer(h, p2, bsz, seq, w_in, rwkv_mu, rwkv_w0, rwkv_w2, rwkv_a0, rwkv_a2, rwkv_g2, rwkv_kk,
           rwkv_ka, rwkv_rk, rwkv_lnx_g, rwkv_lnx_b, s5_lam_re, s5_lam_im, s5_b_re, s5_b_im, s5_c_re,
           s5_c_im, s5_d, s5_log_dt, s5_glu_w, s5_glu_b, fox_bf, conv_w, conv_b, conv_ln_g, conv_ln_b,
           w_branch, w_out, ln1_g, ln1_b, peer_wq, peer_k1, peer_k2, peer_u, peer_v, ple_w, ple_gate_w,
           ln2_g, ln2_b):
    t = bsz * seq
    d = D_MODEL
    n_front = RWKV_COLS + BRANCH + 3 * BRANCH + HEADS
    w_front = w_in[:, :n_front]
    w_conv = w_in[:, n_front:n_front + 2 * BRANCH]
    w_gate = w_in[:, n_front + 2 * BRANCH:]
    pad = jnp.zeros((d, _P_FF[1] - _P_FF[0] - HEADS), w_in.dtype)
    w_packed = jnp.concatenate([w_front, pad, w_conv], axis=1).astype(BF16)
    bd = _block_ones()
    wl = jnp.zeros((LANES, 3 * BRANCH), F32)
    wl = wl.at[0:32, 0:BRANCH].set(rwkv_w2.astype(F32))
    wl = wl.at[32:64, BRANCH:2 * BRANCH].set(rwkv_a2.astype(F32))
    wl = wl.at[64:128, 2 * BRANCH:].set(rwkv_g2.astype(F32))

    p_rwkv, p_s5_t, q_heads, k_heads, v_aug, p_ff, p_conv = _project(h, w_packed, bsz, seq)

    r, dec, k2, v, na, bb, g, bonus = _rwkv_pre(p_rwkv, rwkv_mu, rwkv_w0, rwkv_a0, rwkv_kk, rwkv_ka,
                                                 rwkv_rk.reshape(-1), wl, bd, bsz, seq)
    y_scan = _rwkv_scan(_to_scan_j(na, bsz, seq), _to_scan_j(dec, bsz, seq), _to_scan_j(bb, bsz, seq),
                        _to_scan_j(k2, bsz, seq), _to_scan_j(r, bsz, seq), _to_scan_i(v, bsz, seq))
    y_rwkv = _rwkv_post(_from_scan_i(y_scan, bsz, seq), g.reshape(t, BRANCH), bonus.reshape(t, BRANCH),
                        rwkv_lnx_g, rwkv_lnx_b, bd)

    bre, bim, lre, lim, cc = _s5_params(s5_lam_re, s5_lam_im, s5_b_re, s5_b_im, s5_c_re, s5_c_im, s5_log_dt)
    y_s5_t = _s5(p_s5_t, bre, bim, lre, lim, cc, s5_d, s5_glu_w, s5_glu_b, bsz, seq)

    f_rows = p_ff[:, :HEADS].reshape(bsz, seq, HEADS).transpose(0, 2, 1).reshape(bsz * HEADS, seq)
    bias_rows = jnp.tile(fox_bf.astype(F32), bsz).reshape(bsz * HEADS, 1)
    c = _fox_cumsum(f_rows, bias_rows)
    y_fox = _fox_attn(q_heads, k_heads, v_aug,
                      c.reshape(bsz, HEADS, seq, 1), c.reshape(bsz, HEADS, seq // TK_ATT, TK_ATT))

    y_conv = _conv_mixer(p_conv, conv_w, conv_b, conv_ln_g, conv_ln_b, bsz, seq)

    h1 = _merge(h, y_rwkv, y_s5_t, y_fox, y_conv, w_gate.astype(BF16), w_branch.astype(BF16),
                w_out.astype(BF16), ln1_g, ln1_b, bsz, seq)

    keys = jnp.stack([peer_k1, peer_k2]).astype(F32)
    eid_tok, gate_tok = _route(h1, peer_wq.astype(BF16), keys)
    ffn = _peer(h1, eid_tok, gate_tok, _pack_tables(peer_u, peer_v))

    return _layer_out(h1, ffn, p2, ple_w.astype(BF16), ple_gate_w.astype(BF16), ln2_g, ln2_b)


def kernel(x, p, ln_in_g, ln_in_b, w_in, rwkv_mu, rwkv_w0, rwkv_w2, rwkv_a0, rwkv_a2, rwkv_g2, rwkv_kk, rwkv_ka, rwkv_rk, rwkv_lnx_g, rwkv_lnx_b, s5_lam_re, s5_lam_im, s5_b_re, s5_b_im, s5_c_re, s5_c_im, s5_d, s5_log_dt, s5_glu_w, s5_glu_b, fox_bf, conv_w, conv_b, conv_ln_g, conv_ln_b, w_branch, w_out, ln1_g, ln1_b, peer_wq, peer_k1, peer_k2, peer_u, peer_v, ple_w, ple_gate_w, ln2_g, ln2_b):
    bsz, seq, d = x.shape
    t = bsz * seq
    h = _layer_norm(x.reshape(t, d), ln_in_g, ln_in_b)
    per_layer = (w_in, rwkv_mu, rwkv_w0, rwkv_w2, rwkv_a0, rwkv_a2, rwkv_g2, rwkv_kk, rwkv_ka, rwkv_rk,
                 rwkv_lnx_g, rwkv_lnx_b, s5_lam_re, s5_lam_im, s5_b_re, s5_b_im, s5_c_re, s5_c_im, s5_d,
                 s5_log_dt, s5_glu_w, s5_glu_b, fox_bf, conv_w, conv_b, conv_ln_g, conv_ln_b, w_branch,
                 w_out, ln1_g, ln1_b, peer_wq, peer_k1, peer_k2, peer_u, peer_v, ple_w, ple_gate_w,
                 ln2_g, ln2_b)
    for i in range(p.shape[0]):
        h = _layer(h, p[i].reshape(t, PLE_DIM), bsz, seq, *(w[i] for w in per_layer))
    return h.reshape(bsz, seq, d)
```

```python
import functools
import math

import jax
import jax.numpy as jnp
from jax import lax
from jax.experimental import pallas as pl
from jax.experimental.pallas import tpu as pltpu

F32 = jnp.float32
BF16 = jnp.bfloat16
I32 = jnp.int32
U32 = jnp.uint32
HI = lax.Precision.HIGHEST

D_MODEL = 1024
BRANCH = 256
HEADS = 4
HEAD_DIM = 64
RWKV_COLS = 896
S5_GROUPS = 16
S5_GROUP = 16
S5_STATE = 64
S5_N = S5_GROUPS * S5_STATE
CONV_WIDTH = 31
CONV_HALO = 32
PEER_HEADS = 8
PEER_KEYS = 128
PEER_HALF = 128
PEER_TOPK = 16
PEER_SEL = PEER_HEADS * PEER_TOPK
PLE_DIM = 256
RWKV_GN_EPS = 64e-5
LN_EPS = 1e-5
DEPTH = 2
DN_ALPHA = (2 * DEPTH) ** 0.25

SUBLANES = 8
LANES = 128
VMEM_LIMIT = 56 * 1024 * 1024

TM_PROJ = 512
TQ_PRE = 256
TC_SCAN = 32
TC_S5 = 32
TQ_ATT = 256
TK_ATT = 256
TC_CONV = 512
TM_MERGE = 256
TM_ROUTE = 128
G_PEER = 16
TM_OUT = 256


def _cparams(sem, vmem=None):
    return pltpu.CompilerParams(dimension_semantics=sem, vmem_limit_bytes=vmem)


def _ln(z, g, b):
    mu = jnp.mean(z, axis=-1, keepdims=True)
    zc = z - mu
    var = jnp.mean(zc * zc, axis=-1, keepdims=True)
    return zc * lax.rsqrt(var + LN_EPS) * g + b


def _gelu(y):
    return 0.5 * y * (1.0 + lax.erf(y * (1.0 / math.sqrt(2.0))))


def _sigmoid(y):
    return 1.0 / (1.0 + jnp.exp(-y))


def _softplus(y):
    return jnp.maximum(y, 0.0) + jnp.log(1.0 + jnp.exp(-jnp.abs(y)))


def _ln_kernel(x_ref, g_ref, b_ref, o_ref):
    o_ref[...] = _ln(x_ref[...], g_ref[...], b_ref[...])


def _layer_norm(x2, g, b):
    t, d = x2.shape
    tm = TM_PROJ
    return pl.pallas_call(
        _ln_kernel,
        grid=(t // tm,),
        in_specs=[pl.BlockSpec((tm, d), lambda i: (i, 0)),
                  pl.BlockSpec((1, d), lambda i: (0, 0)),
                  pl.BlockSpec((1, d), lambda i: (0, 0))],
        out_specs=pl.BlockSpec((tm, d), lambda i: (i, 0)),
        out_shape=jax.ShapeDtypeStruct((t, d), F32),
        compiler_params=_cparams(("parallel",)),
        name="ln_in",
    )(x2, g.reshape(1, d), b.reshape(1, d))


_P_RWKV = (0, 896)
_P_S5 = (896, 1152)
_P_FQKV = (1152, 1920)
_P_FF = (1920, 2048)
_P_CONV = (2048, 2560)
_P_COLS = 2560


_N_FRONT = RWKV_COLS + BRANCH + 3 * BRANCH + HEADS
_TR_REPACK = 128


def _repack_kernel(w_ref, wp_ref, wg_ref):
    w = w_ref[0]
    lo, hi = _P_FF
    wp_ref[:, 0:lo] = w[:, 0:lo].astype(BF16)
    lane = lax.broadcasted_iota(I32, (w.shape[0], hi - lo), 1)
    wp_ref[:, lo:hi] = jnp.where(lane < HEADS, w[:, lo:hi], 0.0).astype(BF16)
    wp_ref[:, hi:_P_COLS] = w[:, _N_FRONT:_N_FRONT + 2 * BRANCH].astype(BF16)
    wg_ref[...] = w[:, _N_FRONT + 2 * BRANCH:].astype(BF16)


def _repack_w_in(w_in_all, layer):
    _, d, n = w_in_all.shape
    n_gate = n - _N_FRONT - 2 * BRANCH
    tr = _TR_REPACK
    return pl.pallas_call(
        _repack_kernel,
        grid=(d // tr,),
        in_specs=[pl.BlockSpec((1, tr, n), lambda i: (layer, i, 0))],
        out_specs=[pl.BlockSpec((tr, _P_COLS), lambda i: (i, 0)),
                   pl.BlockSpec((tr, n_gate), lambda i: (i, 0))],
        out_shape=[jax.ShapeDtypeStruct((d, _P_COLS), BF16),
                   jax.ShapeDtypeStruct((d, n_gate), BF16)],
        compiler_params=_cparams(("parallel",)),
        name="repack_w_in",
    )(w_in_all)


def _proj_kernel(h_ref, w_ref, rw_ref, s5_ref, q_ref, k_ref, v_ref, ff_ref, cv_ref):
    x = h_ref[...].astype(BF16)
    for (lo, hi), o_ref in ((_P_RWKV, rw_ref), (_P_S5, s5_ref), (_P_FF, ff_ref), (_P_CONV, cv_ref)):
        o_ref[...] = jnp.dot(x, w_ref[:, lo:hi], preferred_element_type=F32)
    qkv = jnp.dot(x, w_ref[:, _P_FQKV[0]:_P_FQKV[1]], preferred_element_type=F32)
    tm = qkv.shape[0]
    ones_col = (lax.broadcasted_iota(I32, (tm, LANES - HEAD_DIM), 1) == 0).astype(F32)
    for hh in range(HEADS):
        q_ref[0, hh] = qkv[:, hh * HEAD_DIM:(hh + 1) * HEAD_DIM]
        k_ref[0, hh] = qkv[:, BRANCH + hh * HEAD_DIM:BRANCH + (hh + 1) * HEAD_DIM]
        v_ref[0, hh] = jnp.concatenate(
            [qkv[:, 2 * BRANCH + hh * HEAD_DIM:2 * BRANCH + (hh + 1) * HEAD_DIM], ones_col], axis=1)


def _project(h, w_packed, bsz, seq):
    t, d = h.shape
    tm = TM_PROJ
    nst = seq // tm
    w_rwkv, w_s5, w_ff, w_conv = [hi - lo for lo, hi in (_P_RWKV, _P_S5, _P_FF, _P_CONV)]
    tok = lambda i: (i, 0)
    heads = lambda i: (i // nst, 0, i % nst, 0)
    out_shapes = [jax.ShapeDtypeStruct((t, w_rwkv), F32),
                  jax.ShapeDtypeStruct((seq, bsz * w_s5), F32),
                  jax.ShapeDtypeStruct((bsz, HEADS, seq, HEAD_DIM), F32),
                  jax.ShapeDtypeStruct((bsz, HEADS, seq, HEAD_DIM), F32),
                  jax.ShapeDtypeStruct((bsz, HEADS, seq, LANES), F32),
                  jax.ShapeDtypeStruct((t, w_ff), F32),
                  jax.ShapeDtypeStruct((t, w_conv), F32)]
    out_specs = [pl.BlockSpec((tm, w_rwkv), tok),
                 pl.BlockSpec((tm, w_s5), lambda i: (i % nst, i // nst)),
                 pl.BlockSpec((1, HEADS, tm, HEAD_DIM), heads),
                 pl.BlockSpec((1, HEADS, tm, HEAD_DIM), heads),
                 pl.BlockSpec((1, HEADS, tm, LANES), heads),
                 pl.BlockSpec((tm, w_ff), tok),
                 pl.BlockSpec((tm, w_conv), tok)]
    return pl.pallas_call(
        _proj_kernel,
        grid=(t // tm,),
        in_specs=[pl.BlockSpec((tm, d), tok),
                  pl.BlockSpec((d, _P_COLS), lambda i: (0, 0))],
        out_specs=out_specs,
        out_shape=out_shapes,
        compiler_params=_cparams(("parallel",), VMEM_LIMIT),
        name="in_proj",
    )(h, w_packed)


def _rwkv_pre_kernel(z_ref, zp_ref, mu_ref, w0_ref, a0_ref, kk_ref, ka_ref, rk_ref, wl_ref, bd_ref,
                     r_out, w_out, k_out, v_out, na_out, b_out, g_out, bv_out):
    j = pl.program_id(1)
    z = z_ref[0]
    prev = zp_ref[0][SUBLANES - 1:SUBLANES, :]
    prev = jnp.where(j == 0, 0.0, prev)
    row = lax.broadcasted_iota(I32, z.shape, 0)
    zs = jnp.where(row == 0, prev, pltpu.roll(z, 1, 0))
    z = z + (zs - z) * mu_ref[...]
    r = z[:, 0:256]
    k = z[:, 256:512]
    v = z[:, 512:768]
    zc = z[:, 768:896]
    lane = lax.broadcasted_iota(I32, zc.shape, 1)
    act = jnp.where(lane < 32, jnp.tanh(zc), jnp.where(lane < 64, zc, _sigmoid(zc)))
    lo = jnp.dot(act, wl_ref[...], precision=HI, preferred_element_type=F32)
    dw = lo[:, 0:256]
    da = lo[:, 256:512]
    g = lo[:, 512:768]
    w_log = -_softplus(-(w0_ref[...] + dw)) - 0.5
    decay = jnp.exp(-jnp.exp(w_log))
    a = _sigmoid(a0_ref[...] + da)
    kkv = k * kk_ref[...]
    bd = bd_ref[...]
    ss = jnp.dot(kkv * kkv, bd, precision=HI, preferred_element_type=F32)
    kkn = kkv * lax.rsqrt(jnp.maximum(ss, 1e-24))
    k2 = k * (1.0 + (a - 1.0) * ka_ref[...])
    bonus = jnp.dot(r * k2 * rk_ref[...], bd, precision=HI, preferred_element_type=F32) * v
    r_out[0] = r
    w_out[0] = decay
    k_out[0] = k2
    v_out[0] = v
    na_out[0] = -kkn
    b_out[0] = kkn * a
    g_out[0] = g
    bv_out[0] = bonus


def _rwkv_pre(p_rwkv, mu, w0, a0, kk, ka, rk, wl, bd, bsz, seq):
    z3 = p_rwkv.reshape(bsz, seq, RWKV_COLS)
    tq = TQ_PRE
    blk = lambda b, j: (b, j, 0)
    par = lambda b, j: (0, 0)
    out_shape = [jax.ShapeDtypeStruct((bsz, seq, BRANCH), F32)] * 8
    out_specs = [pl.BlockSpec((1, tq, BRANCH), blk)] * 8
    return pl.pallas_call(
        _rwkv_pre_kernel,
        grid=(bsz, seq // tq),
        in_specs=[pl.BlockSpec((1, tq, RWKV_COLS), blk),
                  pl.BlockSpec((1, SUBLANES, RWKV_COLS),
                               lambda b, j: (b, jnp.maximum(j * (tq // SUBLANES) - 1, 0), 0)),
                  pl.BlockSpec((1, RWKV_COLS), par),
                  pl.BlockSpec((1, BRANCH), par), pl.BlockSpec((1, BRANCH), par),
                  pl.BlockSpec((1, BRANCH), par), pl.BlockSpec((1, BRANCH), par),
                  pl.BlockSpec((1, BRANCH), par),
                  pl.BlockSpec((LANES, 3 * BRANCH), par),
                  pl.BlockSpec((BRANCH, BRANCH), par)],
        out_specs=out_specs,
        out_shape=out_shape,
        compiler_params=_cparams(("parallel", "parallel")),
        name="rwkv_pre",
    )(z3, z3, mu.reshape(1, -1), w0.reshape(1, -1), a0.reshape(1, -1), kk.reshape(1, -1),
      ka.reshape(1, -1), rk.reshape(1, -1), wl, bd)


_SCAN_IG = HEAD_DIM // SUBLANES
_SCAN_JL = HEAD_DIM // 2


def _rwkv_scan_kernel(a_ref, w_ref, b_ref, k_ref, r_ref, v_ref, y_ref, s_ref):
    @pl.when(pl.program_id(0) == 0)
    def _():
        s_ref[...] = jnp.zeros_like(s_ref)

    tc = a_ref.shape[0]
    tile = (SUBLANES, LANES)
    both_halves = lambda x: x + pltpu.roll(x, LANES // 2, 1)

    def step(t, carry):
        vv = [v_ref[t, ig * SUBLANES:(ig + 1) * SUBLANES, :] for ig in range(_SCAN_IG)]
        sa = [[jnp.zeros(tile, F32), jnp.zeros(tile, F32)] for _ in range(_SCAN_IG)]
        for jl in range(_SCAN_JL):
            ab = jnp.broadcast_to(a_ref[t, jl:jl + 1, :], tile)
            for ig in range(_SCAN_IG):
                sa[ig][jl % 2] = sa[ig][jl % 2] + s_ref[ig, jl] * ab
        sa = [both_halves(x[0] + x[1]) for x in sa]
        yy = [[jnp.zeros(tile, F32), jnp.zeros(tile, F32)] for _ in range(_SCAN_IG)]
        for jl in range(_SCAN_JL):
            wb = jnp.broadcast_to(w_ref[t, jl:jl + 1, :], tile)
            bb = jnp.broadcast_to(b_ref[t, jl:jl + 1, :], tile)
            kb = jnp.broadcast_to(k_ref[t, jl:jl + 1, :], tile)
            rb = jnp.broadcast_to(r_ref[t, jl:jl + 1, :], tile)
            for ig in range(_SCAN_IG):
                s = s_ref[ig, jl] * wb + sa[ig] * bb + vv[ig] * kb
                s_ref[ig, jl] = s
                yy[ig][jl % 2] = yy[ig][jl % 2] + s * rb
        for ig in range(_SCAN_IG):
            y_ref[t, ig * SUBLANES:(ig + 1) * SUBLANES, :] = both_halves(yy[ig][0] + yy[ig][1])
        return carry

    lax.fori_loop(0, tc, step, 0)


def _rwkv_scan(a_t, w_t, b_t, k_t, r_t, v_t):
    seq = a_t.shape[0]
    tc = TC_SCAN
    jspec = pl.BlockSpec((tc, _SCAN_JL, LANES), lambda i: (i, 0, 0))
    ispec = pl.BlockSpec((tc, HEAD_DIM, LANES), lambda i: (i, 0, 0))
    return pl.pallas_call(
        _rwkv_scan_kernel,
        grid=(seq // tc,),
        in_specs=[jspec, jspec, jspec, jspec, jspec, ispec],
        out_specs=ispec,
        out_shape=jax.ShapeDtypeStruct((seq, HEAD_DIM, LANES), F32),
        scratch_shapes=[pltpu.VMEM((_SCAN_IG, _SCAN_JL, SUBLANES, LANES), F32)],
        compiler_params=_cparams(("arbitrary",)),
        name="rwkv_scan",
    )(a_t, w_t, b_t, k_t, r_t, v_t)


def _to_scan_i(x, bsz, seq):
    n_pairs = bsz * HEADS
    xt = x.reshape(bsz, seq, HEADS, HEAD_DIM).transpose(1, 3, 0, 2).reshape(seq, HEAD_DIM, n_pairs)
    pad = LANES // 2 - n_pairs
    if pad:
        xt = jnp.pad(xt, ((0, 0), (0, 0), (0, pad)))
    return jnp.concatenate([xt, xt], axis=-1)


def _to_scan_j(x, bsz, seq):
    n_pairs = bsz * HEADS
    xt = x.reshape(bsz, seq, HEADS, 2, HEAD_DIM // 2).transpose(1, 4, 3, 0, 2)
    xt = xt.reshape(seq, HEAD_DIM // 2, 2, n_pairs)
    pad = LANES // 2 - n_pairs
    if pad:
        xt = jnp.pad(xt, ((0, 0), (0, 0), (0, 0), (0, pad)))
    return xt.reshape(seq, HEAD_DIM // 2, LANES)


def _from_scan_i(y, bsz, seq):
    n_pairs = bsz * HEADS
    yt = y[..., :n_pairs].reshape(seq, HEAD_DIM, bsz, HEADS).transpose(2, 0, 3, 1)
    return yt.reshape(bsz * seq, BRANCH)


def _rwkv_post_kernel(y_ref, g_ref, bv_ref, gg_ref, gb_ref, bd_ref, o_ref):
    y = y_ref[...]
    bd = bd_ref[...]
    inv = 1.0 / HEAD_DIM
    mean = jnp.dot(y, bd, precision=HI, preferred_element_type=F32) * inv
    yc = y - mean
    var = jnp.dot(yc * yc, bd, precision=HI, preferred_element_type=F32) * inv
    yn = yc * lax.rsqrt(var + RWKV_GN_EPS) * gg_ref[...] + gb_ref[...]
    o_ref[...] = (yn + bv_ref[...]) * g_ref[...]


def _rwkv_post(y, g, bv, gn_g, gn_b, bd):
    t = y.shape[0]
    tm = TM_PROJ
    tok = lambda i: (i, 0)
    par = lambda i: (0, 0)
    return pl.pallas_call(
        _rwkv_post_kernel,
        grid=(t // tm,),
        in_specs=[pl.BlockSpec((tm, BRANCH), tok)] * 3
                 + [pl.BlockSpec((1, BRANCH), par)] * 2 + [pl.BlockSpec((BRANCH, BRANCH), par)],
        out_specs=pl.BlockSpec((tm, BRANCH), tok),
        out_shape=jax.ShapeDtypeStruct((t, BRANCH), F32),
        compiler_params=_cparams(("parallel",)),
        name="rwkv_post",
    )(y, g, bv, gn_g.reshape(1, -1), gn_b.reshape(1, -1), bd)


def _s5_kernel(u_ref, bre_ref, bim_ref, lre_ref, lim_ref, cc_ref, d_ref, gw_ref, gb_ref, o_ref,
               xre_ref, xim_ref, bure_ref, buim_ref, xs_ref):
    @pl.when(pl.program_id(0) == 0)
    def _():
        xre_ref[...] = jnp.zeros_like(xre_ref)
        xim_ref[...] = jnp.zeros_like(xim_ref)

    tc, nb, w = u_ref.shape
    u2 = u_ref[...].reshape(tc * nb, w)
    ub = u2.astype(BF16)
    bure_ref[...] = jnp.dot(ub, bre_ref[...], preferred_element_type=F32)
    buim_ref[...] = jnp.dot(ub, bim_ref[...], preferred_element_type=F32)
    lre = jnp.broadcast_to(lre_ref[...], (nb, S5_N))
    lim = jnp.broadcast_to(lim_ref[...], (nb, S5_N))

    def step(t, carry):
        xr, xi = carry
        r0 = pl.multiple_of(t * nb, nb)
        nr = lre * xr - lim * xi + bure_ref[pl.ds(r0, nb), :]
        ni = lre * xi + lim * xr + buim_ref[pl.ds(r0, nb), :]
        xs_ref[pl.ds(r0, nb), 0:S5_N] = nr
        xs_ref[pl.ds(r0, nb), S5_N:2 * S5_N] = ni
        return nr, ni

    xr, xi = lax.fori_loop(0, tc, step, (xre_ref[...], xim_ref[...]))
    xre_ref[...] = xr
    xim_ref[...] = xi
    y = jnp.dot(xs_ref[...].astype(BF16), cc_ref[...], preferred_element_type=F32) + d_ref[...] * u2
    y = _gelu(y)
    gate = jnp.dot(y.astype(BF16), gw_ref[...], preferred_element_type=F32) + gb_ref[...]
    o_ref[...] = (y * _sigmoid(gate)).reshape(tc, nb, w)


def _s5(u_t, bre, bim, lre, lim, cc, d_skip, glu_w, glu_b, bsz, seq):
    u3 = u_t.reshape(seq, bsz, BRANCH)
    tc = TC_S5
    par = lambda i: (0, 0)
    blk = pl.BlockSpec((tc, bsz, BRANCH), lambda i: (i, 0, 0))
    out = pl.pallas_call(
        _s5_kernel,
        grid=(seq // tc,),
        in_specs=[blk,
                  pl.BlockSpec((BRANCH, S5_N), par), pl.BlockSpec((BRANCH, S5_N), par),
                  pl.BlockSpec((1, S5_N), par), pl.BlockSpec((1, S5_N), par),
                  pl.BlockSpec((2 * S5_N, BRANCH), par),
                  pl.BlockSpec((1, BRANCH), par),
                  pl.BlockSpec((BRANCH, BRANCH), par),
                  pl.BlockSpec((1, BRANCH), par)],
        out_specs=blk,
        out_shape=jax.ShapeDtypeStruct((seq, bsz, BRANCH), F32),
        scratch_shapes=[pltpu.VMEM((bsz, S5_N), F32), pltpu.VMEM((bsz, S5_N), F32),
                        pltpu.VMEM((tc * bsz, S5_N), F32), pltpu.VMEM((tc * bsz, S5_N), F32),
                        pltpu.VMEM((tc * bsz, 2 * S5_N), F32)],
        compiler_params=_cparams(("arbitrary",), VMEM_LIMIT),
        name="s5_scan",
    )(u3, bre.astype(BF16), bim.astype(BF16), lre, lim, cc.astype(BF16), d_skip.reshape(1, -1),
      glu_w.astype(BF16), glu_b.reshape(1, -1))
    return out.reshape(seq, bsz * BRANCH)


def _s5_params(lam_re, lam_im, b_re, b_im, c_re, c_im, log_dt):
    lam = lax.complex(lam_re.astype(F32), lam_im.astype(F32))
    dt = jnp.exp(log_dt.astype(F32))[:, None]
    lam_bar = jnp.exp(lam * dt)
    b_bar = ((lam_bar - 1.0) / lam)[..., None] * lax.complex(b_re.astype(F32), b_im.astype(F32))
    eye = jnp.eye(S5_GROUPS, dtype=F32)
    bre = jnp.einsum('gpc,gh->gchp', jnp.real(b_bar), eye).reshape(BRANCH, S5_N)
    bim = jnp.einsum('gpc,gh->gchp', jnp.imag(b_bar), eye).reshape(BRANCH, S5_N)
    cre = jnp.einsum('gcp,gh->gphc', c_re.astype(F32), eye).reshape(S5_N, BRANCH)
    cim = jnp.einsum('gcp,gh->gphc', c_im.astype(F32), eye).reshape(S5_N, BRANCH)
    cc = jnp.concatenate([cre, -cim], axis=0)
    return bre, bim, jnp.real(lam_bar).reshape(1, S5_N), jnp.imag(lam_bar).reshape(1, S5_N), cc


def _fox_cumsum_kernel(f_ref, bias_ref, c_ref):
    x = f_ref[...] + bias_ref[...]
    c = -_softplus(-x)
    n = c.shape[1]
    lane = lax.broadcasted_iota(I32, c.shape, 1)
    sh = 1
    while sh < n:
        c = c + jnp.where(lane >= sh, pltpu.roll(c, sh, 1), 0.0)
        sh *= 2
    c_ref[...] = c


def _fox_cumsum(f_rows, bias_rows):
    rows, seq = f_rows.shape
    return pl.pallas_call(
        _fox_cumsum_kernel,
        out_shape=jax.ShapeDtypeStruct((rows, seq), F32),
        name="fox_cumsum",
    )(f_rows, bias_rows)


def _fox_attn_kernel(q_ref, k_ref, v_ref, cq_ref, ck_ref, o_ref):
    qi = pl.program_id(1)
    nh, tq = q_ref.shape[1], q_ref.shape[2]
    tk = ck_ref.shape[3]
    qs = [(q_ref[0, h] * (HEAD_DIM ** -0.5)).astype(BF16) for h in range(nh)]
    cqs = [cq_ref[0, h] for h in range(nh)]
    qpos = qi * tq + lax.broadcasted_iota(I32, (tq, tk), 0)
    nkb = (qi * tq + tq + tk - 1) // tk

    def body(kb, carry):
        k0 = pl.multiple_of(kb * tk, tk)
        causal = k0 + lax.broadcasted_iota(I32, (tq, tk), 1) <= qpos
        out = []
        for h in range(nh):
            m, acc = carry[h]
            kblk = k_ref[0, h, pl.ds(k0, tk), :].astype(BF16)
            vblk = v_ref[0, h, pl.ds(k0, tk), :].astype(BF16)
            s = lax.dot_general(qs[h], kblk, (((1,), (1,)), ((), ())), preferred_element_type=F32)
            s = s + cqs[h] - ck_ref[0, h, pl.ds(kb, 1), :]
            s = jnp.where(causal, s, -1e30)
            m_new = jnp.maximum(m, jnp.max(s, axis=-1, keepdims=True))
            p = jnp.exp(s - m_new)
            acc = jnp.exp(m - m_new) * acc + jnp.dot(p.astype(BF16), vblk, preferred_element_type=F32)
            out.append((m_new, acc))
        return tuple(out)

    init = tuple((jnp.full((tq, 1), -1e30, F32), jnp.zeros((tq, v_ref.shape[3]), F32)) for _ in range(nh))
    res = lax.fori_loop(0, nkb, body, init)
    for h in range(nh):
        acc = res[h][1]
        o_ref[0, h] = acc[:, 0:HEAD_DIM] / acc[:, HEAD_DIM:HEAD_DIM + 1]


def _fox_attn(q, k, v, c_col, c_rowb):
    bsz, nh, seq, hd = q.shape
    tq = TQ_ATT
    nkb, tk = c_rowb.shape[2], c_rowb.shape[3]
    full = lambda b, i: (b, 0, 0, 0)
    tile = lambda b, i: (b, 0, i, 0)
    return pl.pallas_call(
        _fox_attn_kernel,
        grid=(bsz, seq // tq),
        in_specs=[pl.BlockSpec((1, nh, tq, hd), tile),
                  pl.BlockSpec((1, nh, seq, hd), full),
                  pl.BlockSpec((1, nh, seq, v.shape[3]), full),
                  pl.BlockSpec((1, nh, tq, 1), tile),
                  pl.BlockSpec((1, nh, nkb, tk), full)],
        out_specs=pl.BlockSpec((1, nh, tq, hd), tile),
        out_shape=jax.ShapeDtypeStruct((bsz, nh, seq, hd), F32),
        compiler_params=_cparams(("parallel", "parallel"), VMEM_LIMIT),
        name="fox_attn",
    )(q, k, v, c_col, c_rowb)


def _conv_kernel(x_ref, xp_ref, w_ref, b_ref, g_ref, be_ref, o_ref, buf_ref):
    j = pl.program_id(1)
    tc = x_ref.shape[1]
    x = x_ref[0]
    buf_ref[CONV_HALO:CONV_HALO + tc, :] = x[:, 0:BRANCH] * _sigmoid(x[:, BRANCH:2 * BRANCH])
    xp = xp_ref[0]
    hp = xp[:, 0:BRANCH] * _sigmoid(xp[:, BRANCH:2 * BRANCH])
    buf_ref[0:CONV_HALO, :] = jnp.where(j == 0, 0.0, hp)
    acc = jnp.zeros((tc, BRANCH), F32) + b_ref[...]
    off = CONV_HALO - (CONV_WIDTH - 1)
    for kk in range(CONV_WIDTH):
        acc = acc + buf_ref[off + kk:off + kk + tc, :] * w_ref[kk:kk + 1, :]
    y = _ln(acc, g_ref[...], be_ref[...])
    o_ref[0] = y * _sigmoid(y)


def _conv_mixer(p_conv, conv_w, conv_b, ln_g, ln_b, bsz, seq):
    x3 = p_conv.reshape(bsz, seq, 2 * BRANCH)
    tc = TC_CONV
    par = lambda b, j: (0, 0)
    out = pl.pallas_call(
        _conv_kernel,
        grid=(bsz, seq // tc),
        in_specs=[pl.BlockSpec((1, tc, 2 * BRANCH), lambda b, j: (b, j, 0)),
                  pl.BlockSpec((1, CONV_HALO, 2 * BRANCH),
                               lambda b, j: (b, jnp.maximum(j * (tc // CONV_HALO) - 1, 0), 0)),
                  pl.BlockSpec((CONV_WIDTH, BRANCH), par),
                  pl.BlockSpec((1, BRANCH), par), pl.BlockSpec((1, BRANCH), par),
                  pl.BlockSpec((1, BRANCH), par)],
        out_specs=pl.BlockSpec((1, tc, BRANCH), lambda b, j: (b, j, 0)),
        out_shape=jax.ShapeDtypeStruct((bsz, seq, BRANCH), F32),
        scratch_shapes=[pltpu.VMEM((CONV_HALO + tc, BRANCH), F32)],
        compiler_params=_cparams(("parallel", "parallel")),
        name="conv_mixer",
    )(x3, x3, conv_w, conv_b.reshape(1, -1), ln_g.reshape(1, -1), ln_b.reshape(1, -1))
    return out.reshape(bsz * seq, BRANCH)


def _merge_kernel(h_ref, y0_ref, y1_ref, y2_ref, y3_ref, wg_ref, wb_ref, wo_ref, g_ref, b_ref, o_ref):
    h = h_ref[...]
    hb = h.astype(BF16)
    merged = None
    for br, y_ref in enumerate((y0_ref, y1_ref, y2_ref, y3_ref)):
        gate = _sigmoid(jnp.dot(hb, wg_ref[:, br * D_MODEL:(br + 1) * D_MODEL], preferred_element_type=F32))
        if br == 2:
            proj = None
            for hh in range(HEADS):
                part = jnp.dot(y_ref[0, hh].astype(BF16), wb_ref[br, hh * HEAD_DIM:(hh + 1) * HEAD_DIM, :],
                               preferred_element_type=F32)
                proj = part if proj is None else proj + part
        else:
            proj = jnp.dot(y_ref[...].astype(BF16), wb_ref[br], preferred_element_type=F32)
        term = gate * proj
        merged = term if merged is None else merged + term
    mix = jnp.dot(merged.astype(BF16), wo_ref[...], preferred_element_type=F32)
    o_ref[...] = _ln(DN_ALPHA * h + mix, g_ref[...], b_ref[...])


def _merge(h, y_rwkv, y_s5_t, y_fox, y_conv, wg, wb, wo, ln_g, ln_b, bsz, seq):
    t, d = h.shape
    tm = TM_MERGE
    nst = seq // tm
    tok = lambda i: (i, 0)
    par2 = lambda i: (0, 0)
    br_spec = pl.BlockSpec((tm, BRANCH), tok)
    return pl.pallas_call(
        _merge_kernel,
        grid=(t // tm,),
        in_specs=[pl.BlockSpec((tm, d), tok),
                  br_spec,
                  pl.BlockSpec((tm, BRANCH), lambda i: (i % nst, i // nst)),
                  pl.BlockSpec((1, HEADS, tm, HEAD_DIM), lambda i: (i // nst, 0, i % nst, 0)),
                  br_spec,
                  pl.BlockSpec((d, 4 * d), par2),
                  pl.BlockSpec((4, BRANCH, d), lambda i: (0, 0, 0)),
                  pl.BlockSpec((d, d), par2),
                  pl.BlockSpec((1, d), par2), pl.BlockSpec((1, d), par2)],
        out_specs=pl.BlockSpec((tm, d), tok),
        out_shape=jax.ShapeDtypeStruct((t, d), F32),
        compiler_params=_cparams(("parallel",), VMEM_LIMIT),
        name="merge_ln1",
    )(h, y_rwkv, y_s5_t, y_fox, y_conv, wg, wb, wo, ln_g.reshape(1, -1), ln_b.reshape(1, -1))


def _top_rows(s, key, payload=None):
    vals, keys, pays = [], [], []
    for _ in range(PEER_TOPK):
        m = jnp.max(s, axis=0, keepdims=True)
        kmin = jnp.min(jnp.where(s == m, key, jnp.int32(2 ** 30)), axis=0, keepdims=True)
        sel = key == kmin
        if payload is not None:
            pays.append(jnp.max(jnp.where(sel, payload, -1), axis=0, keepdims=True))
        s = jnp.where(sel, -jnp.inf, s)
        vals.append(m)
        keys.append(kmin)
    return vals, keys, pays


_CAND_ROWS = {2: 5, 3: 4, 4: 3, 5: 2, 6: 2, 7: 2}


def _candidates(v1, v2, i1, i2):
    tm = v1.shape[1]
    sub = lax.broadcasted_iota(I32, (SUBLANES, tm), 0)
    bc = lambda x, a: jnp.broadcast_to(x[a:a + 1, :], (SUBLANES, tm))
    vals, keys, eids = [], [], []
    for a, b0 in ((0, 0), (0, SUBLANES), (1, 0)):
        vals.append(bc(v1, a) + v2[b0:b0 + SUBLANES, :])
        keys.append(a * PEER_TOPK + b0 + sub)
        eids.append(bc(i1, a) * PEER_KEYS + i2[b0:b0 + SUBLANES, :])
    for a, nb in _CAND_ROWS.items():
        vals.append(jnp.where(sub < nb, bc(v1, a) + v2[0:SUBLANES, :], -jnp.inf))
        keys.append(a * PEER_TOPK + sub)
        eids.append(bc(i1, a) * PEER_KEYS + i2[0:SUBLANES, :])
    vals.append(v1[SUBLANES:, :] + bc(v2, 0))
    keys.append((SUBLANES + sub) * PEER_TOPK)
    eids.append(i1[SUBLANES:, :] * PEER_KEYS + bc(i2, 0))
    return jnp.concatenate(vals, axis=0), jnp.concatenate(keys, axis=0), jnp.concatenate(eids, axis=0)


_ROUTE_HEADS_PER_ITER = 4


def _route_kernel(h_ref, wq_ref, keys_ref, eid_ref, gate_ref, q_scr, eid_scr, gate_scr):
    tm = h_ref.shape[0]
    q = jnp.dot(h_ref[...].astype(BF16), wq_ref[...], preferred_element_type=F32)
    for c in range(2 * PEER_HEADS):
        q_scr[c] = q[:, c * PEER_HALF:(c + 1) * PEER_HALF]
    row = lax.broadcasted_iota(I32, (PEER_KEYS, tm), 0)

    def body(it, carry):
        for off in range(_ROUTE_HEADS_PER_ITER):
            hh = it * _ROUTE_HEADS_PER_ITER + off
            tops = []
            for half in range(2):
                st = lax.dot_general(keys_ref[half], q_scr[2 * hh + half], (((1,), (1,)), ((), ())),
                                     precision=HI, preferred_element_type=F32)
                vals, idxs, _ = _top_rows(st, row)
                tops.append((jnp.concatenate(vals, axis=0), jnp.concatenate(idxs, axis=0)))
            (v1, i1), (v2, i2) = tops
            cand, key, eids = _candidates(v1, v2, i1, i2)
            vals, _, pays = _top_rows(cand, key, payload=eids)
            score = jnp.concatenate(vals, axis=0)
            e = jnp.exp(score - jnp.max(score, axis=0, keepdims=True))
            r0 = pl.multiple_of(hh * PEER_TOPK, PEER_TOPK)
            gate_scr[pl.ds(r0, PEER_TOPK), :] = e / jnp.sum(e, axis=0, keepdims=True)
            eid_scr[pl.ds(r0, PEER_TOPK), :] = jnp.concatenate(pays, axis=0)
        return carry

    lax.fori_loop(0, PEER_HEADS // _ROUTE_HEADS_PER_ITER, body, 0)
    gate_ref[...] = gate_scr[...].T
    eid_ref[...] = eid_scr[...].T


def _route(h, wq, keys):
    t, d = h.shape
    tm = TM_ROUTE
    nq = 2 * PEER_HEADS
    return pl.pallas_call(
        _route_kernel,
        grid=(t // tm,),
        in_specs=[pl.BlockSpec((tm, d), lambda i: (i, 0)),
                  pl.BlockSpec((d, nq * PEER_HALF), lambda i: (0, 0)),
                  pl.BlockSpec((2, PEER_KEYS, PEER_HALF), lambda i: (0, 0, 0))],
        out_specs=[pl.BlockSpec((tm, PEER_SEL), lambda i: (i, 0)),
                   pl.BlockSpec((tm, PEER_SEL), lambda i: (i, 0))],
        out_shape=[jax.ShapeDtypeStruct((t, PEER_SEL), I32),
                   jax.ShapeDtypeStruct((t, PEER_SEL), F32)],
        scratch_shapes=[pltpu.VMEM((nq, tm, PEER_HALF), F32),
                        pltpu.VMEM((PEER_SEL, tm), I32),
                        pltpu.VMEM((PEER_SEL, tm), F32)],
        compiler_params=_cparams(("parallel",), VMEM_LIMIT),
        name="peer_route",
    )(h, wq, keys)


def _peer_kernel(eid_cur_ref, eid_nxt_ref, x_ref, gate_ref, cmp_ref, exp_ref, uv_hbm, o_ref, buf, sem):
    step = pl.program_id(0)
    nsteps = pl.num_programs(0)
    g_tok = G_PEER
    rows = g_tok * PEER_SEL
    nsub = D_MODEL // LANES
    per_piece = rows // (2 * g_tok)

    def slot_wait(sl):
        pltpu.make_async_copy(uv_hbm.at[pl.ds(0, rows)], buf.at[sl], sem.at[sl]).wait()

    @pl.when(step == 0)
    def _():
        def body(r, carry):
            pltpu.make_async_copy(uv_hbm.at[eid_cur_ref[0, 0, r]], buf.at[0, r], sem.at[0]).start()
            return carry
        lax.fori_loop(0, rows, body, 0, unroll=8)

    diag = (lax.broadcasted_iota(I32, (nsub, PEER_SEL * nsub), 1) % nsub
            == lax.broadcasted_iota(I32, (nsub, PEER_SEL * nsub), 0))

    def group(sl, idx_ref, idx_off):
        tok0 = sl * g_tok

        def prefetch(r0, n):
            for i in range(n):
                r = r0 + i
                pltpu.make_async_copy(uv_hbm.at[idx_ref[0, 0, idx_off + r]], buf.at[1 - sl, r],
                                      sem.at[1 - sl]).start(priority=i % 2)

        def tiles(g):
            return buf[sl, g * PEER_SEL:(g + 1) * PEER_SEL].reshape(PEER_SEL * nsub, LANES)

        slot_wait(sl)
        parts = []
        for g in range(g_tok):
            u_rows = pltpu.bitcast(tiles(g) << 16, F32).astype(BF16)
            x_row = x_ref[tok0 + g:tok0 + g + 1, :]
            x_tile = jnp.concatenate([x_row[:, s * LANES:(s + 1) * LANES] for s in range(nsub)], axis=0)
            q = lax.dot_general(x_tile.astype(BF16), u_rows, (((1,), (1,)), ((), ())),
                                preferred_element_type=F32)
            parts.append(jnp.where(diag, q, 0.0))
            prefetch(g * per_piece, per_piece)
        masked = jnp.concatenate(parts, axis=0)
        m_hi = masked.astype(BF16)
        m_lo = (masked - m_hi.astype(F32)).astype(BF16)
        part = (jnp.dot(m_hi, cmp_ref[...], preferred_element_type=F32)
                + jnp.dot(m_lo, cmp_ref[...], preferred_element_type=F32))
        act = jnp.sum(part.reshape(g_tok, nsub, PEER_SEL), axis=1)
        wgt = gate_ref[tok0:tok0 + g_tok, :] * _gelu(act)
        wrep = jnp.dot(wgt.astype(BF16), exp_ref[...], preferred_element_type=F32)
        for g in range(g_tok):
            v_rows = pltpu.bitcast(tiles(g) & jnp.uint32(0xFFFF0000), F32).astype(BF16)
            wexp = jnp.where(diag, jnp.broadcast_to(wrep[g:g + 1, :], diag.shape), 0.0).astype(BF16)
            o_tile = jnp.dot(wexp, v_rows, preferred_element_type=F32)
            for s in range(nsub):
                o_ref[tok0 + g:tok0 + g + 1, s * LANES:(s + 1) * LANES] = o_tile[s:s + 1, :]
            prefetch((g_tok + g) * per_piece, per_piece)

    group(0, eid_cur_ref, rows)
    group(1, eid_nxt_ref, 0)

    @pl.when(step == nsteps - 1)
    def _():
        slot_wait(0)


def _peer(h, eid_tok, gate_tok, uv_packed):
    t, d = h.shape
    g_step = 2 * G_PEER
    nsteps = t // g_step
    rows = G_PEER * PEER_SEL
    nsub = d // LANES
    eid3 = eid_tok.reshape(nsteps, 1, 2 * rows)
    ncol = PEER_SEL * nsub
    col = jnp.arange(ncol) // nsub
    compress = (col[:, None] == jnp.arange(PEER_SEL)[None, :]).astype(F32)
    smem_blk = lambda f: pl.BlockSpec((1, 1, 2 * rows), f, memory_space=pltpu.SMEM)
    tile_blk = pl.BlockSpec((g_step, d), lambda i: (i, 0))
    return pl.pallas_call(
        _peer_kernel,
        grid=(nsteps,),
        in_specs=[smem_blk(lambda i: (i, 0, 0)),
                  smem_blk(lambda i: (jnp.minimum(i + 1, nsteps - 1), 0, 0)),
                  tile_blk,
                  pl.BlockSpec((g_step, PEER_SEL), lambda i: (i, 0)),
                  pl.BlockSpec((ncol, PEER_SEL), lambda i: (0, 0)),
                  pl.BlockSpec((PEER_SEL, ncol), lambda i: (0, 0)),
                  pl.BlockSpec(memory_space=pl.ANY)],
        out_specs=tile_blk,
        out_shape=jax.ShapeDtypeStruct((t, d), F32),
        scratch_shapes=[pltpu.VMEM((2, rows, nsub, LANES), U32), pltpu.SemaphoreType.DMA((2,))],
        compiler_params=_cparams(("arbitrary",), VMEM_LIMIT),
        name="peer_experts",
    )(eid3, eid3, h, gate_tok, compress.astype(BF16), compress.T.astype(BF16),
      uv_packed.reshape(-1, nsub, LANES))


def _pack_tables(u_tab, v_tab):
    ub = lax.bitcast_convert_type(u_tab.astype(BF16), jnp.uint16).astype(U32)
    vb = lax.bitcast_convert_type(v_tab.astype(BF16), jnp.uint16).astype(U32)
    return ub | (vb << 16)


def _out_kernel(h_ref, f_ref, p_ref, wp_ref, wgp_ref, g_ref, b_ref, o_ref):
    h = h_ref[...]
    ple = jnp.dot(p_ref[...].astype(BF16), wp_ref[...], preferred_element_type=F32)
    gate = _sigmoid(jnp.dot(h.astype(BF16), wgp_ref[...], preferred_element_type=F32))
    o_ref[...] = _ln(DN_ALPHA * h + f_ref[...] + ple * gate, g_ref[...], b_ref[...])


def _layer_out(h, ffn, p2, ple_w, ple_gate_w, ln_g, ln_b):
    t, d = h.shape
    tm = TM_OUT
    tok = lambda i: (i, 0)
    par = lambda i: (0, 0)
    return pl.pallas_call(
        _out_kernel,
        grid=(t // tm,),
        in_specs=[pl.BlockSpec((tm, d), tok), pl.BlockSpec((tm, d), tok),
                  pl.BlockSpec((tm, PLE_DIM), tok),
                  pl.BlockSpec((PLE_DIM, d), par), pl.BlockSpec((d, d), par),
                  pl.BlockSpec((1, d), par), pl.BlockSpec((1, d), par)],
        out_specs=pl.BlockSpec((tm, d), tok),
        out_shape=jax.ShapeDtypeStruct((t, d), F32),
        compiler_params=_cparams(("parallel",)),
        name="ple_ln2",
    )(h, ffn, p2, ple_w, ple_gate_w, ln_g.reshape(1, -1), ln_b.reshape(1, -1))


def _block_ones():
    head = jnp.arange(BRANCH) // HEAD_DIM
    return (head[:, None] == head[None, :]).astype(F32)


def _layer(h, p2, bsz, seq, layer, w_in_all, rwkv_mu, rwkv_w0, rwkv_w2, rwkv_a0, rwkv_a2, rwkv_g2, rwkv_kk,
           rwkv_ka, rwkv_rk, rwkv_lnx_g, rwkv_lnx_b, s5_lam_re, s5_lam_im, s5_b_re, s5_b_im, s5_c_re,
           s5_c_im, s5_d, s5_log_dt, s5_glu_w, s5_glu_b, fox_bf, conv_w, conv_b, conv_ln_g, conv_ln_b,
           w_branch, w_out, ln1_g, ln1_b, peer_wq, peer_k1, peer_k2, peer_u, peer_v, ple_w, ple_gate_w,
           ln2_g, ln2_b):
    t = bsz * seq
    d = D_MODEL
    w_packed, w_gate = _repack_w_in(w_in_all, layer)
    bd = _block_ones()
    wl = jnp.zeros((LANES, 3 * BRANCH), F32)
    wl = wl.at[0:32, 0:BRANCH].set(rwkv_w2.astype(F32))
    wl = wl.at[32:64, BRANCH:2 * BRANCH].set(rwkv_a2.astype(F32))
    wl = wl.at[64:128, 2 * BRANCH:].set(rwkv_g2.astype(F32))

    p_rwkv, p_s5_t, q_heads, k_heads, v_aug, p_ff, p_conv = _project(h, w_packed, bsz, seq)

    r, dec, k2, v, na, bb, g, bonus = _rwkv_pre(p_rwkv, rwkv_mu, rwkv_w0, rwkv_a0, rwkv_kk, rwkv_ka,
                                                 rwkv_rk.reshape(-1), wl, bd, bsz, seq)
    y_scan = _rwkv_scan(_to_scan_j(na, bsz, seq), _to_scan_j(dec, bsz, seq), _to_scan_j(bb, bsz, seq),
                        _to_scan_j(k2, bsz, seq), _to_scan_j(r, bsz, seq), _to_scan_i(v, bsz, seq))
    y_rwkv = _rwkv_post(_from_scan_i(y_scan, bsz, seq), g.reshape(t, BRANCH), bonus.reshape(t, BRANCH),
                        rwkv_lnx_g, rwkv_lnx_b, bd)

    bre, bim, lre, lim, cc = _s5_params(s5_lam_re, s5_lam_im, s5_b_re, s5_b_im, s5_c_re, s5_c_im, s5_log_dt)
    y_s5_t = _s5(p_s5_t, bre, bim, lre, lim, cc, s5_d, s5_glu_w, s5_glu_b, bsz, seq)

    f_rows = p_ff[:, :HEADS].reshape(bsz, seq, HEADS).transpose(0, 2, 1).reshape(bsz * HEADS, seq)
    bias_rows = jnp.tile(fox_bf.astype(F32), bsz).reshape(bsz * HEADS, 1)
    c = _fox_cumsum(f_rows, bias_rows)
    y_fox = _fox_attn(q_heads, k_heads, v_aug,
                      c.reshape(bsz, HEADS, seq, 1), c.reshape(bsz, HEADS, seq // TK_ATT, TK_ATT))

    y_conv = _conv_mixer(p_conv, conv_w, conv_b, conv_ln_g, conv_ln_b, bsz, seq)

    h1 = _merge(h, y_rwkv, y_s5_t, y_fox, y_conv, w_gate, w_branch.astype(BF16),
                w_out.astype(BF16), ln1_g, ln1_b, bsz, seq)

    keys = jnp.stack([peer_k1, peer_k2]).astype(F32)
    eid_tok, gate_tok = _route(h1, peer_wq.astype(BF16), keys)
    ffn = _peer(h1, eid_tok, gate_tok, _pack_tables(peer_u, peer_v))

    return _layer_out(h1, ffn, p2, ple_w.astype(BF16), ple_gate_w.astype(BF16), ln2_g, ln2_b)


def kernel(x, p, ln_in_g, ln_in_b, w_in, rwkv_mu, rwkv_w0, rwkv_w2, rwkv_a0, rwkv_a2, rwkv_g2, rwkv_kk, rwkv_ka, rwkv_rk, rwkv_lnx_g, rwkv_lnx_b, s5_lam_re, s5_lam_im, s5_b_re, s5_b_im, s5_c_re, s5_c_im, s5_d, s5_log_dt, s5_glu_w, s5_glu_b, fox_bf, conv_w, conv_b, conv_ln_g, conv_ln_b, w_branch, w_out, ln1_g, ln1_b, peer_wq, peer_k1, peer_k2, peer_u, peer_v, ple_w, ple_gate_w, ln2_g, ln2_b):
    bsz, seq, d = x.shape
    t = bsz * seq
    h = _layer_norm(x.reshape(t, d), ln_in_g, ln_in_b)
    per_layer = (rwkv_mu, rwkv_w0, rwkv_w2, rwkv_a0, rwkv_a2, rwkv_g2, rwkv_kk, rwkv_ka, rwkv_rk,
                 rwkv_lnx_g, rwkv_lnx_b, s5_lam_re, s5_lam_im, s5_b_re, s5_b_im, s5_c_re, s5_c_im, s5_d,
                 s5_log_dt, s5_glu_w, s5_glu_b, fox_bf, conv_w, conv_b, conv_ln_g, conv_ln_b, w_branch,
                 w_out, ln1_g, ln1_b, peer_wq, peer_k1, peer_k2, peer_u, peer_v, ple_w, ple_gate_w,
                 ln2_g, ln2_b)
    for i in range(p.shape[0]):
        h = _layer(h, p[i].reshape(t, PLE_DIM), bsz, seq, i, w_in, *(w[i] for w in per_layer))
    return h.reshape(bsz, seq, d)
```

```python
import functools
import math

import jax
import jax.numpy as jnp
from jax import lax
from jax.experimental import pallas as pl
from jax.experimental.pallas import tpu as pltpu

F32 = jnp.float32
BF16 = jnp.bfloat16
I32 = jnp.int32
U32 = jnp.uint32
HI = lax.Precision.HIGHEST

D_MODEL = 1024
BRANCH = 256
HEADS = 4
HEAD_DIM = 64
RWKV_COLS = 896
S5_GROUPS = 16
S5_GROUP = 16
S5_STATE = 64
S5_N = S5_GROUPS * S5_STATE
CONV_WIDTH = 31
CONV_HALO = 32
PEER_HEADS = 8
PEER_KEYS = 128
PEER_HALF = 128
PEER_TOPK = 16
PEER_SEL = PEER_HEADS * PEER_TOPK
PLE_DIM = 256
RWKV_GN_EPS = 64e-5
LN_EPS = 1e-5
DEPTH = 2
DN_ALPHA = (2 * DEPTH) ** 0.25

SUBLANES = 8
LANES = 128
VMEM_LIMIT = 56 * 1024 * 1024

TM_PROJ = 512
TQ_PRE = 256
TC_SCAN = 32
TC_S5 = 32
TQ_ATT = 256
TK_ATT = 256
TC_CONV = 512
TM_MERGE = 256
TM_ROUTE = 128
G_PEER = 16
TM_OUT = 256


def _cparams(sem, vmem=None):
    return pltpu.CompilerParams(dimension_semantics=sem, vmem_limit_bytes=vmem)


def _ln(z, g, b):
    mu = jnp.mean(z, axis=-1, keepdims=True)
    zc = z - mu
    var = jnp.mean(zc * zc, axis=-1, keepdims=True)
    return zc * lax.rsqrt(var + LN_EPS) * g + b


def _gelu(y):
    return 0.5 * y * (1.0 + lax.erf(y * (1.0 / math.sqrt(2.0))))


def _sigmoid(y):
    return 1.0 / (1.0 + jnp.exp(-y))


def _softplus(y):
    return jnp.maximum(y, 0.0) + jnp.log(1.0 + jnp.exp(-jnp.abs(y)))


def _ln_kernel(x_ref, g_ref, b_ref, o_ref):
    o_ref[...] = _ln(x_ref[...], g_ref[...], b_ref[...])


def _layer_norm(x2, g, b):
    t, d = x2.shape
    tm = TM_PROJ
    return pl.pallas_call(
        _ln_kernel,
        grid=(t // tm,),
        in_specs=[pl.BlockSpec((tm, d), lambda i: (i, 0)),
                  pl.BlockSpec((1, d), lambda i: (0, 0)),
                  pl.BlockSpec((1, d), lambda i: (0, 0))],
        out_specs=pl.BlockSpec((tm, d), lambda i: (i, 0)),
        out_shape=jax.ShapeDtypeStruct((t, d), F32),
        compiler_params=_cparams(("parallel",)),
        name="ln_in",
    )(x2, g.reshape(1, d), b.reshape(1, d))


_P_RWKV = (0, 896)
_P_S5 = (896, 1152)
_P_FQKV = (1152, 1920)
_P_FF = (1920, 2048)
_P_CONV = (2048, 2560)
_P_COLS = 2560


_N_FRONT = RWKV_COLS + BRANCH + 3 * BRANCH + HEADS
_TR_REPACK = 128


def _repack_kernel(w_ref, wp_ref, wg_ref):
    w = w_ref[0]
    lo, hi = _P_FF
    wp_ref[:, 0:lo] = w[:, 0:lo].astype(BF16)
    lane = lax.broadcasted_iota(I32, (w.shape[0], hi - lo), 1)
    wp_ref[:, lo:hi] = jnp.where(lane < HEADS, w[:, lo:hi], 0.0).astype(BF16)
    wp_ref[:, hi:_P_COLS] = w[:, _N_FRONT:_N_FRONT + 2 * BRANCH].astype(BF16)
    wg_ref[...] = w[:, _N_FRONT + 2 * BRANCH:].astype(BF16)


def _repack_w_in(w_in_all, layer):
    _, d, n = w_in_all.shape
    n_gate = n - _N_FRONT - 2 * BRANCH
    tr = _TR_REPACK
    return pl.pallas_call(
        _repack_kernel,
        grid=(d // tr,),
        in_specs=[pl.BlockSpec((1, tr, n), lambda i: (layer, i, 0))],
        out_specs=[pl.BlockSpec((tr, _P_COLS), lambda i: (i, 0)),
                   pl.BlockSpec((tr, n_gate), lambda i: (i, 0))],
        out_shape=[jax.ShapeDtypeStruct((d, _P_COLS), BF16),
                   jax.ShapeDtypeStruct((d, n_gate), BF16)],
        compiler_params=_cparams(("parallel",)),
        name="repack_w_in",
    )(w_in_all)


def _proj_kernel(h_ref, w_ref, rw_ref, s5_ref, q_ref, k_ref, v_ref, ff_ref, cv_ref):
    x = h_ref[...].astype(BF16)
    for (lo, hi), o_ref in ((_P_RWKV, rw_ref), (_P_S5, s5_ref), (_P_FF, ff_ref), (_P_CONV, cv_ref)):
        o_ref[...] = jnp.dot(x, w_ref[:, lo:hi], preferred_element_type=F32)
    qkv = jnp.dot(x, w_ref[:, _P_FQKV[0]:_P_FQKV[1]], preferred_element_type=F32)
    tm = qkv.shape[0]
    ones_col = (lax.broadcasted_iota(I32, (tm, LANES - HEAD_DIM), 1) == 0).astype(F32)
    for hh in range(HEADS):
        q_ref[0, hh] = qkv[:, hh * HEAD_DIM:(hh + 1) * HEAD_DIM]
        k_ref[0, hh] = qkv[:, BRANCH + hh * HEAD_DIM:BRANCH + (hh + 1) * HEAD_DIM]
        v_ref[0, hh] = jnp.concatenate(
            [qkv[:, 2 * BRANCH + hh * HEAD_DIM:2 * BRANCH + (hh + 1) * HEAD_DIM], ones_col], axis=1)


def _project(h, w_packed, bsz, seq):
    t, d = h.shape
    tm = TM_PROJ
    nst = seq // tm
    w_rwkv, w_s5, w_ff, w_conv = [hi - lo for lo, hi in (_P_RWKV, _P_S5, _P_FF, _P_CONV)]
    tok = lambda i: (i, 0)
    heads = lambda i: (i // nst, 0, i % nst, 0)
    out_shapes = [jax.ShapeDtypeStruct((t, w_rwkv), F32),
                  jax.ShapeDtypeStruct((seq, bsz * w_s5), F32),
                  jax.ShapeDtypeStruct((bsz, HEADS, seq, HEAD_DIM), F32),
                  jax.ShapeDtypeStruct((bsz, HEADS, seq, HEAD_DIM), F32),
                  jax.ShapeDtypeStruct((bsz, HEADS, seq, LANES), F32),
                  jax.ShapeDtypeStruct((t, w_ff), F32),
                  jax.ShapeDtypeStruct((t, w_conv), F32)]
    out_specs = [pl.BlockSpec((tm, w_rwkv), tok),
                 pl.BlockSpec((tm, w_s5), lambda i: (i % nst, i // nst)),
                 pl.BlockSpec((1, HEADS, tm, HEAD_DIM), heads),
                 pl.BlockSpec((1, HEADS, tm, HEAD_DIM), heads),
                 pl.BlockSpec((1, HEADS, tm, LANES), heads),
                 pl.BlockSpec((tm, w_ff), tok),
                 pl.BlockSpec((tm, w_conv), tok)]
    return pl.pallas_call(
        _proj_kernel,
        grid=(t // tm,),
        in_specs=[pl.BlockSpec((tm, d), tok),
                  pl.BlockSpec((d, _P_COLS), lambda i: (0, 0))],
        out_specs=out_specs,
        out_shape=out_shapes,
        compiler_params=_cparams(("parallel",), VMEM_LIMIT),
        name="in_proj",
    )(h, w_packed)


def _rwkv_pre_kernel(z_ref, zp_ref, mu_ref, w0_ref, a0_ref, kk_ref, ka_ref, rk_ref, wl_ref, bd_ref,
                     r_out, w_out, k_out, v_out, na_out, b_out, g_out, bv_out):
    j = pl.program_id(1)
    z = z_ref[0]
    prev = zp_ref[0][SUBLANES - 1:SUBLANES, :]
    prev = jnp.where(j == 0, 0.0, prev)
    row = lax.broadcasted_iota(I32, z.shape, 0)
    zs = jnp.where(row == 0, prev, pltpu.roll(z, 1, 0))
    z = z + (zs - z) * mu_ref[...]
    r = z[:, 0:256]
    k = z[:, 256:512]
    v = z[:, 512:768]
    zc = z[:, 768:896]
    lane = lax.broadcasted_iota(I32, zc.shape, 1)
    act = jnp.where(lane < 32, jnp.tanh(zc), jnp.where(lane < 64, zc, _sigmoid(zc)))
    lo = jnp.dot(act, wl_ref[...], precision=HI, preferred_element_type=F32)
    dw = lo[:, 0:256]
    da = lo[:, 256:512]
    g = lo[:, 512:768]
    w_log = -_softplus(-(w0_ref[...] + dw)) - 0.5
    decay = jnp.exp(-jnp.exp(w_log))
    a = _sigmoid(a0_ref[...] + da)
    kkv = k * kk_ref[...]
    bd = bd_ref[...]
    ss = jnp.dot(kkv * kkv, bd, precision=HI, preferred_element_type=F32)
    kkn = kkv * lax.rsqrt(jnp.maximum(ss, 1e-24))
    k2 = k * (1.0 + (a - 1.0) * ka_ref[...])
    bonus = jnp.dot(r * k2 * rk_ref[...], bd, precision=HI, preferred_element_type=F32) * v
    r_out[0] = r
    w_out[0] = decay
    k_out[0] = k2
    v_out[0] = v
    na_out[0] = -kkn
    b_out[0] = kkn * a
    g_out[0] = g
    bv_out[0] = bonus


def _rwkv_pre(p_rwkv, mu, w0, a0, kk, ka, rk, wl, bd, bsz, seq):
    z3 = p_rwkv.reshape(bsz, seq, RWKV_COLS)
    tq = TQ_PRE
    blk = lambda b, j: (b, j, 0)
    par = lambda b, j: (0, 0)
    out_shape = [jax.ShapeDtypeStruct((bsz, seq, BRANCH), F32)] * 8
    out_specs = [pl.BlockSpec((1, tq, BRANCH), blk)] * 8
    return pl.pallas_call(
        _rwkv_pre_kernel,
        grid=(bsz, seq // tq),
        in_specs=[pl.BlockSpec((1, tq, RWKV_COLS), blk),
                  pl.BlockSpec((1, SUBLANES, RWKV_COLS),
                               lambda b, j: (b, jnp.maximum(j * (tq // SUBLANES) - 1, 0), 0)),
                  pl.BlockSpec((1, RWKV_COLS), par),
                  pl.BlockSpec((1, BRANCH), par), pl.BlockSpec((1, BRANCH), par),
                  pl.BlockSpec((1, BRANCH), par), pl.BlockSpec((1, BRANCH), par),
                  pl.BlockSpec((1, BRANCH), par),
                  pl.BlockSpec((LANES, 3 * BRANCH), par),
                  pl.BlockSpec((BRANCH, BRANCH), par)],
        out_specs=out_specs,
        out_shape=out_shape,
        compiler_params=_cparams(("parallel", "parallel")),
        name="rwkv_pre",
    )(z3, z3, mu.reshape(1, -1), w0.reshape(1, -1), a0.reshape(1, -1), kk.reshape(1, -1),
      ka.reshape(1, -1), rk.reshape(1, -1), wl, bd)


_SCAN_IG = HEAD_DIM // 2 // SUBLANES
_SCAN_HALF = HEAD_DIM // 2


def _rwkv_scan_kernel(a_ref, w_ref, b_ref, k_ref, r_ref, v_ref, y_ref, s_ref, a_scr, w_scr, b_scr, k_scr, r_scr):
    @pl.when(pl.program_id(0) == 0)
    def _():
        s_ref[...] = jnp.zeros_like(s_ref)

    tc = a_ref.shape[0]
    tile = (SUBLANES, LANES)

    lane = lax.broadcasted_iota(I32, (tc * _SCAN_HALF, LANES), 1)
    for src_ref, dst_ref in ((a_ref, a_scr), (w_ref, w_scr), (b_ref, b_scr), (k_ref, k_scr), (r_ref, r_scr)):
        x = src_ref[...].reshape(tc * _SCAN_HALF, LANES)
        xr = pltpu.roll(x, LANES // 2, 1)
        dst_ref[:, 0:_SCAN_HALF, :] = jnp.where(lane < LANES // 2, x, xr).reshape(tc, _SCAN_HALF, LANES)
        dst_ref[:, _SCAN_HALF:, :] = jnp.where(lane < LANES // 2, xr, x).reshape(tc, _SCAN_HALF, LANES)

    def step(t, carry):
        vv = [v_ref[t, ig * SUBLANES:(ig + 1) * SUBLANES, :] for ig in range(_SCAN_IG)]
        sa = [[jnp.zeros(tile, F32), jnp.zeros(tile, F32)] for _ in range(_SCAN_IG)]
        for j in range(HEAD_DIM):
            ab = jnp.broadcast_to(a_scr[t, j:j + 1, :], tile)
            for ig in range(_SCAN_IG):
                sa[ig][j % 2] = sa[ig][j % 2] + s_ref[ig, j] * ab
        sa = [x[0] + x[1] for x in sa]
        yy = [[jnp.zeros(tile, F32), jnp.zeros(tile, F32)] for _ in range(_SCAN_IG)]
        for j in range(HEAD_DIM):
            wb = jnp.broadcast_to(w_scr[t, j:j + 1, :], tile)
            bb = jnp.broadcast_to(b_scr[t, j:j + 1, :], tile)
            kb = jnp.broadcast_to(k_scr[t, j:j + 1, :], tile)
            rb = jnp.broadcast_to(r_scr[t, j:j + 1, :], tile)
            for ig in range(_SCAN_IG):
                s = s_ref[ig, j] * wb + sa[ig] * bb + vv[ig] * kb
                s_ref[ig, j] = s
                yy[ig][j % 2] = yy[ig][j % 2] + s * rb
        for ig in range(_SCAN_IG):
            y_ref[t, ig * SUBLANES:(ig + 1) * SUBLANES, :] = yy[ig][0] + yy[ig][1]
        return carry

    lax.fori_loop(0, tc, step, 0)


def _rwkv_scan(a_t, w_t, b_t, k_t, r_t, v_t):
    seq = a_t.shape[0]
    tc = TC_SCAN
    spec = pl.BlockSpec((tc, _SCAN_HALF, LANES), lambda i: (i, 0, 0))
    return pl.pallas_call(
        _rwkv_scan_kernel,
        grid=(seq // tc,),
        in_specs=[spec] * 6,
        out_specs=spec,
        out_shape=jax.ShapeDtypeStruct((seq, _SCAN_HALF, LANES), F32),
        scratch_shapes=[pltpu.VMEM((_SCAN_IG, HEAD_DIM, SUBLANES, LANES), F32)]
                       + [pltpu.VMEM((tc, HEAD_DIM, LANES), F32)] * 5,
        compiler_params=_cparams(("arbitrary",)),
        name="rwkv_scan",
    )(a_t, w_t, b_t, k_t, r_t, v_t)


def _to_scan(x, bsz, seq):
    n_pairs = bsz * HEADS
    xt = x.reshape(bsz, seq, HEADS, 2, HEAD_DIM // 2).transpose(1, 4, 3, 0, 2)
    xt = xt.reshape(seq, HEAD_DIM // 2, 2, n_pairs)
    pad = LANES // 2 - n_pairs
    if pad:
        xt = jnp.pad(xt, ((0, 0), (0, 0), (0, 0), (0, pad)))
    return xt.reshape(seq, HEAD_DIM // 2, LANES)


def _from_scan(y, bsz, seq):
    n_pairs = bsz * HEADS
    yt = y.reshape(seq, HEAD_DIM // 2, 2, LANES // 2)[..., :n_pairs]
    yt = yt.reshape(seq, HEAD_DIM // 2, 2, bsz, HEADS).transpose(3, 0, 4, 2, 1)
    return yt.reshape(bsz * seq, BRANCH)


def _rwkv_post_kernel(y_ref, g_ref, bv_ref, gg_ref, gb_ref, bd_ref, o_ref):
    y = y_ref[...]
    bd = bd_ref[...]
    inv = 1.0 / HEAD_DIM
    mean = jnp.dot(y, bd, precision=HI, preferred_element_type=F32) * inv
    yc = y - mean
    var = jnp.dot(yc * yc, bd, precision=HI, preferred_element_type=F32) * inv
    yn = yc * lax.rsqrt(var + RWKV_GN_EPS) * gg_ref[...] + gb_ref[...]
    o_ref[...] = (yn + bv_ref[...]) * g_ref[...]


def _rwkv_post(y, g, bv, gn_g, gn_b, bd):
    t = y.shape[0]
    tm = TM_PROJ
    tok = lambda i: (i, 0)
    par = lambda i: (0, 0)
    return pl.pallas_call(
        _rwkv_post_kernel,
        grid=(t // tm,),
        in_specs=[pl.BlockSpec((tm, BRANCH), tok)] * 3
                 + [pl.BlockSpec((1, BRANCH), par)] * 2 + [pl.BlockSpec((BRANCH, BRANCH), par)],
        out_specs=pl.BlockSpec((tm, BRANCH), tok),
        out_shape=jax.ShapeDtypeStruct((t, BRANCH), F32),
        compiler_params=_cparams(("parallel",)),
        name="rwkv_post",
    )(y, g, bv, gn_g.reshape(1, -1), gn_b.reshape(1, -1), bd)


def _s5_kernel(u_ref, bre_ref, bim_ref, lre_ref, lim_ref, cc_ref, d_ref, gw_ref, gb_ref, o_ref,
               xre_ref, xim_ref, bure_ref, buim_ref, xs_ref):
    @pl.when(pl.program_id(0) == 0)
    def _():
        xre_ref[...] = jnp.zeros_like(xre_ref)
        xim_ref[...] = jnp.zeros_like(xim_ref)

    tc, nb, w = u_ref.shape
    u2 = u_ref[...].reshape(tc * nb, w)
    ub = u2.astype(BF16)
    bure_ref[...] = jnp.dot(ub, bre_ref[...], preferred_element_type=F32)
    buim_ref[...] = jnp.dot(ub, bim_ref[...], preferred_element_type=F32)
    lre = jnp.broadcast_to(lre_ref[...], (nb, S5_N))
    lim = jnp.broadcast_to(lim_ref[...], (nb, S5_N))

    def step(t, carry):
        xr, xi = carry
        r0 = pl.multiple_of(t * nb, nb)
        nr = lre * xr - lim * xi + bure_ref[pl.ds(r0, nb), :]
        ni = lre * xi + lim * xr + buim_ref[pl.ds(r0, nb), :]
        xs_ref[pl.ds(r0, nb), 0:S5_N] = nr
        xs_ref[pl.ds(r0, nb), S5_N:2 * S5_N] = ni
        return nr, ni

    xr, xi = lax.fori_loop(0, tc, step, (xre_ref[...], xim_ref[...]))
    xre_ref[...] = xr
    xim_ref[...] = xi
    y = jnp.dot(xs_ref[...].astype(BF16), cc_ref[...], preferred_element_type=F32) + d_ref[...] * u2
    y = _gelu(y)
    gate = jnp.dot(y.astype(BF16), gw_ref[...], preferred_element_type=F32) + gb_ref[...]
    o_ref[...] = (y * _sigmoid(gate)).reshape(tc, nb, w)


def _s5(u_t, bre, bim, lre, lim, cc, d_skip, glu_w, glu_b, bsz, seq):
    u3 = u_t.reshape(seq, bsz, BRANCH)
    tc = TC_S5
    par = lambda i: (0, 0)
    blk = pl.BlockSpec((tc, bsz, BRANCH), lambda i: (i, 0, 0))
    out = pl.pallas_call(
        _s5_kernel,
        grid=(seq // tc,),
        in_specs=[blk,
                  pl.BlockSpec((BRANCH, S5_N), par), pl.BlockSpec((BRANCH, S5_N), par),
                  pl.BlockSpec((1, S5_N), par), pl.BlockSpec((1, S5_N), par),
                  pl.BlockSpec((2 * S5_N, BRANCH), par),
                  pl.BlockSpec((1, BRANCH), par),
                  pl.BlockSpec((BRANCH, BRANCH), par),
                  pl.BlockSpec((1, BRANCH), par)],
        out_specs=blk,
        out_shape=jax.ShapeDtypeStruct((seq, bsz, BRANCH), F32),
        scratch_shapes=[pltpu.VMEM((bsz, S5_N), F32), pltpu.VMEM((bsz, S5_N), F32),
                        pltpu.VMEM((tc * bsz, S5_N), F32), pltpu.VMEM((tc * bsz, S5_N), F32),
                        pltpu.VMEM((tc * bsz, 2 * S5_N), F32)],
        compiler_params=_cparams(("arbitrary",), VMEM_LIMIT),
        name="s5_scan",
    )(u3, bre.astype(BF16), bim.astype(BF16), lre, lim, cc.astype(BF16), d_skip.reshape(1, -1),
      glu_w.astype(BF16), glu_b.reshape(1, -1))
    return out.reshape(seq, bsz * BRANCH)


def _s5_params(lam_re, lam_im, b_re, b_im, c_re, c_im, log_dt):
    lam = lax.complex(lam_re.astype(F32), lam_im.astype(F32))
    dt = jnp.exp(log_dt.astype(F32))[:, None]
    lam_bar = jnp.exp(lam * dt)
    b_bar = ((lam_bar - 1.0) / lam)[..., None] * lax.complex(b_re.astype(F32), b_im.astype(F32))
    eye = jnp.eye(S5_GROUPS, dtype=F32)
    bre = jnp.einsum('gpc,gh->gchp', jnp.real(b_bar), eye).reshape(BRANCH, S5_N)
    bim = jnp.einsum('gpc,gh->gchp', jnp.imag(b_bar), eye).reshape(BRANCH, S5_N)
    cre = jnp.einsum('gcp,gh->gphc', c_re.astype(F32), eye).reshape(S5_N, BRANCH)
    cim = jnp.einsum('gcp,gh->gphc', c_im.astype(F32), eye).reshape(S5_N, BRANCH)
    cc = jnp.concatenate([cre, -cim], axis=0)
    return bre, bim, jnp.real(lam_bar).reshape(1, S5_N), jnp.imag(lam_bar).reshape(1, S5_N), cc


def _fox_cumsum_kernel(f_ref, bias_ref, c_ref):
    x = f_ref[...] + bias_ref[...]
    c = -_softplus(-x)
    n = c.shape[1]
    lane = lax.broadcasted_iota(I32, c.shape, 1)
    sh = 1
    while sh < n:
        c = c + jnp.where(lane >= sh, pltpu.roll(c, sh, 1), 0.0)
        sh *= 2
    c_ref[...] = c


def _fox_cumsum(f_rows, bias_rows):
    rows, seq = f_rows.shape
    return pl.pallas_call(
        _fox_cumsum_kernel,
        out_shape=jax.ShapeDtypeStruct((rows, seq), F32),
        name="fox_cumsum",
    )(f_rows, bias_rows)


def _fox_attn_kernel(q_ref, k_ref, v_ref, cq_ref, ck_ref, o_ref):
    qi = pl.program_id(1)
    nh, tq = q_ref.shape[1], q_ref.shape[2]
    tk = ck_ref.shape[3]
    qs = [(q_ref[0, h] * (HEAD_DIM ** -0.5)).astype(BF16) for h in range(nh)]
    cqs = [cq_ref[0, h] for h in range(nh)]
    qpos = qi * tq + lax.broadcasted_iota(I32, (tq, tk), 0)
    nkb = (qi * tq + tq + tk - 1) // tk

    def body(kb, carry):
        k0 = pl.multiple_of(kb * tk, tk)
        causal = k0 + lax.broadcasted_iota(I32, (tq, tk), 1) <= qpos
        out = []
        for h in range(nh):
            m, acc = carry[h]
            kblk = k_ref[0, h, pl.ds(k0, tk), :].astype(BF16)
            vblk = v_ref[0, h, pl.ds(k0, tk), :].astype(BF16)
            s = lax.dot_general(qs[h], kblk, (((1,), (1,)), ((), ())), preferred_element_type=F32)
            s = s + cqs[h] - ck_ref[0, h, pl.ds(kb, 1), :]
            s = jnp.where(causal, s, -1e30)
            m_new = jnp.maximum(m, jnp.max(s, axis=-1, keepdims=True))
            p = jnp.exp(s - m_new)
            acc = jnp.exp(m - m_new) * acc + jnp.dot(p.astype(BF16), vblk, preferred_element_type=F32)
            out.append((m_new, acc))
        return tuple(out)

    init = tuple((jnp.full((tq, 1), -1e30, F32), jnp.zeros((tq, v_ref.shape[3]), F32)) for _ in range(nh))
    res = lax.fori_loop(0, nkb, body, init)
    for h in range(nh):
        acc = res[h][1]
        o_ref[0, h] = acc[:, 0:HEAD_DIM] / acc[:, HEAD_DIM:HEAD_DIM + 1]


def _fox_attn(q, k, v, c_col, c_rowb):
    bsz, nh, seq, hd = q.shape
    tq = TQ_ATT
    nkb, tk = c_rowb.shape[2], c_rowb.shape[3]
    full = lambda b, i: (b, 0, 0, 0)
    tile = lambda b, i: (b, 0, i, 0)
    return pl.pallas_call(
        _fox_attn_kernel,
        grid=(bsz, seq // tq),
        in_specs=[pl.BlockSpec((1, nh, tq, hd), tile),
                  pl.BlockSpec((1, nh, seq, hd), full),
                  pl.BlockSpec((1, nh, seq, v.shape[3]), full),
                  pl.BlockSpec((1, nh, tq, 1), tile),
                  pl.BlockSpec((1, nh, nkb, tk), full)],
        out_specs=pl.BlockSpec((1, nh, tq, hd), tile),
        out_shape=jax.ShapeDtypeStruct((bsz, nh, seq, hd), F32),
        compiler_params=_cparams(("parallel", "parallel"), VMEM_LIMIT),
        name="fox_attn",
    )(q, k, v, c_col, c_rowb)


def _conv_kernel(x_ref, xp_ref, w_ref, b_ref, g_ref, be_ref, o_ref, buf_ref):
    j = pl.program_id(1)
    tc = x_ref.shape[1]
    x = x_ref[0]
    buf_ref[CONV_HALO:CONV_HALO + tc, :] = x[:, 0:BRANCH] * _sigmoid(x[:, BRANCH:2 * BRANCH])
    xp = xp_ref[0]
    hp = xp[:, 0:BRANCH] * _sigmoid(xp[:, BRANCH:2 * BRANCH])
    buf_ref[0:CONV_HALO, :] = jnp.where(j == 0, 0.0, hp)
    acc = jnp.zeros((tc, BRANCH), F32) + b_ref[...]
    off = CONV_HALO - (CONV_WIDTH - 1)
    for kk in range(CONV_WIDTH):
        acc = acc + buf_ref[off + kk:off + kk + tc, :] * w_ref[kk:kk + 1, :]
    y = _ln(acc, g_ref[...], be_ref[...])
    o_ref[0] = y * _sigmoid(y)


def _conv_mixer(p_conv, conv_w, conv_b, ln_g, ln_b, bsz, seq):
    x3 = p_conv.reshape(bsz, seq, 2 * BRANCH)
    tc = TC_CONV
    par = lambda b, j: (0, 0)
    out = pl.pallas_call(
        _conv_kernel,
        grid=(bsz, seq // tc),
        in_specs=[pl.BlockSpec((1, tc, 2 * BRANCH), lambda b, j: (b, j, 0)),
                  pl.BlockSpec((1, CONV_HALO, 2 * BRANCH),
                               lambda b, j: (b, jnp.maximum(j * (tc // CONV_HALO) - 1, 0), 0)),
                  pl.BlockSpec((CONV_WIDTH, BRANCH), par),
                  pl.BlockSpec((1, BRANCH), par), pl.BlockSpec((1, BRANCH), par),
                  pl.BlockSpec((1, BRANCH), par)],
        out_specs=pl.BlockSpec((1, tc, BRANCH), lambda b, j: (b, j, 0)),
        out_shape=jax.ShapeDtypeStruct((bsz, seq, BRANCH), F32),
        scratch_shapes=[pltpu.VMEM((CONV_HALO + tc, BRANCH), F32)],
        compiler_params=_cparams(("parallel", "parallel")),
        name="conv_mixer",
    )(x3, x3, conv_w, conv_b.reshape(1, -1), ln_g.reshape(1, -1), ln_b.reshape(1, -1))
    return out.reshape(bsz * seq, BRANCH)


def _merge_kernel(h_ref, y0_ref, y1_ref, y2_ref, y3_ref, wg_ref, wb_ref, wo_ref, g_ref, b_ref, o_ref):
    h = h_ref[...]
    hb = h.astype(BF16)
    merged = None
    for br, y_ref in enumerate((y0_ref, y1_ref, y2_ref, y3_ref)):
        gate = _sigmoid(jnp.dot(hb, wg_ref[:, br * D_MODEL:(br + 1) * D_MODEL], preferred_element_type=F32))
        if br == 2:
            proj = None
            for hh in range(HEADS):
                part = jnp.dot(y_ref[0, hh].astype(BF16), wb_ref[br, hh * HEAD_DIM:(hh + 1) * HEAD_DIM, :],
                               preferred_element_type=F32)
                proj = part if proj is None else proj + part
        else:
            proj = jnp.dot(y_ref[...].astype(BF16), wb_ref[br], preferred_element_type=F32)
        term = gate * proj
        merged = term if merged is None else merged + term
    mix = jnp.dot(merged.astype(BF16), wo_ref[...], preferred_element_type=F32)
    o_ref[...] = _ln(DN_ALPHA * h + mix, g_ref[...], b_ref[...])


def _merge(h, y_rwkv, y_s5_t, y_fox, y_conv, wg, wb, wo, ln_g, ln_b, bsz, seq):
    t, d = h.shape
    tm = TM_MERGE
    nst = seq // tm
    tok = lambda i: (i, 0)
    par2 = lambda i: (0, 0)
    br_spec = pl.BlockSpec((tm, BRANCH), tok)
    return pl.pallas_call(
        _merge_kernel,
        grid=(t // tm,),
        in_specs=[pl.BlockSpec((tm, d), tok),
                  br_spec,
                  pl.BlockSpec((tm, BRANCH), lambda i: (i % nst, i // nst)),
                  pl.BlockSpec((1, HEADS, tm, HEAD_DIM), lambda i: (i // nst, 0, i % nst, 0)),
                  br_spec,
                  pl.BlockSpec((d, 4 * d), par2),
                  pl.BlockSpec((4, BRANCH, d), lambda i: (0, 0, 0)),
                  pl.BlockSpec((d, d), par2),
                  pl.BlockSpec((1, d), par2), pl.BlockSpec((1, d), par2)],
        out_specs=pl.BlockSpec((tm, d), tok),
        out_shape=jax.ShapeDtypeStruct((t, d), F32),
        compiler_params=_cparams(("parallel",), VMEM_LIMIT),
        name="merge_ln1",
    )(h, y_rwkv, y_s5_t, y_fox, y_conv, wg, wb, wo, ln_g.reshape(1, -1), ln_b.reshape(1, -1))


def _sort_network(n):
    pairs = []
    p = 1
    while p < n:
        k = p
        while k >= 1:
            for j in range(k % p, n - k, 2 * k):
                for i in range(min(k, n - j - k)):
                    if (i + j) // (2 * p) == (i + j + k) // (2 * p):
                        pairs.append((i + j, i + j + k))
            k //= 2
        p *= 2
    return pairs


_KEY_SORT_PAIRS = _sort_network(PEER_KEYS // SUBLANES)


def _top_keys(st):
    nv = PEER_KEYS // SUBLANES
    tm = st.shape[1]
    sub = lax.broadcasted_iota(I32, (SUBLANES, tm), 0)
    vals = [st[k * SUBLANES:(k + 1) * SUBLANES, :] for k in range(nv)]
    keys = [k * SUBLANES + sub for k in range(nv)]
    for i, j in _KEY_SORT_PAIRS:
        first = (vals[i] > vals[j]) | ((vals[i] == vals[j]) & (keys[i] < keys[j]))
        vals[i], vals[j] = jnp.where(first, vals[i], vals[j]), jnp.where(first, vals[j], vals[i])
        keys[i], keys[j] = jnp.where(first, keys[i], keys[j]), jnp.where(first, keys[j], keys[i])
    out_v, out_k = [], []
    for t in range(PEER_TOPK):
        m = jnp.max(vals[0], axis=0, keepdims=True)
        kmin = jnp.min(jnp.where(vals[0] == m, keys[0], jnp.int32(2 ** 30)), axis=0, keepdims=True)
        out_v.append(m)
        out_k.append(kmin)
        win = keys[0] == kmin
        for d in range(PEER_TOPK - 1 - t):
            vals[d] = jnp.where(win, vals[d + 1], vals[d])
            keys[d] = jnp.where(win, keys[d + 1], keys[d])
    return jnp.concatenate(out_v, axis=0), jnp.concatenate(out_k, axis=0)


def _top_candidates(v1, i1, v2, i2):
    tm = v1.shape[1]
    sub = lax.broadcasted_iota(I32, (SUBLANES, tm), 0)
    bc = lambda x, a: jnp.broadcast_to(x[a:a + 1, :], (SUBLANES, tm))
    vals = [jnp.where((a + 1) * (sub + 1) <= PEER_TOPK, bc(v1, a) + v2[0:SUBLANES, :], -jnp.inf)
            for a in range(PEER_TOPK)]
    eids = [bc(i1, a) * PEER_KEYS + i2[0:SUBLANES, :] for a in range(PEER_TOPK)]
    taken = jnp.zeros((SUBLANES, tm), I32)
    vals_hi = bc(v1, 0) + v2[SUBLANES:, :]
    eids_hi = bc(i1, 0) * PEER_KEYS + i2[SUBLANES:, :]
    keys_hi = SUBLANES + sub
    big = jnp.int32(2 ** 30)
    out_v, out_e = [], []
    for t in range(PEER_TOPK):
        keys = taken * PEER_TOPK + sub
        m = jnp.max(jnp.maximum(vals[0], vals_hi), axis=0, keepdims=True)
        kmin = jnp.min(jnp.minimum(jnp.where(vals[0] == m, keys, big), jnp.where(vals_hi == m, keys_hi, big)),
                       axis=0, keepdims=True)
        win = keys == kmin
        win_hi = keys_hi == kmin
        out_v.append(m)
        out_e.append(jnp.max(jnp.maximum(jnp.where(win, eids[0], -1), jnp.where(win_hi, eids_hi, -1)),
                             axis=0, keepdims=True))
        for d in range(PEER_TOPK - 1 - t):
            vals[d] = jnp.where(win, vals[d + 1], vals[d])
            eids[d] = jnp.where(win, eids[d + 1], eids[d])
        taken = jnp.where(win, taken + 1, taken)
        vals_hi = jnp.where(win_hi, -jnp.inf, vals_hi)
    return jnp.concatenate(out_v, axis=0), jnp.concatenate(out_e, axis=0)


_ROUTE_HEADS_PER_ITER = 4


def _route_kernel(h_ref, wq_ref, keys_ref, eid_ref, gate_ref, q_scr, eid_scr, gate_scr):
    tm = h_ref.shape[0]
    q = jnp.dot(h_ref[...].astype(BF16), wq_ref[...], preferred_element_type=F32)
    for c in range(2 * PEER_HEADS):
        q_scr[c] = q[:, c * PEER_HALF:(c + 1) * PEER_HALF]

    def body(it, carry):
        for off in range(_ROUTE_HEADS_PER_ITER):
            hh = it * _ROUTE_HEADS_PER_ITER + off
            tops = []
            for half in range(2):
                st = lax.dot_general(keys_ref[half], q_scr[2 * hh + half], (((1,), (1,)), ((), ())),
                                     precision=HI, preferred_element_type=F32)
                tops.append(_top_keys(st))
            (v1, i1), (v2, i2) = tops
            score, eids = _top_candidates(v1, i1, v2, i2)
            e = jnp.exp(score - jnp.max(score, axis=0, keepdims=True))
            r0 = pl.multiple_of(hh * PEER_TOPK, PEER_TOPK)
            gate_scr[pl.ds(r0, PEER_TOPK), :] = e / jnp.sum(e, axis=0, keepdims=True)
            eid_scr[pl.ds(r0, PEER_TOPK), :] = eids
        return carry

    lax.fori_loop(0, PEER_HEADS // _ROUTE_HEADS_PER_ITER, body, 0)
    gate_ref[...] = gate_scr[...].T
    eid_ref[...] = eid_scr[...].T


def _route(h, wq, keys):
    t, d = h.shape
    tm = TM_ROUTE
    nq = 2 * PEER_HEADS
    return pl.pallas_call(
        _route_kernel,
        grid=(t // tm,),
        in_specs=[pl.BlockSpec((tm, d), lambda i: (i, 0)),
                  pl.BlockSpec((d, nq * PEER_HALF), lambda i: (0, 0)),
                  pl.BlockSpec((2, PEER_KEYS, PEER_HALF), lambda i: (0, 0, 0))],
        out_specs=[pl.BlockSpec((tm, PEER_SEL), lambda i: (i, 0)),
                   pl.BlockSpec((tm, PEER_SEL), lambda i: (i, 0))],
        out_shape=[jax.ShapeDtypeStruct((t, PEER_SEL), I32),
                   jax.ShapeDtypeStruct((t, PEER_SEL), F32)],
        scratch_shapes=[pltpu.VMEM((nq, tm, PEER_HALF), F32),
                        pltpu.VMEM((PEER_SEL, tm), I32),
                        pltpu.VMEM((PEER_SEL, tm), F32)],
        compiler_params=_cparams(("parallel",), VMEM_LIMIT),
        name="peer_route",
    )(h, wq, keys)


def _peer_kernel(eid_cur_ref, eid_nxt_ref, x_ref, gate_ref, cmp_ref, exp_ref, uv_hbm, o_ref, buf, sem):
    step = pl.program_id(0)
    nsteps = pl.num_programs(0)
    g_tok = G_PEER
    rows = g_tok * PEER_SEL
    nsub = D_MODEL // LANES
    per_piece = rows // (2 * g_tok)

    def slot_wait(sl):
        pltpu.make_async_copy(uv_hbm.at[pl.ds(0, rows)], buf.at[sl], sem.at[sl]).wait()

    @pl.when(step == 0)
    def _():
        def body(r, carry):
            pltpu.make_async_copy(uv_hbm.at[eid_cur_ref[0, 0, r]], buf.at[0, r], sem.at[0]).start()
            return carry
        lax.fori_loop(0, rows, body, 0, unroll=8)

    diag = (lax.broadcasted_iota(I32, (nsub, PEER_SEL * nsub), 1) % nsub
            == lax.broadcasted_iota(I32, (nsub, PEER_SEL * nsub), 0))

    def group(sl, idx_ref, idx_off):
        tok0 = sl * g_tok

        def prefetch(r0, n):
            for i in range(n):
                r = r0 + i
                pltpu.make_async_copy(uv_hbm.at[idx_ref[0, 0, idx_off + r]], buf.at[1 - sl, r],
                                      sem.at[1 - sl]).start(priority=i % 2)

        def tiles(g):
            return buf[sl, g * PEER_SEL:(g + 1) * PEER_SEL].reshape(PEER_SEL * nsub, LANES)

        slot_wait(sl)
        parts = []
        for g in range(g_tok):
            u_rows = pltpu.bitcast(tiles(g) << 16, F32).astype(BF16)
            x_row = x_ref[tok0 + g:tok0 + g + 1, :]
            x_tile = jnp.concatenate([x_row[:, s * LANES:(s + 1) * LANES] for s in range(nsub)], axis=0)
            q = lax.dot_general(x_tile.astype(BF16), u_rows, (((1,), (1,)), ((), ())),
                                preferred_element_type=F32)
            parts.append(jnp.where(diag, q, 0.0))
            prefetch(g * per_piece, per_piece)
        masked = jnp.concatenate(parts, axis=0)
        m_hi = masked.astype(BF16)
        m_lo = (masked - m_hi.astype(F32)).astype(BF16)
        part = (jnp.dot(m_hi, cmp_ref[...], preferred_element_type=F32)
                + jnp.dot(m_lo, cmp_ref[...], preferred_element_type=F32))
        act = jnp.sum(part.reshape(g_tok, nsub, PEER_SEL), axis=1)
        wgt = gate_ref[tok0:tok0 + g_tok, :] * _gelu(act)
        wrep = jnp.dot(wgt.astype(BF16), exp_ref[...], preferred_element_type=F32)
        for g in range(g_tok):
            v_rows = pltpu.bitcast(tiles(g) & jnp.uint32(0xFFFF0000), F32).astype(BF16)
            wexp = jnp.where(diag, jnp.broadcast_to(wrep[g:g + 1, :], diag.shape), 0.0).astype(BF16)
            o_tile = jnp.dot(wexp, v_rows, preferred_element_type=F32)
            for s in range(nsub):
                o_ref[tok0 + g:tok0 + g + 1, s * LANES:(s + 1) * LANES] = o_tile[s:s + 1, :]
            prefetch((g_tok + g) * per_piece, per_piece)

    group(0, eid_cur_ref, rows)
    group(1, eid_nxt_ref, 0)

    @pl.when(step == nsteps - 1)
    def _():
        slot_wait(0)


def _peer(h, eid_tok, gate_tok, uv_packed):
    t, d = h.shape
    g_step = 2 * G_PEER
    nsteps = t // g_step
    rows = G_PEER * PEER_SEL
    nsub = d // LANES
    eid3 = eid_tok.reshape(nsteps, 1, 2 * rows)
    ncol = PEER_SEL * nsub
    col = jnp.arange(ncol) // nsub
    compress = (col[:, None] == jnp.arange(PEER_SEL)[None, :]).astype(F32)
    smem_blk = lambda f: pl.BlockSpec((1, 1, 2 * rows), f, memory_space=pltpu.SMEM)
    tile_blk = pl.BlockSpec((g_step, d), lambda i: (i, 0))
    return pl.pallas_call(
        _peer_kernel,
        grid=(nsteps,),
        in_specs=[smem_blk(lambda i: (i, 0, 0)),
                  smem_blk(lambda i: (jnp.minimum(i + 1, nsteps - 1), 0, 0)),
                  tile_blk,
                  pl.BlockSpec((g_step, PEER_SEL), lambda i: (i, 0)),
                  pl.BlockSpec((ncol, PEER_SEL), lambda i: (0, 0)),
                  pl.BlockSpec((PEER_SEL, ncol), lambda i: (0, 0)),
                  pl.BlockSpec(memory_space=pl.ANY)],
        out_specs=tile_blk,
        out_shape=jax.ShapeDtypeStruct((t, d), F32),
        scratch_shapes=[pltpu.VMEM((2, rows, nsub, LANES), U32), pltpu.SemaphoreType.DMA((2,))],
        compiler_params=_cparams(("arbitrary",), VMEM_LIMIT),
        name="peer_experts",
    )(eid3, eid3, h, gate_tok, compress.astype(BF16), compress.T.astype(BF16),
      uv_packed.reshape(-1, nsub, LANES))


def _pack_tables(u_tab, v_tab):
    ub = lax.bitcast_convert_type(u_tab.astype(BF16), jnp.uint16).astype(U32)
    vb = lax.bitcast_convert_type(v_tab.astype(BF16), jnp.uint16).astype(U32)
    return ub | (vb << 16)


def _out_kernel(h_ref, f_ref, p_ref, wp_ref, wgp_ref, g_ref, b_ref, o_ref):
    h = h_ref[...]
    ple = jnp.dot(p_ref[...].astype(BF16), wp_ref[...], preferred_element_type=F32)
    gate = _sigmoid(jnp.dot(h.astype(BF16), wgp_ref[...], preferred_element_type=F32))
    o_ref[...] = _ln(DN_ALPHA * h + f_ref[...] + ple * gate, g_ref[...], b_ref[...])


def _layer_out(h, ffn, p2, ple_w, ple_gate_w, ln_g, ln_b):
    t, d = h.shape
    tm = TM_OUT
    tok = lambda i: (i, 0)
    par = lambda i: (0, 0)
    return pl.pallas_call(
        _out_kernel,
        grid=(t // tm,),
        in_specs=[pl.BlockSpec((tm, d), tok), pl.BlockSpec((tm, d), tok),
                  pl.BlockSpec((tm, PLE_DIM), tok),
                  pl.BlockSpec((PLE_DIM, d), par), pl.BlockSpec((d, d), par),
                  pl.BlockSpec((1, d), par), pl.BlockSpec((1, d), par)],
        out_specs=pl.BlockSpec((tm, d), tok),
        out_shape=jax.ShapeDtypeStruct((t, d), F32),
        compiler_params=_cparams(("parallel",)),
        name="ple_ln2",
    )(h, ffn, p2, ple_w, ple_gate_w, ln_g.reshape(1, -1), ln_b.reshape(1, -1))


def _block_ones():
    head = jnp.arange(BRANCH) // HEAD_DIM
    return (head[:, None] == head[None, :]).astype(F32)


def _layer(h, p2, bsz, seq, layer, w_in_all, rwkv_mu, rwkv_w0, rwkv_w2, rwkv_a0, rwkv_a2, rwkv_g2, rwkv_kk,
           rwkv_ka, rwkv_rk, rwkv_lnx_g, rwkv_lnx_b, s5_lam_re, s5_lam_im, s5_b_re, s5_b_im, s5_c_re,
           s5_c_im, s5_d, s5_log_dt, s5_glu_w, s5_glu_b, fox_bf, conv_w, conv_b, conv_ln_g, conv_ln_b,
           w_branch, w_out, ln1_g, ln1_b, peer_wq, peer_k1, peer_k2, peer_u, peer_v, ple_w, ple_gate_w,
           ln2_g, ln2_b):
    t = bsz * seq
    d = D_MODEL
    w_packed, w_gate = _repack_w_in(w_in_all, layer)
    bd = _block_ones()
    wl = jnp.zeros((LANES, 3 * BRANCH), F32)
    wl = wl.at[0:32, 0:BRANCH].set(rwkv_w2.astype(F32))
    wl = wl.at[32:64, BRANCH:2 * BRANCH].set(rwkv_a2.astype(F32))
    wl = wl.at[64:128, 2 * BRANCH:].set(rwkv_g2.astype(F32))

    p_rwkv, p_s5_t, q_heads, k_heads, v_aug, p_ff, p_conv = _project(h, w_packed, bsz, seq)

    r, dec, k2, v, na, bb, g, bonus = _rwkv_pre(p_rwkv, rwkv_mu, rwkv_w0, rwkv_a0, rwkv_kk, rwkv_ka,
                                                 rwkv_rk.reshape(-1), wl, bd, bsz, seq)
    y_scan = _rwkv_scan(*(_to_scan(x, bsz, seq) for x in (na, dec, bb, k2, r, v)))
    y_rwkv = _rwkv_post(_from_scan(y_scan, bsz, seq), g.reshape(t, BRANCH), bonus.reshape(t, BRANCH),
                        rwkv_lnx_g, rwkv_lnx_b, bd)

    bre, bim, lre, lim, cc = _s5_params(s5_lam_re, s5_lam_im, s5_b_re, s5_b_im, s5_c_re, s5_c_im, s5_log_dt)
    y_s5_t = _s5(p_s5_t, bre, bim, lre, lim, cc, s5_d, s5_glu_w, s5_glu_b, bsz, seq)

    f_rows = p_ff[:, :HEADS].reshape(bsz, seq, HEADS).transpose(0, 2, 1).reshape(bsz * HEADS, seq)
    bias_rows = jnp.tile(fox_bf.astype(F32), bsz).reshape(bsz * HEADS, 1)
    c = _fox_cumsum(f_rows, bias_rows)
    y_fox = _fox_attn(q_heads, k_heads, v_aug,
                      c.reshape(bsz, HEADS, seq, 1), c.reshape(bsz, HEADS, seq // TK_ATT, TK_ATT))

    y_conv = _conv_mixer(p_conv, conv_w, conv_b, conv_ln_g, conv_ln_b, bsz, seq)

    h1 = _merge(h, y_rwkv, y_s5_t, y_fox, y_conv, w_gate, w_branch.astype(BF16),
                w_out.astype(BF16), ln1_g, ln1_b, bsz, seq)

    keys = jnp.stack([peer_k1, peer_k2]).astype(F32)
    eid_tok, gate_tok = _route(h1, peer_wq.astype(BF16), keys)
    ffn = _peer(h1, eid_tok, gate_tok, _pack_tables(peer_u, peer_v))

    return _layer_out(h1, ffn, p2, ple_w.astype(BF16), ple_gate_w.astype(BF16), ln2_g, ln2_b)


def kernel(x, p, ln_in_g, ln_in_b, w_in, rwkv_mu, rwkv_w0, rwkv_w2, rwkv_a0, rwkv_a2, rwkv_g2, rwkv_kk, rwkv_ka, rwkv_rk, rwkv_lnx_g, rwkv_lnx_b, s5_lam_re, s5_lam_im, s5_b_re, s5_b_im, s5_c_re, s5_c_im, s5_d, s5_log_dt, s5_glu_w, s5_glu_b, fox_bf, conv_w, conv_b, conv_ln_g, conv_ln_b, w_branch, w_out, ln1_g, ln1_b, peer_wq, peer_k1, peer_k2, peer_u, peer_v, ple_w, ple_gate_w, ln2_g, ln2_b):
    bsz, seq, d = x.shape
    t = bsz * seq
    h = _layer_norm(x.reshape(t, d), ln_in_g, ln_in_b)
    per_layer = (rwkv_mu, rwkv_w0, rwkv_w2, rwkv_a0, rwkv_a2, rwkv_g2, rwkv_kk, rwkv_ka, rwkv_rk,
                 rwkv_lnx_g, rwkv_lnx_b, s5_lam_re, s5_lam_im, s5_b_re, s5_b_im, s5_c_re, s5_c_im, s5_d,
                 s5_log_dt, s5_glu_w, s5_glu_b, fox_bf, conv_w, conv_b, conv_ln_g, conv_ln_b, w_branch,
                 w_out, ln1_g, ln1_b, peer_wq, peer_k1, peer_k2, peer_u, peer_v, ple_w, ple_gate_w,
                 ln2_g, ln2_b)
    for i in range(p.shape[0]):
        h = _layer(h, p[i].reshape(t, PLE_DIM), bsz, seq, i, w_in, *(w[i] for w in per_layer))
    return h.reshape(bsz, seq, d)
```

```python
import functools
import math

import jax
import jax.numpy as jnp
from jax import lax
from jax.experimental import pallas as pl
from jax.experimental.pallas import tpu as pltpu

F32 = jnp.float32
BF16 = jnp.bfloat16
I32 = jnp.int32
U32 = jnp.uint32
HI = lax.Precision.HIGHEST

D_MODEL = 1024
BRANCH = 256
HEADS = 4
HEAD_DIM = 64
RWKV_COLS = 896
S5_GROUPS = 16
S5_GROUP = 16
S5_STATE = 64
S5_N = S5_GROUPS * S5_STATE
CONV_WIDTH = 31
CONV_HALO = 32
PEER_HEADS = 8
PEER_KEYS = 128
PEER_HALF = 128
PEER_TOPK = 16
PEER_SEL = PEER_HEADS * PEER_TOPK
PLE_DIM = 256
RWKV_GN_EPS = 64e-5
LN_EPS = 1e-5
DEPTH = 2
DN_ALPHA = (2 * DEPTH) ** 0.25

SUBLANES = 8
LANES = 128
VMEM_LIMIT = 56 * 1024 * 1024

TM_PROJ = 512
TQ_PRE = 256
TC_SCAN = 32
TC_S5 = 32
TQ_ATT = 256
TK_ATT = 256
TC_CONV = 512
TM_MERGE = 256
TM_ROUTE = 128
G_PEER = 16
TM_OUT = 256


def _cparams(sem, vmem=None):
    return pltpu.CompilerParams(dimension_semantics=sem, vmem_limit_bytes=vmem)


def _ln(z, g, b):
    mu = jnp.mean(z, axis=-1, keepdims=True)
    zc = z - mu
    var = jnp.mean(zc * zc, axis=-1, keepdims=True)
    return zc * lax.rsqrt(var + LN_EPS) * g + b


def _gelu(y):
    return 0.5 * y * (1.0 + lax.erf(y * (1.0 / math.sqrt(2.0))))


def _sigmoid(y):
    return 1.0 / (1.0 + jnp.exp(-y))


def _softplus(y):
    return jnp.maximum(y, 0.0) + jnp.log(1.0 + jnp.exp(-jnp.abs(y)))


def _ln_kernel(x_ref, g_ref, b_ref, o_ref):
    o_ref[...] = _ln(x_ref[...], g_ref[...], b_ref[...])


def _layer_norm(x2, g, b):
    t, d = x2.shape
    tm = TM_PROJ
    return pl.pallas_call(
        _ln_kernel,
        grid=(t // tm,),
        in_specs=[pl.BlockSpec((tm, d), lambda i: (i, 0)),
                  pl.BlockSpec((1, d), lambda i: (0, 0)),
                  pl.BlockSpec((1, d), lambda i: (0, 0))],
        out_specs=pl.BlockSpec((tm, d), lambda i: (i, 0)),
        out_shape=jax.ShapeDtypeStruct((t, d), F32),
        compiler_params=_cparams(("parallel",)),
        name="ln_in",
    )(x2, g.reshape(1, d), b.reshape(1, d))


_P_RWKV = (0, 896)
_P_S5 = (896, 1152)
_P_FQKV = (1152, 1920)
_P_FF = (1920, 2048)
_P_CONV = (2048, 2560)
_P_COLS = 2560


_N_FRONT = RWKV_COLS + BRANCH + 3 * BRANCH + HEADS
_TR_REPACK = 128


def _repack_kernel(w_ref, wp_ref, wg_ref):
    w = w_ref[0]
    lo, hi = _P_FF
    wp_ref[:, 0:lo] = w[:, 0:lo].astype(BF16)
    lane = lax.broadcasted_iota(I32, (w.shape[0], hi - lo), 1)
    wp_ref[:, lo:hi] = jnp.where(lane < HEADS, w[:, lo:hi], 0.0).astype(BF16)
    wp_ref[:, hi:_P_COLS] = w[:, _N_FRONT:_N_FRONT + 2 * BRANCH].astype(BF16)
    wg_ref[...] = w[:, _N_FRONT + 2 * BRANCH:].astype(BF16)


def _repack_w_in(w_in_all, layer):
    _, d, n = w_in_all.shape
    n_gate = n - _N_FRONT - 2 * BRANCH
    tr = _TR_REPACK
    return pl.pallas_call(
        _repack_kernel,
        grid=(d // tr,),
        in_specs=[pl.BlockSpec((1, tr, n), lambda i: (layer, i, 0))],
        out_specs=[pl.BlockSpec((tr, _P_COLS), lambda i: (i, 0)),
                   pl.BlockSpec((tr, n_gate), lambda i: (i, 0))],
        out_shape=[jax.ShapeDtypeStruct((d, _P_COLS), BF16),
                   jax.ShapeDtypeStruct((d, n_gate), BF16)],
        compiler_params=_cparams(("parallel",)),
        name="repack_w_in",
    )(w_in_all)


def _proj_kernel(h_ref, w_ref, rw_ref, s5_ref, q_ref, k_ref, v_ref, ff_ref, cv_ref):
    x = h_ref[...].astype(BF16)
    for (lo, hi), o_ref in ((_P_RWKV, rw_ref), (_P_S5, s5_ref), (_P_FF, ff_ref), (_P_CONV, cv_ref)):
        o_ref[...] = jnp.dot(x, w_ref[:, lo:hi], preferred_element_type=F32)
    qkv = jnp.dot(x, w_ref[:, _P_FQKV[0]:_P_FQKV[1]], preferred_element_type=F32)
    tm = qkv.shape[0]
    ones_col = (lax.broadcasted_iota(I32, (tm, LANES - HEAD_DIM), 1) == 0).astype(F32)
    for hh in range(HEADS):
        q_ref[0, hh] = qkv[:, hh * HEAD_DIM:(hh + 1) * HEAD_DIM]
        k_ref[0, hh] = qkv[:, BRANCH + hh * HEAD_DIM:BRANCH + (hh + 1) * HEAD_DIM]
        v_ref[0, hh] = jnp.concatenate(
            [qkv[:, 2 * BRANCH + hh * HEAD_DIM:2 * BRANCH + (hh + 1) * HEAD_DIM], ones_col], axis=1)


def _project(h, w_packed, bsz, seq):
    t, d = h.shape
    tm = TM_PROJ
    nst = seq // tm
    w_rwkv, w_s5, w_ff, w_conv = [hi - lo for lo, hi in (_P_RWKV, _P_S5, _P_FF, _P_CONV)]
    tok = lambda i: (i, 0)
    heads = lambda i: (i // nst, 0, i % nst, 0)
    out_shapes = [jax.ShapeDtypeStruct((t, w_rwkv), F32),
                  jax.ShapeDtypeStruct((seq, bsz * w_s5), F32),
                  jax.ShapeDtypeStruct((bsz, HEADS, seq, HEAD_DIM), F32),
                  jax.ShapeDtypeStruct((bsz, HEADS, seq, HEAD_DIM), F32),
                  jax.ShapeDtypeStruct((bsz, HEADS, seq, LANES), F32),
                  jax.ShapeDtypeStruct((t, w_ff), F32),
                  jax.ShapeDtypeStruct((t, w_conv), F32)]
    out_specs = [pl.BlockSpec((tm, w_rwkv), tok),
                 pl.BlockSpec((tm, w_s5), lambda i: (i % nst, i // nst)),
                 pl.BlockSpec((1, HEADS, tm, HEAD_DIM), heads),
                 pl.BlockSpec((1, HEADS, tm, HEAD_DIM), heads),
                 pl.BlockSpec((1, HEADS, tm, LANES), heads),
                 pl.BlockSpec((tm, w_ff), tok),
                 pl.BlockSpec((tm, w_conv), tok)]
    return pl.pallas_call(
        _proj_kernel,
        grid=(t // tm,),
        in_specs=[pl.BlockSpec((tm, d), tok),
                  pl.BlockSpec((d, _P_COLS), lambda i: (0, 0))],
        out_specs=out_specs,
        out_shape=out_shapes,
        compiler_params=_cparams(("parallel",), VMEM_LIMIT),
        name="in_proj",
    )(h, w_packed)


def _rwkv_pre_kernel(z_ref, zp_ref, mu_ref, w0_ref, a0_ref, kk_ref, ka_ref, rk_ref, wl_ref, bd_ref,
                     r_out, w_out, k_out, v_out, na_out, b_out, g_out, bv_out):
    j = pl.program_id(1)
    z = z_ref[0]
    prev = zp_ref[0][SUBLANES - 1:SUBLANES, :]
    prev = jnp.where(j == 0, 0.0, prev)
    row = lax.broadcasted_iota(I32, z.shape, 0)
    zs = jnp.where(row == 0, prev, pltpu.roll(z, 1, 0))
    z = z + (zs - z) * mu_ref[...]
    r = z[:, 0:256]
    k = z[:, 256:512]
    v = z[:, 512:768]
    zc = z[:, 768:896]
    lane = lax.broadcasted_iota(I32, zc.shape, 1)
    act = jnp.where(lane < 32, jnp.tanh(zc), jnp.where(lane < 64, zc, _sigmoid(zc)))
    lo = jnp.dot(act, wl_ref[...], precision=HI, preferred_element_type=F32)
    dw = lo[:, 0:256]
    da = lo[:, 256:512]
    g = lo[:, 512:768]
    w_log = -_softplus(-(w0_ref[...] + dw)) - 0.5
    decay = jnp.exp(-jnp.exp(w_log))
    a = _sigmoid(a0_ref[...] + da)
    kkv = k * kk_ref[...]
    bd = bd_ref[...]
    ss = jnp.dot(kkv * kkv, bd, precision=HI, preferred_element_type=F32)
    kkn = kkv * lax.rsqrt(jnp.maximum(ss, 1e-24))
    k2 = k * (1.0 + (a - 1.0) * ka_ref[...])
    bonus = jnp.dot(r * k2 * rk_ref[...], bd, precision=HI, preferred_element_type=F32) * v
    r_out[0] = r
    w_out[0] = decay
    k_out[0] = k2
    v_out[0] = v
    na_out[0] = -kkn
    b_out[0] = kkn * a
    g_out[0] = g
    bv_out[0] = bonus


def _rwkv_pre(p_rwkv, mu, w0, a0, kk, ka, rk, wl, bd, bsz, seq):
    z3 = p_rwkv.reshape(bsz, seq, RWKV_COLS)
    tq = TQ_PRE
    blk = lambda b, j: (b, j, 0)
    par = lambda b, j: (0, 0)
    out_shape = [jax.ShapeDtypeStruct((bsz, seq, BRANCH), F32)] * 8
    out_specs = [pl.BlockSpec((1, tq, BRANCH), blk)] * 8
    return pl.pallas_call(
        _rwkv_pre_kernel,
        grid=(bsz, seq // tq),
        in_specs=[pl.BlockSpec((1, tq, RWKV_COLS), blk),
                  pl.BlockSpec((1, SUBLANES, RWKV_COLS),
                               lambda b, j: (b, jnp.maximum(j * (tq // SUBLANES) - 1, 0), 0)),
                  pl.BlockSpec((1, RWKV_COLS), par),
                  pl.BlockSpec((1, BRANCH), par), pl.BlockSpec((1, BRANCH), par),
                  pl.BlockSpec((1, BRANCH), par), pl.BlockSpec((1, BRANCH), par),
                  pl.BlockSpec((1, BRANCH), par),
                  pl.BlockSpec((LANES, 3 * BRANCH), par),
                  pl.BlockSpec((BRANCH, BRANCH), par)],
        out_specs=out_specs,
        out_shape=out_shape,
        compiler_params=_cparams(("parallel", "parallel")),
        name="rwkv_pre",
    )(z3, z3, mu.reshape(1, -1), w0.reshape(1, -1), a0.reshape(1, -1), kk.reshape(1, -1),
      ka.reshape(1, -1), rk.reshape(1, -1), wl, bd)


_SCAN_IG = HEAD_DIM // 2 // SUBLANES
_SCAN_HALF = HEAD_DIM // 2


def _rwkv_scan_kernel(a_ref, w_ref, b_ref, k_ref, r_ref, v_ref, y_ref, s_ref, a_scr, w_scr, b_scr, k_scr, r_scr):
    @pl.when(pl.program_id(0) == 0)
    def _():
        s_ref[...] = jnp.zeros_like(s_ref)

    tc = a_ref.shape[0]
    tile = (SUBLANES, LANES)

    lane = lax.broadcasted_iota(I32, (tc * _SCAN_HALF, LANES), 1)
    for src_ref, dst_ref in ((a_ref, a_scr), (w_ref, w_scr), (b_ref, b_scr), (k_ref, k_scr), (r_ref, r_scr)):
        x = src_ref[...].reshape(tc * _SCAN_HALF, LANES)
        xr = pltpu.roll(x, LANES // 2, 1)
        dst_ref[:, 0:_SCAN_HALF, :] = jnp.where(lane < LANES // 2, x, xr).reshape(tc, _SCAN_HALF, LANES)
        dst_ref[:, _SCAN_HALF:, :] = jnp.where(lane < LANES // 2, xr, x).reshape(tc, _SCAN_HALF, LANES)

    def step(t, carry):
        vv = [v_ref[t, ig * SUBLANES:(ig + 1) * SUBLANES, :] for ig in range(_SCAN_IG)]
        sa = [[jnp.zeros(tile, F32), jnp.zeros(tile, F32)] for _ in range(_SCAN_IG)]
        for j in range(HEAD_DIM):
            ab = jnp.broadcast_to(a_scr[t, j:j + 1, :], tile)
            for ig in range(_SCAN_IG):
                sa[ig][j % 2] = sa[ig][j % 2] + s_ref[ig, j] * ab
        sa = [x[0] + x[1] for x in sa]
        yy = [[jnp.zeros(tile, F32), jnp.zeros(tile, F32)] for _ in range(_SCAN_IG)]
        for j in range(HEAD_DIM):
            wb = jnp.broadcast_to(w_scr[t, j:j + 1, :], tile)
            bb = jnp.broadcast_to(b_scr[t, j:j + 1, :], tile)
            kb = jnp.broadcast_to(k_scr[t, j:j + 1, :], tile)
            rb = jnp.broadcast_to(r_scr[t, j:j + 1, :], tile)
            for ig in range(_SCAN_IG):
                s = s_ref[ig, j] * wb + sa[ig] * bb + vv[ig] * kb
                s_ref[ig, j] = s
                yy[ig][j % 2] = yy[ig][j % 2] + s * rb
        for ig in range(_SCAN_IG):
            y_ref[t, ig * SUBLANES:(ig + 1) * SUBLANES, :] = yy[ig][0] + yy[ig][1]
        return carry

    lax.fori_loop(0, tc, step, 0)


def _rwkv_scan(a_t, w_t, b_t, k_t, r_t, v_t):
    seq = a_t.shape[0]
    tc = TC_SCAN
    spec = pl.BlockSpec((tc, _SCAN_HALF, LANES), lambda i: (i, 0, 0))
    return pl.pallas_call(
        _rwkv_scan_kernel,
        grid=(seq // tc,),
        in_specs=[spec] * 6,
        out_specs=spec,
        out_shape=jax.ShapeDtypeStruct((seq, _SCAN_HALF, LANES), F32),
        scratch_shapes=[pltpu.VMEM((_SCAN_IG, HEAD_DIM, SUBLANES, LANES), F32)]
                       + [pltpu.VMEM((tc, HEAD_DIM, LANES), F32)] * 5,
        compiler_params=_cparams(("arbitrary",)),
        name="rwkv_scan",
    )(a_t, w_t, b_t, k_t, r_t, v_t)


def _to_scan(x, bsz, seq):
    n_pairs = bsz * HEADS
    xt = x.reshape(bsz, seq, HEADS, 2, HEAD_DIM // 2).transpose(1, 4, 3, 0, 2)
    xt = xt.reshape(seq, HEAD_DIM // 2, 2, n_pairs)
    pad = LANES // 2 - n_pairs
    if pad:
        xt = jnp.pad(xt, ((0, 0), (0, 0), (0, 0), (0, pad)))
    return xt.reshape(seq, HEAD_DIM // 2, LANES)


def _from_scan(y, bsz, seq):
    n_pairs = bsz * HEADS
    yt = y.reshape(seq, HEAD_DIM // 2, 2, LANES // 2)[..., :n_pairs]
    yt = yt.reshape(seq, HEAD_DIM // 2, 2, bsz, HEADS).transpose(3, 0, 4, 2, 1)
    return yt.reshape(bsz * seq, BRANCH)


def _rwkv_post_kernel(y_ref, g_ref, bv_ref, gg_ref, gb_ref, bd_ref, o_ref):
    y = y_ref[...]
    bd = bd_ref[...]
    inv = 1.0 / HEAD_DIM
    mean = jnp.dot(y, bd, precision=HI, preferred_element_type=F32) * inv
    yc = y - mean
    var = jnp.dot(yc * yc, bd, precision=HI, preferred_element_type=F32) * inv
    yn = yc * lax.rsqrt(var + RWKV_GN_EPS) * gg_ref[...] + gb_ref[...]
    o_ref[...] = (yn + bv_ref[...]) * g_ref[...]


def _rwkv_post(y, g, bv, gn_g, gn_b, bd):
    t = y.shape[0]
    tm = TM_PROJ
    tok = lambda i: (i, 0)
    par = lambda i: (0, 0)
    return pl.pallas_call(
        _rwkv_post_kernel,
        grid=(t // tm,),
        in_specs=[pl.BlockSpec((tm, BRANCH), tok)] * 3
                 + [pl.BlockSpec((1, BRANCH), par)] * 2 + [pl.BlockSpec((BRANCH, BRANCH), par)],
        out_specs=pl.BlockSpec((tm, BRANCH), tok),
        out_shape=jax.ShapeDtypeStruct((t, BRANCH), F32),
        compiler_params=_cparams(("parallel",)),
        name="rwkv_post",
    )(y, g, bv, gn_g.reshape(1, -1), gn_b.reshape(1, -1), bd)


def _s5_kernel(u_ref, bre_ref, bim_ref, lre_ref, lim_ref, cc_ref, d_ref, gw_ref, gb_ref, o_ref,
               xre_ref, xim_ref, bure_ref, buim_ref, xs_ref):
    @pl.when(pl.program_id(0) == 0)
    def _():
        xre_ref[...] = jnp.zeros_like(xre_ref)
        xim_ref[...] = jnp.zeros_like(xim_ref)

    tc, nb, w = u_ref.shape
    u2 = u_ref[...].reshape(tc * nb, w)
    ub = u2.astype(BF16)
    bure_ref[...] = jnp.dot(ub, bre_ref[...], preferred_element_type=F32)
    buim_ref[...] = jnp.dot(ub, bim_ref[...], preferred_element_type=F32)
    lre = jnp.broadcast_to(lre_ref[...], (nb, S5_N))
    lim = jnp.broadcast_to(lim_ref[...], (nb, S5_N))

    def step(t, carry):
        xr, xi = carry
        r0 = pl.multiple_of(t * nb, nb)
        nr = lre * xr - lim * xi + bure_ref[pl.ds(r0, nb), :]
        ni = lre * xi + lim * xr + buim_ref[pl.ds(r0, nb), :]
        xs_ref[pl.ds(r0, nb), 0:S5_N] = nr
        xs_ref[pl.ds(r0, nb), S5_N:2 * S5_N] = ni
        return nr, ni

    xr, xi = lax.fori_loop(0, tc, step, (xre_ref[...], xim_ref[...]))
    xre_ref[...] = xr
    xim_ref[...] = xi
    y = jnp.dot(xs_ref[...].astype(BF16), cc_ref[...], preferred_element_type=F32) + d_ref[...] * u2
    y = _gelu(y)
    gate = jnp.dot(y.astype(BF16), gw_ref[...], preferred_element_type=F32) + gb_ref[...]
    o_ref[...] = (y * _sigmoid(gate)).reshape(tc, nb, w)


def _s5(u_t, bre, bim, lre, lim, cc, d_skip, glu_w, glu_b, bsz, seq):
    u3 = u_t.reshape(seq, bsz, BRANCH)
    tc = TC_S5
    par = lambda i: (0, 0)
    blk = pl.BlockSpec((tc, bsz, BRANCH), lambda i: (i, 0, 0))
    out = pl.pallas_call(
        _s5_kernel,
        grid=(seq // tc,),
        in_specs=[blk,
                  pl.BlockSpec((BRANCH, S5_N), par), pl.BlockSpec((BRANCH, S5_N), par),
                  pl.BlockSpec((1, S5_N), par), pl.BlockSpec((1, S5_N), par),
                  pl.BlockSpec((2 * S5_N, BRANCH), par),
                  pl.BlockSpec((1, BRANCH), par),
                  pl.BlockSpec((BRANCH, BRANCH), par),
                  pl.BlockSpec((1, BRANCH), par)],
        out_specs=blk,
        out_shape=jax.ShapeDtypeStruct((seq, bsz, BRANCH), F32),
        scratch_shapes=[pltpu.VMEM((bsz, S5_N), F32), pltpu.VMEM((bsz, S5_N), F32),
                        pltpu.VMEM((tc * bsz, S5_N), F32), pltpu.VMEM((tc * bsz, S5_N), F32),
                        pltpu.VMEM((tc * bsz, 2 * S5_N), F32)],
        compiler_params=_cparams(("arbitrary",), VMEM_LIMIT),
        name="s5_scan",
    )(u3, bre.astype(BF16), bim.astype(BF16), lre, lim, cc.astype(BF16), d_skip.reshape(1, -1),
      glu_w.astype(BF16), glu_b.reshape(1, -1))
    return out.reshape(seq, bsz * BRANCH)


def _s5_params(lam_re, lam_im, b_re, b_im, c_re, c_im, log_dt):
    lam = lax.complex(lam_re.astype(F32), lam_im.astype(F32))
    dt = jnp.exp(log_dt.astype(F32))[:, None]
    lam_bar = jnp.exp(lam * dt)
    b_bar = ((lam_bar - 1.0) / lam)[..., None] * lax.complex(b_re.astype(F32), b_im.astype(F32))
    eye = jnp.eye(S5_GROUPS, dtype=F32)
    bre = jnp.einsum('gpc,gh->gchp', jnp.real(b_bar), eye).reshape(BRANCH, S5_N)
    bim = jnp.einsum('gpc,gh->gchp', jnp.imag(b_bar), eye).reshape(BRANCH, S5_N)
    cre = jnp.einsum('gcp,gh->gphc', c_re.astype(F32), eye).reshape(S5_N, BRANCH)
    cim = jnp.einsum('gcp,gh->gphc', c_im.astype(F32), eye).reshape(S5_N, BRANCH)
    cc = jnp.concatenate([cre, -cim], axis=0)
    return bre, bim, jnp.real(lam_bar).reshape(1, S5_N), jnp.imag(lam_bar).reshape(1, S5_N), cc


def _fox_cumsum_kernel(f_ref, bias_ref, c_ref):
    x = f_ref[...] + bias_ref[...]
    c = -_softplus(-x)
    n = c.shape[1]
    lane = lax.broadcasted_iota(I32, c.shape, 1)
    sh = 1
    while sh < n:
        c = c + jnp.where(lane >= sh, pltpu.roll(c, sh, 1), 0.0)
        sh *= 2
    c_ref[...] = c


def _fox_cumsum(f_rows, bias_rows):
    rows, seq = f_rows.shape
    return pl.pallas_call(
        _fox_cumsum_kernel,
        out_shape=jax.ShapeDtypeStruct((rows, seq), F32),
        name="fox_cumsum",
    )(f_rows, bias_rows)


def _fox_attn_kernel(q_ref, k_ref, v_ref, cq_ref, ck_ref, o_ref):
    qi = pl.program_id(1)
    nh, tq = q_ref.shape[1], q_ref.shape[2]
    tk = ck_ref.shape[3]
    qs = [(q_ref[0, h] * (HEAD_DIM ** -0.5)).astype(BF16) for h in range(nh)]
    cqs = [cq_ref[0, h] for h in range(nh)]
    qpos = qi * tq + lax.broadcasted_iota(I32, (tq, tk), 0)
    nkb = (qi * tq + tq + tk - 1) // tk

    def body(kb, carry):
        k0 = pl.multiple_of(kb * tk, tk)
        causal = k0 + lax.broadcasted_iota(I32, (tq, tk), 1) <= qpos
        out = []
        for h in range(nh):
            m, acc = carry[h]
            kblk = k_ref[0, h, pl.ds(k0, tk), :].astype(BF16)
            vblk = v_ref[0, h, pl.ds(k0, tk), :].astype(BF16)
            s = lax.dot_general(qs[h], kblk, (((1,), (1,)), ((), ())), preferred_element_type=F32)
            s = s + cqs[h] - ck_ref[0, h, pl.ds(kb, 1), :]
            s = jnp.where(causal, s, -1e30)
            m_new = jnp.maximum(m, jnp.max(s, axis=-1, keepdims=True))
            p = jnp.exp(s - m_new)
            acc = jnp.exp(m - m_new) * acc + jnp.dot(p.astype(BF16), vblk, preferred_element_type=F32)
            out.append((m_new, acc))
        return tuple(out)

    init = tuple((jnp.full((tq, 1), -1e30, F32), jnp.zeros((tq, v_ref.shape[3]), F32)) for _ in range(nh))
    res = lax.fori_loop(0, nkb, body, init)
    for h in range(nh):
        acc = res[h][1]
        o_ref[0, h] = acc[:, 0:HEAD_DIM] / acc[:, HEAD_DIM:HEAD_DIM + 1]


def _fox_attn(q, k, v, c_col, c_rowb):
    bsz, nh, seq, hd = q.shape
    tq = TQ_ATT
    nkb, tk = c_rowb.shape[2], c_rowb.shape[3]
    full = lambda b, i: (b, 0, 0, 0)
    tile = lambda b, i: (b, 0, i, 0)
    return pl.pallas_call(
        _fox_attn_kernel,
        grid=(bsz, seq // tq),
        in_specs=[pl.BlockSpec((1, nh, tq, hd), tile),
                  pl.BlockSpec((1, nh, seq, hd), full),
                  pl.BlockSpec((1, nh, seq, v.shape[3]), full),
                  pl.BlockSpec((1, nh, tq, 1), tile),
                  pl.BlockSpec((1, nh, nkb, tk), full)],
        out_specs=pl.BlockSpec((1, nh, tq, hd), tile),
        out_shape=jax.ShapeDtypeStruct((bsz, nh, seq, hd), F32),
        compiler_params=_cparams(("parallel", "parallel"), VMEM_LIMIT),
        name="fox_attn",
    )(q, k, v, c_col, c_rowb)


def _conv_kernel(x_ref, xp_ref, w_ref, b_ref, g_ref, be_ref, o_ref, buf_ref):
    j = pl.program_id(1)
    tc = x_ref.shape[1]
    x = x_ref[0]
    buf_ref[CONV_HALO:CONV_HALO + tc, :] = x[:, 0:BRANCH] * _sigmoid(x[:, BRANCH:2 * BRANCH])
    xp = xp_ref[0]
    hp = xp[:, 0:BRANCH] * _sigmoid(xp[:, BRANCH:2 * BRANCH])
    buf_ref[0:CONV_HALO, :] = jnp.where(j == 0, 0.0, hp)
    acc = jnp.zeros((tc, BRANCH), F32) + b_ref[...]
    off = CONV_HALO - (CONV_WIDTH - 1)
    for kk in range(CONV_WIDTH):
        acc = acc + buf_ref[off + kk:off + kk + tc, :] * w_ref[kk:kk + 1, :]
    y = _ln(acc, g_ref[...], be_ref[...])
    o_ref[0] = y * _sigmoid(y)


def _conv_mixer(p_conv, conv_w, conv_b, ln_g, ln_b, bsz, seq):
    x3 = p_conv.reshape(bsz, seq, 2 * BRANCH)
    tc = TC_CONV
    par = lambda b, j: (0, 0)
    out = pl.pallas_call(
        _conv_kernel,
        grid=(bsz, seq // tc),
        in_specs=[pl.BlockSpec((1, tc, 2 * BRANCH), lambda b, j: (b, j, 0)),
                  pl.BlockSpec((1, CONV_HALO, 2 * BRANCH),
                               lambda b, j: (b, jnp.maximum(j * (tc // CONV_HALO) - 1, 0), 0)),
                  pl.BlockSpec((CONV_WIDTH, BRANCH), par),
                  pl.BlockSpec((1, BRANCH), par), pl.BlockSpec((1, BRANCH), par),
                  pl.BlockSpec((1, BRANCH), par)],
        out_specs=pl.BlockSpec((1, tc, BRANCH), lambda b, j: (b, j, 0)),
        out_shape=jax.ShapeDtypeStruct((bsz, seq, BRANCH), F32),
        scratch_shapes=[pltpu.VMEM((CONV_HALO + tc, BRANCH), F32)],
        compiler_params=_cparams(("parallel", "parallel")),
        name="conv_mixer",
    )(x3, x3, conv_w, conv_b.reshape(1, -1), ln_g.reshape(1, -1), ln_b.reshape(1, -1))
    return out.reshape(bsz * seq, BRANCH)


def _merge_kernel(h_ref, y0_ref, y1_ref, y2_ref, y3_ref, wg_ref, wb_ref, wo_ref, g_ref, b_ref, o_ref):
    h = h_ref[...]
    hb = h.astype(BF16)
    merged = None
    for br, y_ref in enumerate((y0_ref, y1_ref, y2_ref, y3_ref)):
        gate = _sigmoid(jnp.dot(hb, wg_ref[:, br * D_MODEL:(br + 1) * D_MODEL], preferred_element_type=F32))
        if br == 2:
            proj = None
            for hh in range(HEADS):
                part = jnp.dot(y_ref[0, hh].astype(BF16), wb_ref[br, hh * HEAD_DIM:(hh + 1) * HEAD_DIM, :],
                               preferred_element_type=F32)
                proj = part if proj is None else proj + part
        else:
            proj = jnp.dot(y_ref[...].astype(BF16), wb_ref[br], preferred_element_type=F32)
        term = gate * proj
        merged = term if merged is None else merged + term
    mix = jnp.dot(merged.astype(BF16), wo_ref[...], preferred_element_type=F32)
    o_ref[...] = _ln(DN_ALPHA * h + mix, g_ref[...], b_ref[...])


def _merge(h, y_rwkv, y_s5_t, y_fox, y_conv, wg, wb, wo, ln_g, ln_b, bsz, seq):
    t, d = h.shape
    tm = TM_MERGE
    nst = seq // tm
    tok = lambda i: (i, 0)
    par2 = lambda i: (0, 0)
    br_spec = pl.BlockSpec((tm, BRANCH), tok)
    return pl.pallas_call(
        _merge_kernel,
        grid=(t // tm,),
        in_specs=[pl.BlockSpec((tm, d), tok),
                  br_spec,
                  pl.BlockSpec((tm, BRANCH), lambda i: (i % nst, i // nst)),
                  pl.BlockSpec((1, HEADS, tm, HEAD_DIM), lambda i: (i // nst, 0, i % nst, 0)),
                  br_spec,
                  pl.BlockSpec((d, 4 * d), par2),
                  pl.BlockSpec((4, BRANCH, d), lambda i: (0, 0, 0)),
                  pl.BlockSpec((d, d), par2),
                  pl.BlockSpec((1, d), par2), pl.BlockSpec((1, d), par2)],
        out_specs=pl.BlockSpec((tm, d), tok),
        out_shape=jax.ShapeDtypeStruct((t, d), F32),
        compiler_params=_cparams(("parallel",), VMEM_LIMIT),
        name="merge_ln1",
    )(h, y_rwkv, y_s5_t, y_fox, y_conv, wg, wb, wo, ln_g.reshape(1, -1), ln_b.reshape(1, -1))


def _sort_network(n):
    pairs = []
    p = 1
    while p < n:
        k = p
        while k >= 1:
            for j in range(k % p, n - k, 2 * k):
                for i in range(min(k, n - j - k)):
                    if (i + j) // (2 * p) == (i + j + k) // (2 * p):
                        pairs.append((i + j, i + j + k))
            k //= 2
        p *= 2
    return pairs


_KEY_SORT_PAIRS = _sort_network(PEER_KEYS // SUBLANES)


def _top_keys(st):
    nv = PEER_KEYS // SUBLANES
    tm = st.shape[1]
    sub = lax.broadcasted_iota(I32, (SUBLANES, tm), 0)
    vals = [st[k * SUBLANES:(k + 1) * SUBLANES, :] for k in range(nv)]
    keys = [k * SUBLANES + sub for k in range(nv)]
    for i, j in _KEY_SORT_PAIRS:
        first = (vals[i] > vals[j]) | ((vals[i] == vals[j]) & (keys[i] < keys[j]))
        vals[i], vals[j] = jnp.where(first, vals[i], vals[j]), jnp.where(first, vals[j], vals[i])
        keys[i], keys[j] = jnp.where(first, keys[i], keys[j]), jnp.where(first, keys[j], keys[i])
    out_v, out_k = [], []
    for t in range(PEER_TOPK):
        m = jnp.max(vals[0], axis=0, keepdims=True)
        kmin = jnp.min(jnp.where(vals[0] == m, keys[0], jnp.int32(2 ** 30)), axis=0, keepdims=True)
        out_v.append(m)
        out_k.append(kmin)
        win = keys[0] == kmin
        for d in range(PEER_TOPK - 1 - t):
            vals[d] = jnp.where(win, vals[d + 1], vals[d])
            keys[d] = jnp.where(win, keys[d + 1], keys[d])
    return jnp.concatenate(out_v, axis=0), jnp.concatenate(out_k, axis=0)


def _top_candidates(v1, i1, v2, i2):
    tm = v1.shape[1]
    sub = lax.broadcasted_iota(I32, (SUBLANES, tm), 0)
    bc = lambda x, a: jnp.broadcast_to(x[a:a + 1, :], (SUBLANES, tm))
    vals = [jnp.where((a + 1) * (sub + 1) <= PEER_TOPK, bc(v1, a) + v2[0:SUBLANES, :], -jnp.inf)
            for a in range(PEER_TOPK)]
    eids = [bc(i1, a) * PEER_KEYS + i2[0:SUBLANES, :] for a in range(PEER_TOPK)]
    taken = jnp.zeros((SUBLANES, tm), I32)
    vals_hi = bc(v1, 0) + v2[SUBLANES:, :]
    eids_hi = bc(i1, 0) * PEER_KEYS + i2[SUBLANES:, :]
    keys_hi = SUBLANES + sub
    big = jnp.int32(2 ** 30)
    out_v, out_e = [], []
    for t in range(PEER_TOPK):
        keys = taken * PEER_TOPK + sub
        m = jnp.max(jnp.maximum(vals[0], vals_hi), axis=0, keepdims=True)
        kmin = jnp.min(jnp.minimum(jnp.where(vals[0] == m, keys, big), jnp.where(vals_hi == m, keys_hi, big)),
                       axis=0, keepdims=True)
        win = keys == kmin
        win_hi = keys_hi == kmin
        out_v.append(m)
        out_e.append(jnp.max(jnp.maximum(jnp.where(win, eids[0], -1), jnp.where(win_hi, eids_hi, -1)),
                             axis=0, keepdims=True))
        for d in range(PEER_TOPK - 1 - t):
            vals[d] = jnp.where(win, vals[d + 1], vals[d])
            eids[d] = jnp.where(win, eids[d + 1], eids[d])
        taken = jnp.where(win, taken + 1, taken)
        vals_hi = jnp.where(win_hi, -jnp.inf, vals_hi)
    return jnp.concatenate(out_v, axis=0), jnp.concatenate(out_e, axis=0)


_ROUTE_HEADS_PER_ITER = 8


def _route_kernel(h_ref, wq_ref, keys_ref, eid_ref, gate_ref, q_scr, eid_scr, gate_scr):
    tm = h_ref.shape[0]
    q = jnp.dot(h_ref[...].astype(BF16), wq_ref[...], preferred_element_type=F32)
    for c in range(2 * PEER_HEADS):
        q_scr[c] = q[:, c * PEER_HALF:(c + 1) * PEER_HALF]

    def body(it, carry):
        for off in range(_ROUTE_HEADS_PER_ITER):
            hh = it * _ROUTE_HEADS_PER_ITER + off
            tops = []
            for half in range(2):
                st = lax.dot_general(keys_ref[half], q_scr[2 * hh + half], (((1,), (1,)), ((), ())),
                                     precision=HI, preferred_element_type=F32)
                tops.append(_top_keys(st))
            (v1, i1), (v2, i2) = tops
            score, eids = _top_candidates(v1, i1, v2, i2)
            e = jnp.exp(score - jnp.max(score, axis=0, keepdims=True))
            r0 = pl.multiple_of(hh * PEER_TOPK, PEER_TOPK)
            gate_scr[pl.ds(r0, PEER_TOPK), :] = e / jnp.sum(e, axis=0, keepdims=True)
            eid_scr[pl.ds(r0, PEER_TOPK), :] = eids
        return carry

    lax.fori_loop(0, PEER_HEADS // _ROUTE_HEADS_PER_ITER, body, 0)
    gate_ref[...] = gate_scr[...].T
    eid_ref[...] = eid_scr[...].T


def _route(h, wq, keys):
    t, d = h.shape
    tm = TM_ROUTE
    nq = 2 * PEER_HEADS
    return pl.pallas_call(
        _route_kernel,
        grid=(t // tm,),
        in_specs=[pl.BlockSpec((tm, d), lambda i: (i, 0)),
                  pl.BlockSpec((d, nq * PEER_HALF), lambda i: (0, 0)),
                  pl.BlockSpec((2, PEER_KEYS, PEER_HALF), lambda i: (0, 0, 0))],
        out_specs=[pl.BlockSpec((tm, PEER_SEL), lambda i: (i, 0)),
                   pl.BlockSpec((tm, PEER_SEL), lambda i: (i, 0))],
        out_shape=[jax.ShapeDtypeStruct((t, PEER_SEL), I32),
                   jax.ShapeDtypeStruct((t, PEER_SEL), F32)],
        scratch_shapes=[pltpu.VMEM((nq, tm, PEER_HALF), F32),
                        pltpu.VMEM((PEER_SEL, tm), I32),
                        pltpu.VMEM((PEER_SEL, tm), F32)],
        compiler_params=_cparams(("parallel",), VMEM_LIMIT),
        name="peer_route",
    )(h, wq, keys)


def _peer_kernel(eid_cur_ref, eid_nxt_ref, x_ref, gate_ref, cmp_ref, exp_ref, uv_hbm, o_ref, buf, sem):
    step = pl.program_id(0)
    nsteps = pl.num_programs(0)
    g_tok = G_PEER
    rows = g_tok * PEER_SEL
    nsub = D_MODEL // LANES
    per_piece = rows // (2 * g_tok)

    def slot_wait(sl):
        pltpu.make_async_copy(uv_hbm.at[pl.ds(0, rows)], buf.at[sl], sem.at[sl]).wait()

    @pl.when(step == 0)
    def _():
        def body(r, carry):
            pltpu.make_async_copy(uv_hbm.at[eid_cur_ref[0, 0, r]], buf.at[0, r], sem.at[0]).start()
            return carry
        lax.fori_loop(0, rows, body, 0, unroll=8)

    diag = (lax.broadcasted_iota(I32, (nsub, PEER_SEL * nsub), 1) % nsub
            == lax.broadcasted_iota(I32, (nsub, PEER_SEL * nsub), 0))

    def group(sl, idx_ref, idx_off):
        tok0 = sl * g_tok

        def prefetch(r0, n):
            for i in range(n):
                r = r0 + i
                pltpu.make_async_copy(uv_hbm.at[idx_ref[0, 0, idx_off + r]], buf.at[1 - sl, r],
                                      sem.at[1 - sl]).start(priority=i % 2)

        def tiles(g):
            return buf[sl, g * PEER_SEL:(g + 1) * PEER_SEL].reshape(PEER_SEL * nsub, LANES)

        slot_wait(sl)
        parts = []
        for g in range(g_tok):
            u_rows = pltpu.bitcast(tiles(g) << 16, F32).astype(BF16)
            x_row = x_ref[tok0 + g:tok0 + g + 1, :]
            x_tile = jnp.concatenate([x_row[:, s * LANES:(s + 1) * LANES] for s in range(nsub)], axis=0)
            q = lax.dot_general(x_tile.astype(BF16), u_rows, (((1,), (1,)), ((), ())),
                                preferred_element_type=F32)
            parts.append(jnp.where(diag, q, 0.0))
            prefetch(g * per_piece, per_piece)
        masked = jnp.concatenate(parts, axis=0)
        m_hi = masked.astype(BF16)
        m_lo = (masked - m_hi.astype(F32)).astype(BF16)
        part = (jnp.dot(m_hi, cmp_ref[...], preferred_element_type=F32)
                + jnp.dot(m_lo, cmp_ref[...], preferred_element_type=F32))
        act = jnp.sum(part.reshape(g_tok, nsub, PEER_SEL), axis=1)
        wgt = gate_ref[tok0:tok0 + g_tok, :] * _gelu(act)
        wrep = jnp.dot(wgt.astype(BF16), exp_ref[...], preferred_element_type=F32)
        for g in range(g_tok):
            v_rows = pltpu.bitcast(tiles(g) & jnp.uint32(0xFFFF0000), F32).astype(BF16)
            wexp = jnp.where(diag, jnp.broadcast_to(wrep[g:g + 1, :], diag.shape), 0.0).astype(BF16)
            o_tile = jnp.dot(wexp, v_rows, preferred_element_type=F32)
            for s in range(nsub):
                o_ref[tok0 + g:tok0 + g + 1, s * LANES:(s + 1) * LANES] = o_tile[s:s + 1, :]
            prefetch((g_tok + g) * per_piece, per_piece)

    group(0, eid_cur_ref, rows)
    group(1, eid_nxt_ref, 0)

    @pl.when(step == nsteps - 1)
    def _():
        slot_wait(0)


def _peer(h, eid_tok, gate_tok, uv_packed):
    t, d = h.shape
    g_step = 2 * G_PEER
    nsteps = t // g_step
    rows = G_PEER * PEER_SEL
    nsub = d // LANES
    eid3 = eid_tok.reshape(nsteps, 1, 2 * rows)
    ncol = PEER_SEL * nsub
    col = jnp.arange(ncol) // nsub
    compress = (col[:, None] == jnp.arange(PEER_SEL)[None, :]).astype(F32)
    smem_blk = lambda f: pl.BlockSpec((1, 1, 2 * rows), f, memory_space=pltpu.SMEM)
    tile_blk = pl.BlockSpec((g_step, d), lambda i: (i, 0))
    return pl.pallas_call(
        _peer_kernel,
        grid=(nsteps,),
        in_specs=[smem_blk(lambda i: (i, 0, 0)),
                  smem_blk(lambda i: (jnp.minimum(i + 1, nsteps - 1), 0, 0)),
                  tile_blk,
                  pl.BlockSpec((g_step, PEER_SEL), lambda i: (i, 0)),
                  pl.BlockSpec((ncol, PEER_SEL), lambda i: (0, 0)),
                  pl.BlockSpec((PEER_SEL, ncol), lambda i: (0, 0)),
                  pl.BlockSpec(memory_space=pl.ANY)],
        out_specs=tile_blk,
        out_shape=jax.ShapeDtypeStruct((t, d), F32),
        scratch_shapes=[pltpu.VMEM((2, rows, nsub, LANES), U32), pltpu.SemaphoreType.DMA((2,))],
        compiler_params=_cparams(("arbitrary",), VMEM_LIMIT),
        name="peer_experts",
    )(eid3, eid3, h, gate_tok, compress.astype(BF16), compress.T.astype(BF16),
      uv_packed.reshape(-1, nsub, LANES))


_TR_PACK = 256


def _bf16_bits(x):
    bits = pltpu.bitcast(x, U32)
    return (bits + jnp.uint32(0x7FFF) + ((bits >> 16) & jnp.uint32(1))) >> 16


def _pack_kernel(u_ref, v_ref, o_ref):
    for s in range(D_MODEL // LANES):
        cols = slice(s * LANES, (s + 1) * LANES)
        o_ref[:, s, :] = _bf16_bits(u_ref[0, :, cols]) | (_bf16_bits(v_ref[0, :, cols]) << 16)


def _pack_tables(u_all, v_all, layer):
    _, n, d = u_all.shape
    tr = _TR_PACK
    src = pl.BlockSpec((1, tr, d), lambda i: (layer, i, 0))
    return pl.pallas_call(
        _pack_kernel,
        grid=(n // tr,),
        in_specs=[src, src],
        out_specs=pl.BlockSpec((tr, d // LANES, LANES), lambda i: (i, 0, 0)),
        out_shape=jax.ShapeDtypeStruct((n, d // LANES, LANES), U32),
        compiler_params=_cparams(("parallel",)),
        name="pack_tables",
    )(u_all, v_all)


def _out_kernel(h_ref, f_ref, p_ref, wp_ref, wgp_ref, g_ref, b_ref, o_ref):
    h = h_ref[...]
    ple = jnp.dot(p_ref[...].astype(BF16), wp_ref[...], preferred_element_type=F32)
    gate = _sigmoid(jnp.dot(h.astype(BF16), wgp_ref[...], preferred_element_type=F32))
    o_ref[...] = _ln(DN_ALPHA * h + f_ref[...] + ple * gate, g_ref[...], b_ref[...])


def _layer_out(h, ffn, p2, ple_w, ple_gate_w, ln_g, ln_b):
    t, d = h.shape
    tm = TM_OUT
    tok = lambda i: (i, 0)
    par = lambda i: (0, 0)
    return pl.pallas_call(
        _out_kernel,
        grid=(t // tm,),
        in_specs=[pl.BlockSpec((tm, d), tok), pl.BlockSpec((tm, d), tok),
                  pl.BlockSpec((tm, PLE_DIM), tok),
                  pl.BlockSpec((PLE_DIM, d), par), pl.BlockSpec((d, d), par),
                  pl.BlockSpec((1, d), par), pl.BlockSpec((1, d), par)],
        out_specs=pl.BlockSpec((tm, d), tok),
        out_shape=jax.ShapeDtypeStruct((t, d), F32),
        compiler_params=_cparams(("parallel",)),
        name="ple_ln2",
    )(h, ffn, p2, ple_w, ple_gate_w, ln_g.reshape(1, -1), ln_b.reshape(1, -1))


def _block_ones():
    head = jnp.arange(BRANCH) // HEAD_DIM
    return (head[:, None] == head[None, :]).astype(F32)


def _layer(h, p2, bsz, seq, layer, w_in_all, peer_u_all, peer_v_all, rwkv_mu, rwkv_w0, rwkv_w2, rwkv_a0, rwkv_a2,
           rwkv_g2, rwkv_kk, rwkv_ka, rwkv_rk, rwkv_lnx_g, rwkv_lnx_b, s5_lam_re, s5_lam_im, s5_b_re, s5_b_im,
           s5_c_re, s5_c_im, s5_d, s5_log_dt, s5_glu_w, s5_glu_b, fox_bf, conv_w, conv_b, conv_ln_g, conv_ln_b,
           w_branch, w_out, ln1_g, ln1_b, peer_wq, peer_k1, peer_k2, ple_w, ple_gate_w, ln2_g, ln2_b):
    t = bsz * seq
    d = D_MODEL
    w_packed, w_gate = _repack_w_in(w_in_all, layer)
    bd = _block_ones()
    wl = jnp.zeros((LANES, 3 * BRANCH), F32)
    wl = wl.at[0:32, 0:BRANCH].set(rwkv_w2.astype(F32))
    wl = wl.at[32:64, BRANCH:2 * BRANCH].set(rwkv_a2.astype(F32))
    wl = wl.at[64:128, 2 * BRANCH:].set(rwkv_g2.astype(F32))

    p_rwkv, p_s5_t, q_heads, k_heads, v_aug, p_ff, p_conv = _project(h, w_packed, bsz, seq)

    r, dec, k2, v, na, bb, g, bonus = _rwkv_pre(p_rwkv, rwkv_mu, rwkv_w0, rwkv_a0, rwkv_kk, rwkv_ka,
                                                 rwkv_rk.reshape(-1), wl, bd, bsz, seq)
    y_scan = _rwkv_scan(*(_to_scan(x, bsz, seq) for x in (na, dec, bb, k2, r, v)))
    y_rwkv = _rwkv_post(_from_scan(y_scan, bsz, seq), g.reshape(t, BRANCH), bonus.reshape(t, BRANCH),
                        rwkv_lnx_g, rwkv_lnx_b, bd)

    bre, bim, lre, lim, cc = _s5_params(s5_lam_re, s5_lam_im, s5_b_re, s5_b_im, s5_c_re, s5_c_im, s5_log_dt)
    y_s5_t = _s5(p_s5_t, bre, bim, lre, lim, cc, s5_d, s5_glu_w, s5_glu_b, bsz, seq)

    f_rows = p_ff[:, :HEADS].reshape(bsz, seq, HEADS).transpose(0, 2, 1).reshape(bsz * HEADS, seq)
    bias_rows = jnp.tile(fox_bf.astype(F32), bsz).reshape(bsz * HEADS, 1)
    c = _fox_cumsum(f_rows, bias_rows)
    y_fox = _fox_attn(q_heads, k_heads, v_aug,
                      c.reshape(bsz, HEADS, seq, 1), c.reshape(bsz, HEADS, seq // TK_ATT, TK_ATT))

    y_conv = _conv_mixer(p_conv, conv_w, conv_b, conv_ln_g, conv_ln_b, bsz, seq)

    h1 = _merge(h, y_rwkv, y_s5_t, y_fox, y_conv, w_gate, w_branch.astype(BF16),
                w_out.astype(BF16), ln1_g, ln1_b, bsz, seq)

    keys = jnp.stack([peer_k1, peer_k2]).astype(F32)
    eid_tok, gate_tok = _route(h1, peer_wq.astype(BF16), keys)
    ffn = _peer(h1, eid_tok, gate_tok, _pack_tables(peer_u_all, peer_v_all, layer))

    return _layer_out(h1, ffn, p2, ple_w.astype(BF16), ple_gate_w.astype(BF16), ln2_g, ln2_b)


def kernel(x, p, ln_in_g, ln_in_b, w_in, rwkv_mu, rwkv_w0, rwkv_w2, rwkv_a0, rwkv_a2, rwkv_g2, rwkv_kk, rwkv_ka, rwkv_rk, rwkv_lnx_g, rwkv_lnx_b, s5_lam_re, s5_lam_im, s5_b_re, s5_b_im, s5_c_re, s5_c_im, s5_d, s5_log_dt, s5_glu_w, s5_glu_b, fox_bf, conv_w, conv_b, conv_ln_g, conv_ln_b, w_branch, w_out, ln1_g, ln1_b, peer_wq, peer_k1, peer_k2, peer_u, peer_v, ple_w, ple_gate_w, ln2_g, ln2_b):
    bsz, seq, d = x.shape
    t = bsz * seq
    h = _layer_norm(x.reshape(t, d), ln_in_g, ln_in_b)
    per_layer = (rwkv_mu, rwkv_w0, rwkv_w2, rwkv_a0, rwkv_a2, rwkv_g2, rwkv_kk, rwkv_ka, rwkv_rk,
                 rwkv_lnx_g, rwkv_lnx_b, s5_lam_re, s5_lam_im, s5_b_re, s5_b_im, s5_c_re, s5_c_im, s5_d,
                 s5_log_dt, s5_glu_w, s5_glu_b, fox_bf, conv_w, conv_b, conv_ln_g, conv_ln_b, w_branch,
                 w_out, ln1_g, ln1_b, peer_wq, peer_k1, peer_k2, ple_w, ple_gate_w, ln2_g, ln2_b)
    for i in range(p.shape[0]):
        h = _layer(h, p[i].reshape(t, PLE_DIM), bsz, seq, i, w_in, peer_u, peer_v, *(w[i] for w in per_layer))
    return h.reshape(bsz, seq, d)
```

```python
import functools
import math

import jax
import jax.numpy as jnp
from jax import lax
from jax.experimental import pallas as pl
from jax.experimental.pallas import tpu as pltpu

F32 = jnp.float32
BF16 = jnp.bfloat16
I32 = jnp.int32
U32 = jnp.uint32
HI = lax.Precision.HIGHEST

D_MODEL = 1024
BRANCH = 256
HEADS = 4
HEAD_DIM = 64
RWKV_COLS = 896
S5_GROUPS = 16
S5_GROUP = 16
S5_STATE = 64
S5_N = S5_GROUPS * S5_STATE
CONV_WIDTH = 31
CONV_HALO = 32
PEER_HEADS = 8
PEER_KEYS = 128
PEER_HALF = 128
PEER_TOPK = 16
PEER_SEL = PEER_HEADS * PEER_TOPK
PLE_DIM = 256
RWKV_GN_EPS = 64e-5
LN_EPS = 1e-5
DEPTH = 2
DN_ALPHA = (2 * DEPTH) ** 0.25

SUBLANES = 8
LANES = 128
VMEM_LIMIT = 56 * 1024 * 1024

TM_PROJ = 512
TQ_PRE = 256
TC_SCAN = 32
TC_S5 = 32
TQ_ATT = 256
TK_ATT = 256
TC_CONV = 512
TM_MERGE = 256
TM_ROUTE = 128
G_PEER = 16
PEER_SLOTS = 4
PEER_AHEAD = 2
TM_OUT = 256


def _cparams(sem, vmem=None):
    return pltpu.CompilerParams(dimension_semantics=sem, vmem_limit_bytes=vmem)


def _ln(z, g, b):
    mu = jnp.mean(z, axis=-1, keepdims=True)
    zc = z - mu
    var = jnp.mean(zc * zc, axis=-1, keepdims=True)
    return zc * lax.rsqrt(var + LN_EPS) * g + b


def _gelu(y):
    return 0.5 * y * (1.0 + lax.erf(y * (1.0 / math.sqrt(2.0))))


def _sigmoid(y):
    return 1.0 / (1.0 + jnp.exp(-y))


def _softplus(y):
    return jnp.maximum(y, 0.0) + jnp.log(1.0 + jnp.exp(-jnp.abs(y)))


def _ln_kernel(x_ref, g_ref, b_ref, o_ref):
    o_ref[...] = _ln(x_ref[...], g_ref[...], b_ref[...])


def _layer_norm(x2, g, b):
    t, d = x2.shape
    tm = TM_PROJ
    return pl.pallas_call(
        _ln_kernel,
        grid=(t // tm,),
        in_specs=[pl.BlockSpec((tm, d), lambda i: (i, 0)),
                  pl.BlockSpec((1, d), lambda i: (0, 0)),
                  pl.BlockSpec((1, d), lambda i: (0, 0))],
        out_specs=pl.BlockSpec((tm, d), lambda i: (i, 0)),
        out_shape=jax.ShapeDtypeStruct((t, d), F32),
        compiler_params=_cparams(("parallel",)),
        name="ln_in",
    )(x2, g.reshape(1, d), b.reshape(1, d))


_P_RWKV = (0, 896)
_P_S5 = (896, 1152)
_P_FQKV = (1152, 1920)
_P_FF = (1920, 2048)
_P_CONV = (2048, 2560)
_P_COLS = 2560


_N_FRONT = RWKV_COLS + BRANCH + 3 * BRANCH + HEADS
_TR_REPACK = 128


def _repack_kernel(w_ref, wp_ref, wg_ref):
    w = w_ref[0]
    lo, hi = _P_FF
    wp_ref[:, 0:lo] = w[:, 0:lo].astype(BF16)
    lane = lax.broadcasted_iota(I32, (w.shape[0], hi - lo), 1)
    wp_ref[:, lo:hi] = jnp.where(lane < HEADS, w[:, lo:hi], 0.0).astype(BF16)
    wp_ref[:, hi:_P_COLS] = w[:, _N_FRONT:_N_FRONT + 2 * BRANCH].astype(BF16)
    wg_ref[...] = w[:, _N_FRONT + 2 * BRANCH:].astype(BF16)


def _repack_w_in(w_in_all, layer):
    _, d, n = w_in_all.shape
    n_gate = n - _N_FRONT - 2 * BRANCH
    tr = _TR_REPACK
    return pl.pallas_call(
        _repack_kernel,
        grid=(d // tr,),
        in_specs=[pl.BlockSpec((1, tr, n), lambda i: (layer, i, 0))],
        out_specs=[pl.BlockSpec((tr, _P_COLS), lambda i: (i, 0)),
                   pl.BlockSpec((tr, n_gate), lambda i: (i, 0))],
        out_shape=[jax.ShapeDtypeStruct((d, _P_COLS), BF16),
                   jax.ShapeDtypeStruct((d, n_gate), BF16)],
        compiler_params=_cparams(("parallel",)),
        name="repack_w_in",
    )(w_in_all)


def _proj_kernel(h_ref, w_ref, rw_ref, s5_ref, q_ref, k_ref, v_ref, ff_ref, cv_ref):
    x = h_ref[...].astype(BF16)
    for (lo, hi), o_ref in ((_P_RWKV, rw_ref), (_P_S5, s5_ref), (_P_FF, ff_ref), (_P_CONV, cv_ref)):
        o_ref[...] = jnp.dot(x, w_ref[:, lo:hi], preferred_element_type=F32)
    qkv = jnp.dot(x, w_ref[:, _P_FQKV[0]:_P_FQKV[1]], preferred_element_type=F32)
    tm = qkv.shape[0]
    ones_col = (lax.broadcasted_iota(I32, (tm, LANES - HEAD_DIM), 1) == 0).astype(F32)
    for hh in range(HEADS):
        q_ref[0, hh] = qkv[:, hh * HEAD_DIM:(hh + 1) * HEAD_DIM]
        k_ref[0, hh] = qkv[:, BRANCH + hh * HEAD_DIM:BRANCH + (hh + 1) * HEAD_DIM]
        v_ref[0, hh] = jnp.concatenate(
            [qkv[:, 2 * BRANCH + hh * HEAD_DIM:2 * BRANCH + (hh + 1) * HEAD_DIM], ones_col], axis=1)


def _project(h, w_packed, bsz, seq):
    t, d = h.shape
    tm = TM_PROJ
    nst = seq // tm
    w_rwkv, w_s5, w_ff, w_conv = [hi - lo for lo, hi in (_P_RWKV, _P_S5, _P_FF, _P_CONV)]
    tok = lambda i: (i, 0)
    heads = lambda i: (i // nst, 0, i % nst, 0)
    out_shapes = [jax.ShapeDtypeStruct((t, w_rwkv), F32),
                  jax.ShapeDtypeStruct((seq, bsz * w_s5), F32),
                  jax.ShapeDtypeStruct((bsz, HEADS, seq, HEAD_DIM), F32),
                  jax.ShapeDtypeStruct((bsz, HEADS, seq, HEAD_DIM), F32),
                  jax.ShapeDtypeStruct((bsz, HEADS, seq, LANES), F32),
                  jax.ShapeDtypeStruct((t, w_ff), F32),
                  jax.ShapeDtypeStruct((t, w_conv), F32)]
    out_specs = [pl.BlockSpec((tm, w_rwkv), tok),
                 pl.BlockSpec((tm, w_s5), lambda i: (i % nst, i // nst)),
                 pl.BlockSpec((1, HEADS, tm, HEAD_DIM), heads),
                 pl.BlockSpec((1, HEADS, tm, HEAD_DIM), heads),
                 pl.BlockSpec((1, HEADS, tm, LANES), heads),
                 pl.BlockSpec((tm, w_ff), tok),
                 pl.BlockSpec((tm, w_conv), tok)]
    return pl.pallas_call(
        _proj_kernel,
        grid=(t // tm,),
        in_specs=[pl.BlockSpec((tm, d), tok),
                  pl.BlockSpec((d, _P_COLS), lambda i: (0, 0))],
        out_specs=out_specs,
        out_shape=out_shapes,
        compiler_params=_cparams(("parallel",), VMEM_LIMIT),
        name="in_proj",
    )(h, w_packed)


def _rwkv_pre_kernel(z_ref, zp_ref, mu_ref, w0_ref, a0_ref, kk_ref, ka_ref, rk_ref, wl_ref, bd_ref,
                     r_out, w_out, k_out, v_out, na_out, b_out, g_out, bv_out):
    j = pl.program_id(1)
    z = z_ref[0]
    prev = zp_ref[0][SUBLANES - 1:SUBLANES, :]
    prev = jnp.where(j == 0, 0.0, prev)
    row = lax.broadcasted_iota(I32, z.shape, 0)
    zs = jnp.where(row == 0, prev, pltpu.roll(z, 1, 0))
    z = z + (zs - z) * mu_ref[...]
    r = z[:, 0:256]
    k = z[:, 256:512]
    v = z[:, 512:768]
    zc = z[:, 768:896]
    lane = lax.broadcasted_iota(I32, zc.shape, 1)
    act = jnp.where(lane < 32, jnp.tanh(zc), jnp.where(lane < 64, zc, _sigmoid(zc)))
    lo = jnp.dot(act, wl_ref[...], precision=HI, preferred_element_type=F32)
    dw = lo[:, 0:256]
    da = lo[:, 256:512]
    g = lo[:, 512:768]
    w_log = -_softplus(-(w0_ref[...] + dw)) - 0.5
    decay = jnp.exp(-jnp.exp(w_log))
    a = _sigmoid(a0_ref[...] + da)
    kkv = k * kk_ref[...]
    bd = bd_ref[...]
    ss = jnp.dot(kkv * kkv, bd, precision=HI, preferred_element_type=F32)
    kkn = kkv * lax.rsqrt(jnp.maximum(ss, 1e-24))
    k2 = k * (1.0 + (a - 1.0) * ka_ref[...])
    bonus = jnp.dot(r * k2 * rk_ref[...], bd, precision=HI, preferred_element_type=F32) * v
    r_out[0] = r
    w_out[0] = decay
    k_out[0] = k2
    v_out[0] = v
    na_out[0] = -kkn
    b_out[0] = kkn * a
    g_out[0] = g
    bv_out[0] = bonus


def _rwkv_pre(p_rwkv, mu, w0, a0, kk, ka, rk, wl, bd, bsz, seq):
    z3 = p_rwkv.reshape(bsz, seq, RWKV_COLS)
    tq = TQ_PRE
    blk = lambda b, j: (b, j, 0)
    par = lambda b, j: (0, 0)
    out_shape = [jax.ShapeDtypeStruct((bsz, seq, BRANCH), F32)] * 8
    out_specs = [pl.BlockSpec((1, tq, BRANCH), blk)] * 8
    return pl.pallas_call(
        _rwkv_pre_kernel,
        grid=(bsz, seq // tq),
        in_specs=[pl.BlockSpec((1, tq, RWKV_COLS), blk),
                  pl.BlockSpec((1, SUBLANES, RWKV_COLS),
                               lambda b, j: (b, jnp.maximum(j * (tq // SUBLANES) - 1, 0), 0)),
                  pl.BlockSpec((1, RWKV_COLS), par),
                  pl.BlockSpec((1, BRANCH), par), pl.BlockSpec((1, BRANCH), par),
                  pl.BlockSpec((1, BRANCH), par), pl.BlockSpec((1, BRANCH), par),
                  pl.BlockSpec((1, BRANCH), par),
                  pl.BlockSpec((LANES, 3 * BRANCH), par),
                  pl.BlockSpec((BRANCH, BRANCH), par)],
        out_specs=out_specs,
        out_shape=out_shape,
        compiler_params=_cparams(("parallel", "parallel")),
        name="rwkv_pre",
    )(z3, z3, mu.reshape(1, -1), w0.reshape(1, -1), a0.reshape(1, -1), kk.reshape(1, -1),
      ka.reshape(1, -1), rk.reshape(1, -1), wl, bd)


_SCAN_IG = HEAD_DIM // 2 // SUBLANES
_SCAN_HALF = HEAD_DIM // 2


def _rwkv_scan_kernel(a_ref, w_ref, b_ref, k_ref, r_ref, v_ref, y_ref, s_ref, a_scr, w_scr, b_scr, k_scr, r_scr):
    @pl.when(pl.program_id(0) == 0)
    def _():
        s_ref[...] = jnp.zeros_like(s_ref)

    tc = a_ref.shape[0]
    tile = (SUBLANES, LANES)

    lane = lax.broadcasted_iota(I32, (tc * _SCAN_HALF, LANES), 1)
    for src_ref, dst_ref in ((a_ref, a_scr), (w_ref, w_scr), (b_ref, b_scr), (k_ref, k_scr), (r_ref, r_scr)):
        x = src_ref[...].reshape(tc * _SCAN_HALF, LANES)
        xr = pltpu.roll(x, LANES // 2, 1)
        dst_ref[:, 0:_SCAN_HALF, :] = jnp.where(lane < LANES // 2, x, xr).reshape(tc, _SCAN_HALF, LANES)
        dst_ref[:, _SCAN_HALF:, :] = jnp.where(lane < LANES // 2, xr, x).reshape(tc, _SCAN_HALF, LANES)

    def step(t, carry):
        vv = [v_ref[t, ig * SUBLANES:(ig + 1) * SUBLANES, :] for ig in range(_SCAN_IG)]
        sa = [[jnp.zeros(tile, F32), jnp.zeros(tile, F32)] for _ in range(_SCAN_IG)]
        for j in range(HEAD_DIM):
            ab = jnp.broadcast_to(a_scr[t, j:j + 1, :], tile)
            for ig in range(_SCAN_IG):
                sa[ig][j % 2] = sa[ig][j % 2] + s_ref[ig, j] * ab
        sa = [x[0] + x[1] for x in sa]
        yy = [[jnp.zeros(tile, F32), jnp.zeros(tile, F32)] for _ in range(_SCAN_IG)]
        for j in range(HEAD_DIM):
            wb = jnp.broadcast_to(w_scr[t, j:j + 1, :], tile)
            bb = jnp.broadcast_to(b_scr[t, j:j + 1, :], tile)
            kb = jnp.broadcast_to(k_scr[t, j:j + 1, :], tile)
            rb = jnp.broadcast_to(r_scr[t, j:j + 1, :], tile)
            for ig in range(_SCAN_IG):
                s = s_ref[ig, j] * wb + sa[ig] * bb + vv[ig] * kb
                s_ref[ig, j] = s
                yy[ig][j % 2] = yy[ig][j % 2] + s * rb
        for ig in range(_SCAN_IG):
            y_ref[t, ig * SUBLANES:(ig + 1) * SUBLANES, :] = yy[ig][0] + yy[ig][1]
        return carry

    lax.fori_loop(0, tc, step, 0)


def _rwkv_scan(a_t, w_t, b_t, k_t, r_t, v_t):
    seq = a_t.shape[0]
    tc = TC_SCAN
    spec = pl.BlockSpec((tc, _SCAN_HALF, LANES), lambda i: (i, 0, 0))
    return pl.pallas_call(
        _rwkv_scan_kernel,
        grid=(seq // tc,),
        in_specs=[spec] * 6,
        out_specs=spec,
        out_shape=jax.ShapeDtypeStruct((seq, _SCAN_HALF, LANES), F32),
        scratch_shapes=[pltpu.VMEM((_SCAN_IG, HEAD_DIM, SUBLANES, LANES), F32)]
                       + [pltpu.VMEM((tc, HEAD_DIM, LANES), F32)] * 5,
        compiler_params=_cparams(("arbitrary",)),
        name="rwkv_scan",
    )(a_t, w_t, b_t, k_t, r_t, v_t)


def _to_scan(x, bsz, seq):
    n_pairs = bsz * HEADS
    xt = x.reshape(bsz, seq, HEADS, 2, HEAD_DIM // 2).transpose(1, 4, 3, 0, 2)
    xt = xt.reshape(seq, HEAD_DIM // 2, 2, n_pairs)
    pad = LANES // 2 - n_pairs
    if pad:
        xt = jnp.pad(xt, ((0, 0), (0, 0), (0, 0), (0, pad)))
    return xt.reshape(seq, HEAD_DIM // 2, LANES)


def _from_scan(y, bsz, seq):
    n_pairs = bsz * HEADS
    yt = y.reshape(seq, HEAD_DIM // 2, 2, LANES // 2)[..., :n_pairs]
    yt = yt.reshape(seq, HEAD_DIM // 2, 2, bsz, HEADS).transpose(3, 0, 4, 2, 1)
    return yt.reshape(bsz * seq, BRANCH)


def _rwkv_post_kernel(y_ref, g_ref, bv_ref, gg_ref, gb_ref, bd_ref, o_ref):
    y = y_ref[...]
    bd = bd_ref[...]
    inv = 1.0 / HEAD_DIM
    mean = jnp.dot(y, bd, precision=HI, preferred_element_type=F32) * inv
    yc = y - mean
    var = jnp.dot(yc * yc, bd, precision=HI, preferred_element_type=F32) * inv
    yn = yc * lax.rsqrt(var + RWKV_GN_EPS) * gg_ref[...] + gb_ref[...]
    o_ref[...] = (yn + bv_ref[...]) * g_ref[...]


def _rwkv_post(y, g, bv, gn_g, gn_b, bd):
    t = y.shape[0]
    tm = TM_PROJ
    tok = lambda i: (i, 0)
    par = lambda i: (0, 0)
    return pl.pallas_call(
        _rwkv_post_kernel,
        grid=(t // tm,),
        in_specs=[pl.BlockSpec((tm, BRANCH), tok)] * 3
                 + [pl.BlockSpec((1, BRANCH), par)] * 2 + [pl.BlockSpec((BRANCH, BRANCH), par)],
        out_specs=pl.BlockSpec((tm, BRANCH), tok),
        out_shape=jax.ShapeDtypeStruct((t, BRANCH), F32),
        compiler_params=_cparams(("parallel",)),
        name="rwkv_post",
    )(y, g, bv, gn_g.reshape(1, -1), gn_b.reshape(1, -1), bd)


def _s5_kernel(u_ref, bre_ref, bim_ref, lre_ref, lim_ref, cc_ref, d_ref, gw_ref, gb_ref, o_ref,
               xre_ref, xim_ref, bure_ref, buim_ref, xs_ref):
    @pl.when(pl.program_id(0) == 0)
    def _():
        xre_ref[...] = jnp.zeros_like(xre_ref)
        xim_ref[...] = jnp.zeros_like(xim_ref)

    tc, nb, w = u_ref.shape
    u2 = u_ref[...].reshape(tc * nb, w)
    ub = u2.astype(BF16)
    bure_ref[...] = jnp.dot(ub, bre_ref[...], preferred_element_type=F32)
    buim_ref[...] = jnp.dot(ub, bim_ref[...], preferred_element_type=F32)
    lre = jnp.broadcast_to(lre_ref[...], (nb, S5_N))
    lim = jnp.broadcast_to(lim_ref[...], (nb, S5_N))

    def step(t, carry):
        xr, xi = carry
        r0 = pl.multiple_of(t * nb, nb)
        nr = lre * xr - lim * xi + bure_ref[pl.ds(r0, nb), :]
        ni = lre * xi + lim * xr + buim_ref[pl.ds(r0, nb), :]
        xs_ref[pl.ds(r0, nb), 0:S5_N] = nr
        xs_ref[pl.ds(r0, nb), S5_N:2 * S5_N] = ni
        return nr, ni

    xr, xi = lax.fori_loop(0, tc, step, (xre_ref[...], xim_ref[...]))
    xre_ref[...] = xr
    xim_ref[...] = xi
    y = jnp.dot(xs_ref[...].astype(BF16), cc_ref[...], preferred_element_type=F32) + d_ref[...] * u2
    y = _gelu(y)
    gate = jnp.dot(y.astype(BF16), gw_ref[...], preferred_element_type=F32) + gb_ref[...]
    o_ref[...] = (y * _sigmoid(gate)).reshape(tc, nb, w)


def _s5(u_t, bre, bim, lre, lim, cc, d_skip, glu_w, glu_b, bsz, seq):
    u3 = u_t.reshape(seq, bsz, BRANCH)
    tc = TC_S5
    par = lambda i: (0, 0)
    blk = pl.BlockSpec((tc, bsz, BRANCH), lambda i: (i, 0, 0))
    out = pl.pallas_call(
        _s5_kernel,
        grid=(seq // tc,),
        in_specs=[blk,
                  pl.BlockSpec((BRANCH, S5_N), par), pl.BlockSpec((BRANCH, S5_N), par),
                  pl.BlockSpec((1, S5_N), par), pl.BlockSpec((1, S5_N), par),
                  pl.BlockSpec((2 * S5_N, BRANCH), par),
                  pl.BlockSpec((1, BRANCH), par),
                  pl.BlockSpec((BRANCH, BRANCH), par),
                  pl.BlockSpec((1, BRANCH), par)],
        out_specs=blk,
        out_shape=jax.ShapeDtypeStruct((seq, bsz, BRANCH), F32),
        scratch_shapes=[pltpu.VMEM((bsz, S5_N), F32), pltpu.VMEM((bsz, S5_N), F32),
                        pltpu.VMEM((tc * bsz, S5_N), F32), pltpu.VMEM((tc * bsz, S5_N), F32),
                        pltpu.VMEM((tc * bsz, 2 * S5_N), F32)],
        compiler_params=_cparams(("arbitrary",), VMEM_LIMIT),
        name="s5_scan",
    )(u3, bre.astype(BF16), bim.astype(BF16), lre, lim, cc.astype(BF16), d_skip.reshape(1, -1),
      glu_w.astype(BF16), glu_b.reshape(1, -1))
    return out.reshape(seq, bsz * BRANCH)


def _s5_params(lam_re, lam_im, b_re, b_im, c_re, c_im, log_dt):
    lam = lax.complex(lam_re.astype(F32), lam_im.astype(F32))
    dt = jnp.exp(log_dt.astype(F32))[:, None]
    lam_bar = jnp.exp(lam * dt)
    b_bar = ((lam_bar - 1.0) / lam)[..., None] * lax.complex(b_re.astype(F32), b_im.astype(F32))
    eye = jnp.eye(S5_GROUPS, dtype=F32)
    bre = jnp.einsum('gpc,gh->gchp', jnp.real(b_bar), eye).reshape(BRANCH, S5_N)
    bim = jnp.einsum('gpc,gh->gchp', jnp.imag(b_bar), eye).reshape(BRANCH, S5_N)
    cre = jnp.einsum('gcp,gh->gphc', c_re.astype(F32), eye).reshape(S5_N, BRANCH)
    cim = jnp.einsum('gcp,gh->gphc', c_im.astype(F32), eye).reshape(S5_N, BRANCH)
    cc = jnp.concatenate([cre, -cim], axis=0)
    return bre, bim, jnp.real(lam_bar).reshape(1, S5_N), jnp.imag(lam_bar).reshape(1, S5_N), cc


def _fox_cumsum_kernel(f_ref, bias_ref, c_ref):
    x = f_ref[...] + bias_ref[...]
    c = -_softplus(-x)
    n = c.shape[1]
    lane = lax.broadcasted_iota(I32, c.shape, 1)
    sh = 1
    while sh < n:
        c = c + jnp.where(lane >= sh, pltpu.roll(c, sh, 1), 0.0)
        sh *= 2
    c_ref[...] = c


def _fox_cumsum(f_rows, bias_rows):
    rows, seq = f_rows.shape
    return pl.pallas_call(
        _fox_cumsum_kernel,
        out_shape=jax.ShapeDtypeStruct((rows, seq), F32),
        name="fox_cumsum",
    )(f_rows, bias_rows)


def _fox_attn_kernel(q_ref, k_ref, v_ref, cq_ref, ck_ref, o_ref):
    qi = pl.program_id(1)
    nh, tq = q_ref.shape[1], q_ref.shape[2]
    tk = ck_ref.shape[3]
    qs = [(q_ref[0, h] * (HEAD_DIM ** -0.5)).astype(BF16) for h in range(nh)]
    cqs = [cq_ref[0, h] for h in range(nh)]
    qpos = qi * tq + lax.broadcasted_iota(I32, (tq, tk), 0)
    nkb = (qi * tq + tq + tk - 1) // tk

    def body(kb, carry):
        k0 = pl.multiple_of(kb * tk, tk)
        causal = k0 + lax.broadcasted_iota(I32, (tq, tk), 1) <= qpos
        out = []
        for h in range(nh):
            m, acc = carry[h]
            kblk = k_ref[0, h, pl.ds(k0, tk), :].astype(BF16)
            vblk = v_ref[0, h, pl.ds(k0, tk), :].astype(BF16)
            s = lax.dot_general(qs[h], kblk, (((1,), (1,)), ((), ())), preferred_element_type=F32)
            s = s + cqs[h] - ck_ref[0, h, pl.ds(kb, 1), :]
            s = jnp.where(causal, s, -1e30)
            m_new = jnp.maximum(m, jnp.max(s, axis=-1, keepdims=True))
            p = jnp.exp(s - m_new)
            acc = jnp.exp(m - m_new) * acc + jnp.dot(p.astype(BF16), vblk, preferred_element_type=F32)
            out.append((m_new, acc))
        return tuple(out)

    init = tuple((jnp.full((tq, 1), -1e30, F32), jnp.zeros((tq, v_ref.shape[3]), F32)) for _ in range(nh))
    res = lax.fori_loop(0, nkb, body, init)
    for h in range(nh):
        acc = res[h][1]
        o_ref[0, h] = acc[:, 0:HEAD_DIM] / acc[:, HEAD_DIM:HEAD_DIM + 1]


def _fox_attn(q, k, v, c_col, c_rowb):
    bsz, nh, seq, hd = q.shape
    tq = TQ_ATT
    nkb, tk = c_rowb.shape[2], c_rowb.shape[3]
    full = lambda b, i: (b, 0, 0, 0)
    tile = lambda b, i: (b, 0, i, 0)
    return pl.pallas_call(
        _fox_attn_kernel,
        grid=(bsz, seq // tq),
        in_specs=[pl.BlockSpec((1, nh, tq, hd), tile),
                  pl.BlockSpec((1, nh, seq, hd), full),
                  pl.BlockSpec((1, nh, seq, v.shape[3]), full),
                  pl.BlockSpec((1, nh, tq, 1), tile),
                  pl.BlockSpec((1, nh, nkb, tk), full)],
        out_specs=pl.BlockSpec((1, nh, tq, hd), tile),
        out_shape=jax.ShapeDtypeStruct((bsz, nh, seq, hd), F32),
        compiler_params=_cparams(("parallel", "parallel"), VMEM_LIMIT),
        name="fox_attn",
    )(q, k, v, c_col, c_rowb)


def _conv_kernel(x_ref, xp_ref, w_ref, b_ref, g_ref, be_ref, o_ref, buf_ref):
    j = pl.program_id(1)
    tc = x_ref.shape[1]
    x = x_ref[0]
    buf_ref[CONV_HALO:CONV_HALO + tc, :] = x[:, 0:BRANCH] * _sigmoid(x[:, BRANCH:2 * BRANCH])
    xp = xp_ref[0]
    hp = xp[:, 0:BRANCH] * _sigmoid(xp[:, BRANCH:2 * BRANCH])
    buf_ref[0:CONV_HALO, :] = jnp.where(j == 0, 0.0, hp)
    acc = jnp.zeros((tc, BRANCH), F32) + b_ref[...]
    off = CONV_HALO - (CONV_WIDTH - 1)
    for kk in range(CONV_WIDTH):
        acc = acc + buf_ref[off + kk:off + kk + tc, :] * w_ref[kk:kk + 1, :]
    y = _ln(acc, g_ref[...], be_ref[...])
    o_ref[0] = y * _sigmoid(y)


def _conv_mixer(p_conv, conv_w, conv_b, ln_g, ln_b, bsz, seq):
    x3 = p_conv.reshape(bsz, seq, 2 * BRANCH)
    tc = TC_CONV
    par = lambda b, j: (0, 0)
    out = pl.pallas_call(
        _conv_kernel,
        grid=(bsz, seq // tc),
        in_specs=[pl.BlockSpec((1, tc, 2 * BRANCH), lambda b, j: (b, j, 0)),
                  pl.BlockSpec((1, CONV_HALO, 2 * BRANCH),
                               lambda b, j: (b, jnp.maximum(j * (tc // CONV_HALO) - 1, 0), 0)),
                  pl.BlockSpec((CONV_WIDTH, BRANCH), par),
                  pl.BlockSpec((1, BRANCH), par), pl.BlockSpec((1, BRANCH), par),
                  pl.BlockSpec((1, BRANCH), par)],
        out_specs=pl.BlockSpec((1, tc, BRANCH), lambda b, j: (b, j, 0)),
        out_shape=jax.ShapeDtypeStruct((bsz, seq, BRANCH), F32),
        scratch_shapes=[pltpu.VMEM((CONV_HALO + tc, BRANCH), F32)],
        compiler_params=_cparams(("parallel", "parallel")),
        name="conv_mixer",
    )(x3, x3, conv_w, conv_b.reshape(1, -1), ln_g.reshape(1, -1), ln_b.reshape(1, -1))
    return out.reshape(bsz * seq, BRANCH)


def _merge_kernel(h_ref, y0_ref, y1_ref, y2_ref, y3_ref, wg_ref, wb_ref, wo_ref, g_ref, b_ref, o_ref):
    h = h_ref[...]
    hb = h.astype(BF16)
    merged = None
    for br, y_ref in enumerate((y0_ref, y1_ref, y2_ref, y3_ref)):
        gate = _sigmoid(jnp.dot(hb, wg_ref[:, br * D_MODEL:(br + 1) * D_MODEL], preferred_element_type=F32))
        if br == 2:
            proj = None
            for hh in range(HEADS):
                part = jnp.dot(y_ref[0, hh].astype(BF16), wb_ref[br, hh * HEAD_DIM:(hh + 1) * HEAD_DIM, :],
                               preferred_element_type=F32)
                proj = part if proj is None else proj + part
        else:
            proj = jnp.dot(y_ref[...].astype(BF16), wb_ref[br], preferred_element_type=F32)
        term = gate * proj
        merged = term if merged is None else merged + term
    mix = jnp.dot(merged.astype(BF16), wo_ref[...], preferred_element_type=F32)
    o_ref[...] = _ln(DN_ALPHA * h + mix, g_ref[...], b_ref[...])


def _merge(h, y_rwkv, y_s5_t, y_fox, y_conv, wg, wb, wo, ln_g, ln_b, bsz, seq):
    t, d = h.shape
    tm = TM_MERGE
    nst = seq // tm
    tok = lambda i: (i, 0)
    par2 = lambda i: (0, 0)
    br_spec = pl.BlockSpec((tm, BRANCH), tok)
    return pl.pallas_call(
        _merge_kernel,
        grid=(t // tm,),
        in_specs=[pl.BlockSpec((tm, d), tok),
                  br_spec,
                  pl.BlockSpec((tm, BRANCH), lambda i: (i % nst, i // nst)),
                  pl.BlockSpec((1, HEADS, tm, HEAD_DIM), lambda i: (i // nst, 0, i % nst, 0)),
                  br_spec,
                  pl.BlockSpec((d, 4 * d), par2),
                  pl.BlockSpec((4, BRANCH, d), lambda i: (0, 0, 0)),
                  pl.BlockSpec((d, d), par2),
                  pl.BlockSpec((1, d), par2), pl.BlockSpec((1, d), par2)],
        out_specs=pl.BlockSpec((tm, d), tok),
        out_shape=jax.ShapeDtypeStruct((t, d), F32),
        compiler_params=_cparams(("parallel",), VMEM_LIMIT),
        name="merge_ln1",
    )(h, y_rwkv, y_s5_t, y_fox, y_conv, wg, wb, wo, ln_g.reshape(1, -1), ln_b.reshape(1, -1))


def _sort_network(n):
    pairs = []
    p = 1
    while p < n:
        k = p
        while k >= 1:
            for j in range(k % p, n - k, 2 * k):
                for i in range(min(k, n - j - k)):
                    if (i + j) // (2 * p) == (i + j + k) // (2 * p):
                        pairs.append((i + j, i + j + k))
            k //= 2
        p *= 2
    return pairs


_KEY_SORT_PAIRS = _sort_network(PEER_KEYS // SUBLANES)


def _top_keys(st):
    nv = PEER_KEYS // SUBLANES
    tm = st.shape[1]
    sub = lax.broadcasted_iota(I32, (SUBLANES, tm), 0)
    vals = [st[k * SUBLANES:(k + 1) * SUBLANES, :] for k in range(nv)]
    keys = [k * SUBLANES + sub for k in range(nv)]
    for i, j in _KEY_SORT_PAIRS:
        first = (vals[i] > vals[j]) | ((vals[i] == vals[j]) & (keys[i] < keys[j]))
        vals[i], vals[j] = jnp.where(first, vals[i], vals[j]), jnp.where(first, vals[j], vals[i])
        keys[i], keys[j] = jnp.where(first, keys[i], keys[j]), jnp.where(first, keys[j], keys[i])
    out_v, out_k = [], []
    for t in range(PEER_TOPK):
        m = jnp.max(vals[0], axis=0, keepdims=True)
        kmin = jnp.min(jnp.where(vals[0] == m, keys[0], jnp.int32(2 ** 30)), axis=0, keepdims=True)
        out_v.append(m)
        out_k.append(kmin)
        win = keys[0] == kmin
        for d in range(PEER_TOPK - 1 - t):
            vals[d] = jnp.where(win, vals[d + 1], vals[d])
            keys[d] = jnp.where(win, keys[d + 1], keys[d])
    return jnp.concatenate(out_v, axis=0), jnp.concatenate(out_k, axis=0)


def _top_candidates(v1, i1, v2, i2):
    tm = v1.shape[1]
    sub = lax.broadcasted_iota(I32, (SUBLANES, tm), 0)
    bc = lambda x, a: jnp.broadcast_to(x[a:a + 1, :], (SUBLANES, tm))
    vals = [jnp.where((a + 1) * (sub + 1) <= PEER_TOPK, bc(v1, a) + v2[0:SUBLANES, :], -jnp.inf)
            for a in range(PEER_TOPK)]
    eids = [bc(i1, a) * PEER_KEYS + i2[0:SUBLANES, :] for a in range(PEER_TOPK)]
    taken = jnp.zeros((SUBLANES, tm), I32)
    vals_hi = bc(v1, 0) + v2[SUBLANES:, :]
    eids_hi = bc(i1, 0) * PEER_KEYS + i2[SUBLANES:, :]
    keys_hi = SUBLANES + sub
    big = jnp.int32(2 ** 30)
    out_v, out_e = [], []
    for t in range(PEER_TOPK):
        keys = taken * PEER_TOPK + sub
        m = jnp.max(jnp.maximum(vals[0], vals_hi), axis=0, keepdims=True)
        kmin = jnp.min(jnp.minimum(jnp.where(vals[0] == m, keys, big), jnp.where(vals_hi == m, keys_hi, big)),
                       axis=0, keepdims=True)
        win = keys == kmin
        win_hi = keys_hi == kmin
        out_v.append(m)
        out_e.append(jnp.max(jnp.maximum(jnp.where(win, eids[0], -1), jnp.where(win_hi, eids_hi, -1)),
                             axis=0, keepdims=True))
        for d in range(PEER_TOPK - 1 - t):
            vals[d] = jnp.where(win, vals[d + 1], vals[d])
            eids[d] = jnp.where(win, eids[d + 1], eids[d])
        taken = jnp.where(win, taken + 1, taken)
        vals_hi = jnp.where(win_hi, -jnp.inf, vals_hi)
    return jnp.concatenate(out_v, axis=0), jnp.concatenate(out_e, axis=0)


_ROUTE_HEADS_PER_ITER = 8


def _route_kernel(h_ref, wq_ref, keys_ref, eid_ref, gate_ref, q_scr, eid_scr, gate_scr):
    tm = h_ref.shape[0]
    q = jnp.dot(h_ref[...].astype(BF16), wq_ref[...], preferred_element_type=F32)
    for c in range(2 * PEER_HEADS):
        q_scr[c] = q[:, c * PEER_HALF:(c + 1) * PEER_HALF]

    def body(it, carry):
        for off in range(_ROUTE_HEADS_PER_ITER):
            hh = it * _ROUTE_HEADS_PER_ITER + off
            tops = []
            for half in range(2):
                st = lax.dot_general(keys_ref[half], q_scr[2 * hh + half], (((1,), (1,)), ((), ())),
                                     precision=HI, preferred_element_type=F32)
                tops.append(_top_keys(st))
            (v1, i1), (v2, i2) = tops
            score, eids = _top_candidates(v1, i1, v2, i2)
            e = jnp.exp(score - jnp.max(score, axis=0, keepdims=True))
            r0 = pl.multiple_of(hh * PEER_TOPK, PEER_TOPK)
            gate_scr[pl.ds(r0, PEER_TOPK), :] = e / jnp.sum(e, axis=0, keepdims=True)
            eid_scr[pl.ds(r0, PEER_TOPK), :] = eids
        return carry

    lax.fori_loop(0, PEER_HEADS // _ROUTE_HEADS_PER_ITER, body, 0)
    gate_ref[...] = gate_scr[...].T
    eid_ref[...] = eid_scr[...].T


def _route(h, wq, keys):
    t, d = h.shape
    tm = TM_ROUTE
    nq = 2 * PEER_HEADS
    return pl.pallas_call(
        _route_kernel,
        grid=(t // tm,),
        in_specs=[pl.BlockSpec((tm, d), lambda i: (i, 0)),
                  pl.BlockSpec((d, nq * PEER_HALF), lambda i: (0, 0)),
                  pl.BlockSpec((2, PEER_KEYS, PEER_HALF), lambda i: (0, 0, 0))],
        out_specs=[pl.BlockSpec((tm, PEER_SEL), lambda i: (i, 0)),
                   pl.BlockSpec((tm, PEER_SEL), lambda i: (i, 0))],
        out_shape=[jax.ShapeDtypeStruct((t, PEER_SEL), I32),
                   jax.ShapeDtypeStruct((t, PEER_SEL), F32)],
        scratch_shapes=[pltpu.VMEM((nq, tm, PEER_HALF), F32),
                        pltpu.VMEM((PEER_SEL, tm), I32),
                        pltpu.VMEM((PEER_SEL, tm), F32)],
        compiler_params=_cparams(("parallel",), VMEM_LIMIT),
        name="peer_route",
    )(h, wq, keys)


def _peer_kernel(eid_cur_ref, eid_nxt_ref, x_ref, gate_ref, cmp_ref, exp_ref, uv_hbm, o_ref, buf, sem):
    step = pl.program_id(0)
    nsteps = pl.num_programs(0)
    g_tok = G_PEER
    rows = g_tok * PEER_SEL
    nsub = D_MODEL // LANES
    per_piece = rows // (2 * g_tok)

    def slot_wait(sl):
        pltpu.make_async_copy(uv_hbm.at[pl.ds(0, rows)], buf.at[sl], sem.at[sl]).wait()

    @pl.when(step == 0)
    def _():
        for sl in range(PEER_AHEAD):
            def body(r, carry, sl=sl):
                pltpu.make_async_copy(uv_hbm.at[eid_cur_ref[0, 0, sl * rows + r]], buf.at[sl, r], sem.at[sl]).start()
                return carry
            lax.fori_loop(0, rows, body, 0, unroll=8)

    diag = (lax.broadcasted_iota(I32, (nsub, PEER_SEL * nsub), 1) % nsub
            == lax.broadcasted_iota(I32, (nsub, PEER_SEL * nsub), 0))

    def group(sl):
        tok0 = sl * g_tok
        ahead = sl + PEER_AHEAD
        dst = ahead % PEER_SLOTS
        idx_ref = eid_cur_ref if ahead < PEER_SLOTS else eid_nxt_ref
        idx_off = dst * rows

        def prefetch(r0, n):
            for i in range(n):
                r = r0 + i
                pltpu.make_async_copy(uv_hbm.at[idx_ref[0, 0, idx_off + r]], buf.at[dst, r],
                                      sem.at[dst]).start(priority=i % 2)

        def tiles(g):
            return buf[sl, g * PEER_SEL:(g + 1) * PEER_SEL].reshape(PEER_SEL * nsub, LANES)

        slot_wait(sl)
        parts = []
        for g in range(g_tok):
            u_rows = pltpu.bitcast(tiles(g) << 16, F32).astype(BF16)
            x_row = x_ref[tok0 + g:tok0 + g + 1, :]
            x_tile = jnp.concatenate([x_row[:, s * LANES:(s + 1) * LANES] for s in range(nsub)], axis=0)
            q = lax.dot_general(x_tile.astype(BF16), u_rows, (((1,), (1,)), ((), ())),
                                preferred_element_type=F32)
            parts.append(jnp.where(diag, q, 0.0))
            prefetch(g * per_piece, per_piece)
        masked = jnp.concatenate(parts, axis=0)
        m_hi = masked.astype(BF16)
        m_lo = (masked - m_hi.astype(F32)).astype(BF16)
        part = (jnp.dot(m_hi, cmp_ref[...], preferred_element_type=F32)
                + jnp.dot(m_lo, cmp_ref[...], preferred_element_type=F32))
        act = jnp.sum(part.reshape(g_tok, nsub, PEER_SEL), axis=1)
        wgt = gate_ref[tok0:tok0 + g_tok, :] * _gelu(act)
        wrep = jnp.dot(wgt.astype(BF16), exp_ref[...], preferred_element_type=F32)
        for g in range(g_tok):
            v_rows = pltpu.bitcast(tiles(g) & jnp.uint32(0xFFFF0000), F32).astype(BF16)
            wexp = jnp.where(diag, jnp.broadcast_to(wrep[g:g + 1, :], diag.shape), 0.0).astype(BF16)
            o_tile = jnp.dot(wexp, v_rows, preferred_element_type=F32)
            for s in range(nsub):
                o_ref[tok0 + g:tok0 + g + 1, s * LANES:(s + 1) * LANES] = o_tile[s:s + 1, :]
            prefetch((g_tok + g) * per_piece, per_piece)

    for sl in range(PEER_SLOTS):
        group(sl)

    @pl.when(step == nsteps - 1)
    def _():
        for sl in range(PEER_AHEAD):
            slot_wait(sl)


def _peer(h, eid_tok, gate_tok, uv_packed):
    t, d = h.shape
    g_step = PEER_SLOTS * G_PEER
    nsteps = t // g_step
    rows = G_PEER * PEER_SEL
    nsub = d // LANES
    eid3 = eid_tok.reshape(nsteps, 1, PEER_SLOTS * rows)
    ncol = PEER_SEL * nsub
    col = jnp.arange(ncol) // nsub
    compress = (col[:, None] == jnp.arange(PEER_SEL)[None, :]).astype(F32)
    smem_blk = lambda f: pl.BlockSpec((1, 1, PEER_SLOTS * rows), f, memory_space=pltpu.SMEM)
    tile_blk = pl.BlockSpec((g_step, d), lambda i: (i, 0))
    return pl.pallas_call(
        _peer_kernel,
        grid=(nsteps,),
        in_specs=[smem_blk(lambda i: (i, 0, 0)),
                  smem_blk(lambda i: (jnp.minimum(i + 1, nsteps - 1), 0, 0)),
                  tile_blk,
                  pl.BlockSpec((g_step, PEER_SEL), lambda i: (i, 0)),
                  pl.BlockSpec((ncol, PEER_SEL), lambda i: (0, 0)),
                  pl.BlockSpec((PEER_SEL, ncol), lambda i: (0, 0)),
                  pl.BlockSpec(memory_space=pl.ANY)],
        out_specs=tile_blk,
        out_shape=jax.ShapeDtypeStruct((t, d), F32),
        scratch_shapes=[pltpu.VMEM((PEER_SLOTS, rows, nsub, LANES), U32), pltpu.SemaphoreType.DMA((PEER_SLOTS,))],
        compiler_params=_cparams(("arbitrary",), VMEM_LIMIT),
        name="peer_experts",
    )(eid3, eid3, h, gate_tok, compress.astype(BF16), compress.T.astype(BF16),
      uv_packed.reshape(-1, nsub, LANES))


_TR_PACK = 256


def _bf16_bits(x):
    bits = pltpu.bitcast(x, U32)
    return (bits + jnp.uint32(0x7FFF) + ((bits >> 16) & jnp.uint32(1))) >> 16


def _pack_kernel(u_ref, v_ref, o_ref):
    for s in range(D_MODEL // LANES):
        cols = slice(s * LANES, (s + 1) * LANES)
        o_ref[:, s, :] = _bf16_bits(u_ref[0, :, cols]) | (_bf16_bits(v_ref[0, :, cols]) << 16)


def _pack_tables(u_all, v_all, layer):
    _, n, d = u_all.shape
    tr = _TR_PACK
    src = pl.BlockSpec((1, tr, d), lambda i: (layer, i, 0))
    return pl.pallas_call(
        _pack_kernel,
        grid=(n // tr,),
        in_specs=[src, src],
        out_specs=pl.BlockSpec((tr, d // LANES, LANES), lambda i: (i, 0, 0)),
        out_shape=jax.ShapeDtypeStruct((n, d // LANES, LANES), U32),
        compiler_params=_cparams(("parallel",)),
        name="pack_tables",
    )(u_all, v_all)


def _out_kernel(h_ref, f_ref, p_ref, wp_ref, wgp_ref, g_ref, b_ref, o_ref):
    h = h_ref[...]
    ple = jnp.dot(p_ref[...].astype(BF16), wp_ref[...], preferred_element_type=F32)
    gate = _sigmoid(jnp.dot(h.astype(BF16), wgp_ref[...], preferred_element_type=F32))
    o_ref[...] = _ln(DN_ALPHA * h + f_ref[...] + ple * gate, g_ref[...], b_ref[...])


def _layer_out(h, ffn, p2, ple_w, ple_gate_w, ln_g, ln_b):
    t, d = h.shape
    tm = TM_OUT
    tok = lambda i: (i, 0)
    par = lambda i: (0, 0)
    return pl.pallas_call(
        _out_kernel,
        grid=(t // tm,),
        in_specs=[pl.BlockSpec((tm, d), tok), pl.BlockSpec((tm, d), tok),
                  pl.BlockSpec((tm, PLE_DIM), tok),
                  pl.BlockSpec((PLE_DIM, d), par), pl.BlockSpec((d, d), par),
                  pl.BlockSpec((1, d), par), pl.BlockSpec((1, d), par)],
        out_specs=pl.BlockSpec((tm, d), tok),
        out_shape=jax.ShapeDtypeStruct((t, d), F32),
        compiler_params=_cparams(("parallel",)),
        name="ple_ln2",
    )(h, ffn, p2, ple_w, ple_gate_w, ln_g.reshape(1, -1), ln_b.reshape(1, -1))


def _block_ones():
    head = jnp.arange(BRANCH) // HEAD_DIM
    return (head[:, None] == head[None, :]).astype(F32)


def _layer(h, p2, bsz, seq, layer, w_in_all, peer_u_all, peer_v_all, rwkv_mu, rwkv_w0, rwkv_w2, rwkv_a0, rwkv_a2,
           rwkv_g2, rwkv_kk, rwkv_ka, rwkv_rk, rwkv_lnx_g, rwkv_lnx_b, s5_lam_re, s5_lam_im, s5_b_re, s5_b_im,
           s5_c_re, s5_c_im, s5_d, s5_log_dt, s5_glu_w, s5_glu_b, fox_bf, conv_w, conv_b, conv_ln_g, conv_ln_b,
           w_branch, w_out, ln1_g, ln1_b, peer_wq, peer_k1, peer_k2, ple_w, ple_gate_w, ln2_g, ln2_b):
    t = bsz * seq
    d = D_MODEL
    w_packed, w_gate = _repack_w_in(w_in_all, layer)
    bd = _block_ones()
    wl = jnp.zeros((LANES, 3 * BRANCH), F32)
    wl = wl.at[0:32, 0:BRANCH].set(rwkv_w2.astype(F32))
    wl = wl.at[32:64, BRANCH:2 * BRANCH].set(rwkv_a2.astype(F32))
    wl = wl.at[64:128, 2 * BRANCH:].set(rwkv_g2.astype(F32))

    p_rwkv, p_s5_t, q_heads, k_heads, v_aug, p_ff, p_conv = _project(h, w_packed, bsz, seq)

    r, dec, k2, v, na, bb, g, bonus = _rwkv_pre(p_rwkv, rwkv_mu, rwkv_w0, rwkv_a0, rwkv_kk, rwkv_ka,
                                                 rwkv_rk.reshape(-1), wl, bd, bsz, seq)
    y_scan = _rwkv_scan(*(_to_scan(x, bsz, seq) for x in (na, dec, bb, k2, r, v)))
    y_rwkv = _rwkv_post(_from_scan(y_scan, bsz, seq), g.reshape(t, BRANCH), bonus.reshape(t, BRANCH),
                        rwkv_lnx_g, rwkv_lnx_b, bd)

    bre, bim, lre, lim, cc = _s5_params(s5_lam_re, s5_lam_im, s5_b_re, s5_b_im, s5_c_re, s5_c_im, s5_log_dt)
    y_s5_t = _s5(p_s5_t, bre, bim, lre, lim, cc, s5_d, s5_glu_w, s5_glu_b, bsz, seq)

    f_rows = p_ff[:, :HEADS].reshape(bsz, seq, HEADS).transpose(0, 2, 1).reshape(bsz * HEADS, seq)
    bias_rows = jnp.tile(fox_bf.astype(F32), bsz).reshape(bsz * HEADS, 1)
    c = _fox_cumsum(f_rows, bias_rows)
    y_fox = _fox_attn(q_heads, k_heads, v_aug,
                      c.reshape(bsz, HEADS, seq, 1), c.reshape(bsz, HEADS, seq // TK_ATT, TK_ATT))

    y_conv = _conv_mixer(p_conv, conv_w, conv_b, conv_ln_g, conv_ln_b, bsz, seq)

    h1 = _merge(h, y_rwkv, y_s5_t, y_fox, y_conv, w_gate, w_branch.astype(BF16),
                w_out.astype(BF16), ln1_g, ln1_b, bsz, seq)

    keys = jnp.stack([peer_k1, peer_k2]).astype(F32)
    eid_tok, gate_tok = _route(h1, peer_wq.astype(BF16), keys)
    ffn = _peer(h1, eid_tok, gate_tok, _pack_tables(peer_u_all, peer_v_all, layer))

    return _layer_out(h1, ffn, p2, ple_w.astype(BF16), ple_gate_w.astype(BF16), ln2_g, ln2_b)


def kernel(x, p, ln_in_g, ln_in_b, w_in, rwkv_mu, rwkv_w0, rwkv_w2, rwkv_a0, rwkv_a2, rwkv_g2, rwkv_kk, rwkv_ka, rwkv_rk, rwkv_lnx_g, rwkv_lnx_b, s5_lam_re, s5_lam_im, s5_b_re, s5_b_im, s5_c_re, s5_c_im, s5_d, s5_log_dt, s5_glu_w, s5_glu_b, fox_bf, conv_w, conv_b, conv_ln_g, conv_ln_b, w_branch, w_out, ln1_g, ln1_b, peer_wq, peer_k1, peer_k2, peer_u, peer_v, ple_w, ple_gate_w, ln2_g, ln2_b):
    bsz, seq, d = x.shape
    t = bsz * seq
    h = _layer_norm(x.reshape(t, d), ln_in_g, ln_in_b)
    per_layer = (rwkv_mu, rwkv_w0, rwkv_w2, rwkv_a0, rwkv_a2, rwkv_g2, rwkv_kk, rwkv_ka, rwkv_rk,
                 rwkv_lnx_g, rwkv_lnx_b, s5_lam_re, s5_lam_im, s5_b_re, s5_b_im, s5_c_re, s5_c_im, s5_d,
                 s5_log_dt, s5_glu_w, s5_glu_b, fox_bf, conv_w, conv_b, conv_ln_g, conv_ln_b, w_branch,
                 w_out, ln1_g, ln1_b, peer_wq, peer_k1, peer_k2, ple_w, ple_gate_w, ln2_g, ln2_b)
    for i in range(p.shape[0]):
        h = _layer(h, p[i].reshape(t, PLE_DIM), bsz, seq, i, w_in, peer_u, peer_v, *(w[i] for w in per_layer))
    return h.reshape(bsz, seq, d)
```

```python
import functools
import math

import jax
import jax.numpy as jnp
from jax import lax
from jax.experimental import pallas as pl
from jax.experimental.pallas import tpu as pltpu

F32 = jnp.float32
BF16 = jnp.bfloat16
I32 = jnp.int32
U32 = jnp.uint32
HI = lax.Precision.HIGHEST

D_MODEL = 1024
BRANCH = 256
HEADS = 4
HEAD_DIM = 64
RWKV_COLS = 896
S5_GROUPS = 16
S5_GROUP = 16
S5_STATE = 64
S5_N = S5_GROUPS * S5_STATE
CONV_WIDTH = 31
CONV_HALO = 32
PEER_HEADS = 8
PEER_KEYS = 128
PEER_HALF = 128
PEER_TOPK = 16
PEER_SEL = PEER_HEADS * PEER_TOPK
PLE_DIM = 256
RWKV_GN_EPS = 64e-5
LN_EPS = 1e-5
DEPTH = 2
DN_ALPHA = (2 * DEPTH) ** 0.25

SUBLANES = 8
LANES = 128
VMEM_LIMIT = 56 * 1024 * 1024

TM_PROJ = 512
TQ_PRE = 256
TC_SCAN = 32
TC_S5 = 32
TQ_ATT = 256
TK_ATT = 512
TC_CONV = 512
TM_MERGE = 256
TM_ROUTE = 128
G_PEER = 16
PEER_SLOTS = 4
PEER_AHEAD = 2
TM_OUT = 256


def _cparams(sem, vmem=None):
    return pltpu.CompilerParams(dimension_semantics=sem, vmem_limit_bytes=vmem)


def _ln(z, g, b):
    mu = jnp.mean(z, axis=-1, keepdims=True)
    zc = z - mu
    var = jnp.mean(zc * zc, axis=-1, keepdims=True)
    return zc * lax.rsqrt(var + LN_EPS) * g + b


def _gelu(y):
    return 0.5 * y * (1.0 + lax.erf(y * (1.0 / math.sqrt(2.0))))


def _sigmoid(y):
    return 1.0 / (1.0 + jnp.exp(-y))


def _softplus(y):
    return jnp.maximum(y, 0.0) + jnp.log(1.0 + jnp.exp(-jnp.abs(y)))


def _ln_kernel(x_ref, g_ref, b_ref, o_ref):
    o_ref[...] = _ln(x_ref[...], g_ref[...], b_ref[...])


def _layer_norm(x2, g, b):
    t, d = x2.shape
    tm = TM_PROJ
    return pl.pallas_call(
        _ln_kernel,
        grid=(t // tm,),
        in_specs=[pl.BlockSpec((tm, d), lambda i: (i, 0)),
                  pl.BlockSpec((1, d), lambda i: (0, 0)),
                  pl.BlockSpec((1, d), lambda i: (0, 0))],
        out_specs=pl.BlockSpec((tm, d), lambda i: (i, 0)),
        out_shape=jax.ShapeDtypeStruct((t, d), F32),
        compiler_params=_cparams(("parallel",)),
        name="ln_in",
    )(x2, g.reshape(1, d), b.reshape(1, d))


_P_RWKV = (0, 896)
_P_S5 = (896, 1152)
_P_FQKV = (1152, 1920)
_P_FF = (1920, 2048)
_P_CONV = (2048, 2560)
_P_COLS = 2560


_N_FRONT = RWKV_COLS + BRANCH + 3 * BRANCH + HEADS
_TR_REPACK = 128


def _repack_kernel(w_ref, wp_ref, wg_ref):
    w = w_ref[0]
    lo, hi = _P_FF
    wp_ref[:, 0:lo] = w[:, 0:lo].astype(BF16)
    lane = lax.broadcasted_iota(I32, (w.shape[0], hi - lo), 1)
    wp_ref[:, lo:hi] = jnp.where(lane < HEADS, w[:, lo:hi], 0.0).astype(BF16)
    wp_ref[:, hi:_P_COLS] = w[:, _N_FRONT:_N_FRONT + 2 * BRANCH].astype(BF16)
    wg_ref[...] = w[:, _N_FRONT + 2 * BRANCH:].astype(BF16)


def _repack_w_in(w_in_all, layer):
    _, d, n = w_in_all.shape
    n_gate = n - _N_FRONT - 2 * BRANCH
    tr = _TR_REPACK
    return pl.pallas_call(
        _repack_kernel,
        grid=(d // tr,),
        in_specs=[pl.BlockSpec((1, tr, n), lambda i: (layer, i, 0))],
        out_specs=[pl.BlockSpec((tr, _P_COLS), lambda i: (i, 0)),
                   pl.BlockSpec((tr, n_gate), lambda i: (i, 0))],
        out_shape=[jax.ShapeDtypeStruct((d, _P_COLS), BF16),
                   jax.ShapeDtypeStruct((d, n_gate), BF16)],
        compiler_params=_cparams(("parallel",)),
        name="repack_w_in",
    )(w_in_all)


def _proj_kernel(h_ref, w_ref, rw_ref, s5_ref, q_ref, k_ref, v_ref, ff_ref, cv_ref):
    x = h_ref[...].astype(BF16)
    for (lo, hi), o_ref in ((_P_RWKV, rw_ref), (_P_S5, s5_ref), (_P_FF, ff_ref), (_P_CONV, cv_ref)):
        o_ref[...] = jnp.dot(x, w_ref[:, lo:hi], preferred_element_type=F32)
    qkv = jnp.dot(x, w_ref[:, _P_FQKV[0]:_P_FQKV[1]], preferred_element_type=F32)
    tm = qkv.shape[0]
    ones_col = (lax.broadcasted_iota(I32, (tm, LANES - HEAD_DIM), 1) == 0).astype(F32)
    for hh in range(HEADS):
        q_ref[0, hh] = qkv[:, hh * HEAD_DIM:(hh + 1) * HEAD_DIM]
        k_ref[0, hh] = qkv[:, BRANCH + hh * HEAD_DIM:BRANCH + (hh + 1) * HEAD_DIM]
        v_ref[0, hh] = jnp.concatenate(
            [qkv[:, 2 * BRANCH + hh * HEAD_DIM:2 * BRANCH + (hh + 1) * HEAD_DIM], ones_col], axis=1)


def _project(h, w_packed, bsz, seq):
    t, d = h.shape
    tm = TM_PROJ
    nst = seq // tm
    w_rwkv, w_s5, w_ff, w_conv = [hi - lo for lo, hi in (_P_RWKV, _P_S5, _P_FF, _P_CONV)]
    tok = lambda i: (i, 0)
    heads = lambda i: (i // nst, 0, i % nst, 0)
    out_shapes = [jax.ShapeDtypeStruct((t, w_rwkv), F32),
                  jax.ShapeDtypeStruct((seq, bsz * w_s5), F32),
                  jax.ShapeDtypeStruct((bsz, HEADS, seq, HEAD_DIM), F32),
                  jax.ShapeDtypeStruct((bsz, HEADS, seq, HEAD_DIM), F32),
                  jax.ShapeDtypeStruct((bsz, HEADS, seq, LANES), F32),
                  jax.ShapeDtypeStruct((t, w_ff), F32),
                  jax.ShapeDtypeStruct((t, w_conv), F32)]
    out_specs = [pl.BlockSpec((tm, w_rwkv), tok),
                 pl.BlockSpec((tm, w_s5), lambda i: (i % nst, i // nst)),
                 pl.BlockSpec((1, HEADS, tm, HEAD_DIM), heads),
                 pl.BlockSpec((1, HEADS, tm, HEAD_DIM), heads),
                 pl.BlockSpec((1, HEADS, tm, LANES), heads),
                 pl.BlockSpec((tm, w_ff), tok),
                 pl.BlockSpec((tm, w_conv), tok)]
    return pl.pallas_call(
        _proj_kernel,
        grid=(t // tm,),
        in_specs=[pl.BlockSpec((tm, d), tok),
                  pl.BlockSpec((d, _P_COLS), lambda i: (0, 0))],
        out_specs=out_specs,
        out_shape=out_shapes,
        compiler_params=_cparams(("parallel",), VMEM_LIMIT),
        name="in_proj",
    )(h, w_packed)


def _rwkv_pre_kernel(z_ref, zp_ref, mu_ref, w0_ref, a0_ref, kk_ref, ka_ref, rk_ref, wl_ref, bd_ref,
                     r_out, w_out, k_out, v_out, na_out, b_out, g_out, bv_out):
    j = pl.program_id(1)
    z = z_ref[0]
    prev = zp_ref[0][SUBLANES - 1:SUBLANES, :]
    prev = jnp.where(j == 0, 0.0, prev)
    row = lax.broadcasted_iota(I32, z.shape, 0)
    zs = jnp.where(row == 0, prev, pltpu.roll(z, 1, 0))
    z = z + (zs - z) * mu_ref[...]
    r = z[:, 0:256]
    k = z[:, 256:512]
    v = z[:, 512:768]
    zc = z[:, 768:896]
    lane = lax.broadcasted_iota(I32, zc.shape, 1)
    act = jnp.where(lane < 32, jnp.tanh(zc), jnp.where(lane < 64, zc, _sigmoid(zc)))
    lo = jnp.dot(act, wl_ref[...], precision=HI, preferred_element_type=F32)
    dw = lo[:, 0:256]
    da = lo[:, 256:512]
    g = lo[:, 512:768]
    w_log = -_softplus(-(w0_ref[...] + dw)) - 0.5
    decay = jnp.exp(-jnp.exp(w_log))
    a = _sigmoid(a0_ref[...] + da)
    kkv = k * kk_ref[...]
    bd = bd_ref[...]
    ss = jnp.dot(kkv * kkv, bd, precision=HI, preferred_element_type=F32)
    kkn = kkv * lax.rsqrt(jnp.maximum(ss, 1e-24))
    k2 = k * (1.0 + (a - 1.0) * ka_ref[...])
    bonus = jnp.dot(r * k2 * rk_ref[...], bd, precision=HI, preferred_element_type=F32) * v
    r_out[0] = r
    w_out[0] = decay
    k_out[0] = k2
    v_out[0] = v
    na_out[0] = -kkn
    b_out[0] = kkn * a
    g_out[0] = g
    bv_out[0] = bonus


def _rwkv_pre(p_rwkv, mu, w0, a0, kk, ka, rk, wl, bd, bsz, seq):
    z3 = p_rwkv.reshape(bsz, seq, RWKV_COLS)
    tq = TQ_PRE
    blk = lambda b, j: (b, j, 0)
    par = lambda b, j: (0, 0)
    out_shape = [jax.ShapeDtypeStruct((bsz, seq, BRANCH), F32)] * 8
    out_specs = [pl.BlockSpec((1, tq, BRANCH), blk)] * 8
    return pl.pallas_call(
        _rwkv_pre_kernel,
        grid=(bsz, seq // tq),
        in_specs=[pl.BlockSpec((1, tq, RWKV_COLS), blk),
                  pl.BlockSpec((1, SUBLANES, RWKV_COLS),
                               lambda b, j: (b, jnp.maximum(j * (tq // SUBLANES) - 1, 0), 0)),
                  pl.BlockSpec((1, RWKV_COLS), par),
                  pl.BlockSpec((1, BRANCH), par), pl.BlockSpec((1, BRANCH), par),
                  pl.BlockSpec((1, BRANCH), par), pl.BlockSpec((1, BRANCH), par),
                  pl.BlockSpec((1, BRANCH), par),
                  pl.BlockSpec((LANES, 3 * BRANCH), par),
                  pl.BlockSpec((BRANCH, BRANCH), par)],
        out_specs=out_specs,
        out_shape=out_shape,
        compiler_params=_cparams(("parallel", "parallel")),
        name="rwkv_pre",
    )(z3, z3, mu.reshape(1, -1), w0.reshape(1, -1), a0.reshape(1, -1), kk.reshape(1, -1),
      ka.reshape(1, -1), rk.reshape(1, -1), wl, bd)


_SCAN_IG = HEAD_DIM // 2 // SUBLANES
_SCAN_HALF = HEAD_DIM // 2


def _rwkv_scan_kernel(a_ref, w_ref, b_ref, k_ref, r_ref, v_ref, y_ref, s_ref, a_scr, w_scr, b_scr, k_scr, r_scr):
    @pl.when(pl.program_id(0) == 0)
    def _():
        s_ref[...] = jnp.zeros_like(s_ref)

    tc = a_ref.shape[0]
    tile = (SUBLANES, LANES)

    lane = lax.broadcasted_iota(I32, (tc * _SCAN_HALF, LANES), 1)
    for src_ref, dst_ref in ((a_ref, a_scr), (w_ref, w_scr), (b_ref, b_scr), (k_ref, k_scr), (r_ref, r_scr)):
        x = src_ref[...].reshape(tc * _SCAN_HALF, LANES)
        xr = pltpu.roll(x, LANES // 2, 1)
        dst_ref[:, 0:_SCAN_HALF, :] = jnp.where(lane < LANES // 2, x, xr).reshape(tc, _SCAN_HALF, LANES)
        dst_ref[:, _SCAN_HALF:, :] = jnp.where(lane < LANES // 2, xr, x).reshape(tc, _SCAN_HALF, LANES)

    def step(t, carry):
        vv = [v_ref[t, ig * SUBLANES:(ig + 1) * SUBLANES, :] for ig in range(_SCAN_IG)]
        sa = [[jnp.zeros(tile, F32), jnp.zeros(tile, F32)] for _ in range(_SCAN_IG)]
        for j in range(HEAD_DIM):
            ab = jnp.broadcast_to(a_scr[t, j:j + 1, :], tile)
            for ig in range(_SCAN_IG):
                sa[ig][j % 2] = sa[ig][j % 2] + s_ref[ig, j] * ab
        sa = [x[0] + x[1] for x in sa]
        yy = [[jnp.zeros(tile, F32), jnp.zeros(tile, F32)] for _ in range(_SCAN_IG)]
        for j in range(HEAD_DIM):
            wb = jnp.broadcast_to(w_scr[t, j:j + 1, :], tile)
            bb = jnp.broadcast_to(b_scr[t, j:j + 1, :], tile)
            kb = jnp.broadcast_to(k_scr[t, j:j + 1, :], tile)
            rb = jnp.broadcast_to(r_scr[t, j:j + 1, :], tile)
            for ig in range(_SCAN_IG):
                s = s_ref[ig, j] * wb + sa[ig] * bb + vv[ig] * kb
                s_ref[ig, j] = s
                yy[ig][j % 2] = yy[ig][j % 2] + s * rb
        for ig in range(_SCAN_IG):
            y_ref[t, ig * SUBLANES:(ig + 1) * SUBLANES, :] = yy[ig][0] + yy[ig][1]
        return carry

    lax.fori_loop(0, tc, step, 0)


def _rwkv_scan(a_t, w_t, b_t, k_t, r_t, v_t):
    seq = a_t.shape[0]
    tc = TC_SCAN
    spec = pl.BlockSpec((tc, _SCAN_HALF, LANES), lambda i: (i, 0, 0))
    return pl.pallas_call(
        _rwkv_scan_kernel,
        grid=(seq // tc,),
        in_specs=[spec] * 6,
        out_specs=spec,
        out_shape=jax.ShapeDtypeStruct((seq, _SCAN_HALF, LANES), F32),
        scratch_shapes=[pltpu.VMEM((_SCAN_IG, HEAD_DIM, SUBLANES, LANES), F32)]
                       + [pltpu.VMEM((tc, HEAD_DIM, LANES), F32)] * 5,
        compiler_params=_cparams(("arbitrary",)),
        name="rwkv_scan",
    )(a_t, w_t, b_t, k_t, r_t, v_t)


def _to_scan(x, bsz, seq):
    n_pairs = bsz * HEADS
    xt = x.reshape(bsz, seq, HEADS, 2, HEAD_DIM // 2).transpose(1, 4, 3, 0, 2)
    xt = xt.reshape(seq, HEAD_DIM // 2, 2, n_pairs)
    pad = LANES // 2 - n_pairs
    if pad:
        xt = jnp.pad(xt, ((0, 0), (0, 0), (0, 0), (0, pad)))
    return xt.reshape(seq, HEAD_DIM // 2, LANES)


def _from_scan(y, bsz, seq):
    n_pairs = bsz * HEADS
    yt = y.reshape(seq, HEAD_DIM // 2, 2, LANES // 2)[..., :n_pairs]
    yt = yt.reshape(seq, HEAD_DIM // 2, 2, bsz, HEADS).transpose(3, 0, 4, 2, 1)
    return yt.reshape(bsz * seq, BRANCH)


def _rwkv_post_kernel(y_ref, g_ref, bv_ref, gg_ref, gb_ref, bd_ref, o_ref):
    y = y_ref[...]
    bd = bd_ref[...]
    inv = 1.0 / HEAD_DIM
    mean = jnp.dot(y, bd, precision=HI, preferred_element_type=F32) * inv
    yc = y - mean
    var = jnp.dot(yc * yc, bd, precision=HI, preferred_element_type=F32) * inv
    yn = yc * lax.rsqrt(var + RWKV_GN_EPS) * gg_ref[...] + gb_ref[...]
    o_ref[...] = (yn + bv_ref[...]) * g_ref[...]


def _rwkv_post(y, g, bv, gn_g, gn_b, bd):
    t = y.shape[0]
    tm = TM_PROJ
    tok = lambda i: (i, 0)
    par = lambda i: (0, 0)
    return pl.pallas_call(
        _rwkv_post_kernel,
        grid=(t // tm,),
        in_specs=[pl.BlockSpec((tm, BRANCH), tok)] * 3
                 + [pl.BlockSpec((1, BRANCH), par)] * 2 + [pl.BlockSpec((BRANCH, BRANCH), par)],
        out_specs=pl.BlockSpec((tm, BRANCH), tok),
        out_shape=jax.ShapeDtypeStruct((t, BRANCH), F32),
        compiler_params=_cparams(("parallel",)),
        name="rwkv_post",
    )(y, g, bv, gn_g.reshape(1, -1), gn_b.reshape(1, -1), bd)


def _s5_kernel(u_ref, bre_ref, bim_ref, lre_ref, lim_ref, cc_ref, d_ref, gw_ref, gb_ref, o_ref,
               xre_ref, xim_ref, bure_ref, buim_ref, xs_ref):
    @pl.when(pl.program_id(0) == 0)
    def _():
        xre_ref[...] = jnp.zeros_like(xre_ref)
        xim_ref[...] = jnp.zeros_like(xim_ref)

    tc, nb, w = u_ref.shape
    u2 = u_ref[...].reshape(tc * nb, w)
    ub = u2.astype(BF16)
    bure_ref[...] = jnp.dot(ub, bre_ref[...], preferred_element_type=F32)
    buim_ref[...] = jnp.dot(ub, bim_ref[...], preferred_element_type=F32)
    lre = jnp.broadcast_to(lre_ref[...], (nb, S5_N))
    lim = jnp.broadcast_to(lim_ref[...], (nb, S5_N))

    def step(t, carry):
        xr, xi = carry
        r0 = pl.multiple_of(t * nb, nb)
        nr = lre * xr - lim * xi + bure_ref[pl.ds(r0, nb), :]
        ni = lre * xi + lim * xr + buim_ref[pl.ds(r0, nb), :]
        xs_ref[pl.ds(r0, nb), 0:S5_N] = nr
        xs_ref[pl.ds(r0, nb), S5_N:2 * S5_N] = ni
        return nr, ni

    xr, xi = lax.fori_loop(0, tc, step, (xre_ref[...], xim_ref[...]))
    xre_ref[...] = xr
    xim_ref[...] = xi
    y = jnp.dot(xs_ref[...].astype(BF16), cc_ref[...], preferred_element_type=F32) + d_ref[...] * u2
    y = _gelu(y)
    gate = jnp.dot(y.astype(BF16), gw_ref[...], preferred_element_type=F32) + gb_ref[...]
    o_ref[...] = (y * _sigmoid(gate)).reshape(tc, nb, w)


def _s5(u_t, bre, bim, lre, lim, cc, d_skip, glu_w, glu_b, bsz, seq):
    u3 = u_t.reshape(seq, bsz, BRANCH)
    tc = TC_S5
    par = lambda i: (0, 0)
    blk = pl.BlockSpec((tc, bsz, BRANCH), lambda i: (i, 0, 0))
    out = pl.pallas_call(
        _s5_kernel,
        grid=(seq // tc,),
        in_specs=[blk,
                  pl.BlockSpec((BRANCH, S5_N), par), pl.BlockSpec((BRANCH, S5_N), par),
                  pl.BlockSpec((1, S5_N), par), pl.BlockSpec((1, S5_N), par),
                  pl.BlockSpec((2 * S5_N, BRANCH), par),
                  pl.BlockSpec((1, BRANCH), par),
                  pl.BlockSpec((BRANCH, BRANCH), par),
                  pl.BlockSpec((1, BRANCH), par)],
        out_specs=blk,
        out_shape=jax.ShapeDtypeStruct((seq, bsz, BRANCH), F32),
        scratch_shapes=[pltpu.VMEM((bsz, S5_N), F32), pltpu.VMEM((bsz, S5_N), F32),
                        pltpu.VMEM((tc * bsz, S5_N), F32), pltpu.VMEM((tc * bsz, S5_N), F32),
                        pltpu.VMEM((tc * bsz, 2 * S5_N), F32)],
        compiler_params=_cparams(("arbitrary",), VMEM_LIMIT),
        name="s5_scan",
    )(u3, bre.astype(BF16), bim.astype(BF16), lre, lim, cc.astype(BF16), d_skip.reshape(1, -1),
      glu_w.astype(BF16), glu_b.reshape(1, -1))
    return out.reshape(seq, bsz * BRANCH)


def _s5_params(lam_re, lam_im, b_re, b_im, c_re, c_im, log_dt):
    lam = lax.complex(lam_re.astype(F32), lam_im.astype(F32))
    dt = jnp.exp(log_dt.astype(F32))[:, None]
    lam_bar = jnp.exp(lam * dt)
    b_bar = ((lam_bar - 1.0) / lam)[..., None] * lax.complex(b_re.astype(F32), b_im.astype(F32))
    eye = jnp.eye(S5_GROUPS, dtype=F32)
    bre = jnp.einsum('gpc,gh->gchp', jnp.real(b_bar), eye).reshape(BRANCH, S5_N)
    bim = jnp.einsum('gpc,gh->gchp', jnp.imag(b_bar), eye).reshape(BRANCH, S5_N)
    cre = jnp.einsum('gcp,gh->gphc', c_re.astype(F32), eye).reshape(S5_N, BRANCH)
    cim = jnp.einsum('gcp,gh->gphc', c_im.astype(F32), eye).reshape(S5_N, BRANCH)
    cc = jnp.concatenate([cre, -cim], axis=0)
    return bre, bim, jnp.real(lam_bar).reshape(1, S5_N), jnp.imag(lam_bar).reshape(1, S5_N), cc


def _fox_cumsum_kernel(f_ref, bias_ref, c_ref):
    x = f_ref[...] + bias_ref[...]
    c = -_softplus(-x)
    n = c.shape[1]
    lane = lax.broadcasted_iota(I32, c.shape, 1)
    sh = 1
    while sh < n:
        c = c + jnp.where(lane >= sh, pltpu.roll(c, sh, 1), 0.0)
        sh *= 2
    c_ref[...] = c


def _fox_cumsum(f_rows, bias_rows):
    rows, seq = f_rows.shape
    return pl.pallas_call(
        _fox_cumsum_kernel,
        out_shape=jax.ShapeDtypeStruct((rows, seq), F32),
        name="fox_cumsum",
    )(f_rows, bias_rows)


def _fox_attn_kernel(q_ref, k_ref, v_ref, cq_ref, ck_ref, o_ref):
    qi = pl.program_id(1)
    nh, tq = q_ref.shape[1], q_ref.shape[2]
    tk = ck_ref.shape[3]
    qs = [(q_ref[0, h] * (HEAD_DIM ** -0.5)).astype(BF16) for h in range(nh)]
    cqs = [cq_ref[0, h] for h in range(nh)]
    qpos = qi * tq + lax.broadcasted_iota(I32, (tq, tk), 0)
    nkb = (qi * tq + tq + tk - 1) // tk

    def body(kb, carry):
        k0 = pl.multiple_of(kb * tk, tk)
        causal = k0 + lax.broadcasted_iota(I32, (tq, tk), 1) <= qpos
        out = []
        for h in range(nh):
            m, acc = carry[h]
            kblk = k_ref[0, h, pl.ds(k0, tk), :].astype(BF16)
            vblk = v_ref[0, h, pl.ds(k0, tk), :].astype(BF16)
            s = lax.dot_general(qs[h], kblk, (((1,), (1,)), ((), ())), preferred_element_type=F32)
            s = s + cqs[h] - ck_ref[0, h, pl.ds(kb, 1), :]
            s = jnp.where(causal, s, -1e30)
            m_new = jnp.maximum(m, jnp.max(s, axis=-1, keepdims=True))
            p = jnp.exp(s - m_new)
            acc = jnp.exp(m - m_new) * acc + jnp.dot(p.astype(BF16), vblk, preferred_element_type=F32)
            out.append((m_new, acc))
        return tuple(out)

    init = tuple((jnp.full((tq, 1), -1e30, F32), jnp.zeros((tq, v_ref.shape[3]), F32)) for _ in range(nh))
    res = lax.fori_loop(0, nkb, body, init)
    for h in range(nh):
        acc = res[h][1]
        o_ref[0, h] = acc[:, 0:HEAD_DIM] / acc[:, HEAD_DIM:HEAD_DIM + 1]


def _fox_attn(q, k, v, c_col, c_rowb):
    bsz, nh, seq, hd = q.shape
    tq = TQ_ATT
    nkb, tk = c_rowb.shape[2], c_rowb.shape[3]
    full = lambda b, i: (b, 0, 0, 0)
    tile = lambda b, i: (b, 0, i, 0)
    return pl.pallas_call(
        _fox_attn_kernel,
        grid=(bsz, seq // tq),
        in_specs=[pl.BlockSpec((1, nh, tq, hd), tile),
                  pl.BlockSpec((1, nh, seq, hd), full),
                  pl.BlockSpec((1, nh, seq, v.shape[3]), full),
                  pl.BlockSpec((1, nh, tq, 1), tile),
                  pl.BlockSpec((1, nh, nkb, tk), full)],
        out_specs=pl.BlockSpec((1, nh, tq, hd), tile),
        out_shape=jax.ShapeDtypeStruct((bsz, nh, seq, hd), F32),
        compiler_params=_cparams(("parallel", "parallel"), VMEM_LIMIT),
        name="fox_attn",
    )(q, k, v, c_col, c_rowb)


def _conv_kernel(x_ref, xp_ref, w_ref, b_ref, g_ref, be_ref, o_ref, buf_ref):
    j = pl.program_id(1)
    tc = x_ref.shape[1]
    x = x_ref[0]
    buf_ref[CONV_HALO:CONV_HALO + tc, :] = x[:, 0:BRANCH] * _sigmoid(x[:, BRANCH:2 * BRANCH])
    xp = xp_ref[0]
    hp = xp[:, 0:BRANCH] * _sigmoid(xp[:, BRANCH:2 * BRANCH])
    buf_ref[0:CONV_HALO, :] = jnp.where(j == 0, 0.0, hp)
    acc = jnp.zeros((tc, BRANCH), F32) + b_ref[...]
    off = CONV_HALO - (CONV_WIDTH - 1)
    for kk in range(CONV_WIDTH):
        acc = acc + buf_ref[off + kk:off + kk + tc, :] * w_ref[kk:kk + 1, :]
    y = _ln(acc, g_ref[...], be_ref[...])
    o_ref[0] = y * _sigmoid(y)


def _conv_mixer(p_conv, conv_w, conv_b, ln_g, ln_b, bsz, seq):
    x3 = p_conv.reshape(bsz, seq, 2 * BRANCH)
    tc = TC_CONV
    par = lambda b, j: (0, 0)
    out = pl.pallas_call(
        _conv_kernel,
        grid=(bsz, seq // tc),
        in_specs=[pl.BlockSpec((1, tc, 2 * BRANCH), lambda b, j: (b, j, 0)),
                  pl.BlockSpec((1, CONV_HALO, 2 * BRANCH),
                               lambda b, j: (b, jnp.maximum(j * (tc // CONV_HALO) - 1, 0), 0)),
                  pl.BlockSpec((CONV_WIDTH, BRANCH), par),
                  pl.BlockSpec((1, BRANCH), par), pl.BlockSpec((1, BRANCH), par),
                  pl.BlockSpec((1, BRANCH), par)],
        out_specs=pl.BlockSpec((1, tc, BRANCH), lambda b, j: (b, j, 0)),
        out_shape=jax.ShapeDtypeStruct((bsz, seq, BRANCH), F32),
        scratch_shapes=[pltpu.VMEM((CONV_HALO + tc, BRANCH), F32)],
        compiler_params=_cparams(("parallel", "parallel")),
        name="conv_mixer",
    )(x3, x3, conv_w, conv_b.reshape(1, -1), ln_g.reshape(1, -1), ln_b.reshape(1, -1))
    return out.reshape(bsz * seq, BRANCH)


def _merge_kernel(h_ref, y0_ref, y1_ref, y2_ref, y3_ref, wg_ref, wb_ref, wo_ref, g_ref, b_ref, o_ref):
    h = h_ref[...]
    hb = h.astype(BF16)
    merged = None
    for br, y_ref in enumerate((y0_ref, y1_ref, y2_ref, y3_ref)):
        gate = _sigmoid(jnp.dot(hb, wg_ref[:, br * D_MODEL:(br + 1) * D_MODEL], preferred_element_type=F32))
        if br == 2:
            proj = None
            for hh in range(HEADS):
                part = jnp.dot(y_ref[0, hh].astype(BF16), wb_ref[br, hh * HEAD_DIM:(hh + 1) * HEAD_DIM, :],
                               preferred_element_type=F32)
                proj = part if proj is None else proj + part
        else:
            proj = jnp.dot(y_ref[...].astype(BF16), wb_ref[br], preferred_element_type=F32)
        term = gate * proj
        merged = term if merged is None else merged + term
    mix = jnp.dot(merged.astype(BF16), wo_ref[...], preferred_element_type=F32)
    o_ref[...] = _ln(DN_ALPHA * h + mix, g_ref[...], b_ref[...])


def _merge(h, y_rwkv, y_s5_t, y_fox, y_conv, wg, wb, wo, ln_g, ln_b, bsz, seq):
    t, d = h.shape
    tm = TM_MERGE
    nst = seq // tm
    tok = lambda i: (i, 0)
    par2 = lambda i: (0, 0)
    br_spec = pl.BlockSpec((tm, BRANCH), tok)
    return pl.pallas_call(
        _merge_kernel,
        grid=(t // tm,),
        in_specs=[pl.BlockSpec((tm, d), tok),
                  br_spec,
                  pl.BlockSpec((tm, BRANCH), lambda i: (i % nst, i // nst)),
                  pl.BlockSpec((1, HEADS, tm, HEAD_DIM), lambda i: (i // nst, 0, i % nst, 0)),
                  br_spec,
                  pl.BlockSpec((d, 4 * d), par2),
                  pl.BlockSpec((4, BRANCH, d), lambda i: (0, 0, 0)),
                  pl.BlockSpec((d, d), par2),
                  pl.BlockSpec((1, d), par2), pl.BlockSpec((1, d), par2)],
        out_specs=pl.BlockSpec((tm, d), tok),
        out_shape=jax.ShapeDtypeStruct((t, d), F32),
        compiler_params=_cparams(("parallel",), VMEM_LIMIT),
        name="merge_ln1",
    )(h, y_rwkv, y_s5_t, y_fox, y_conv, wg, wb, wo, ln_g.reshape(1, -1), ln_b.reshape(1, -1))


def _sort_network(n):
    pairs = []
    p = 1
    while p < n:
        k = p
        while k >= 1:
            for j in range(k % p, n - k, 2 * k):
                for i in range(min(k, n - j - k)):
                    if (i + j) // (2 * p) == (i + j + k) // (2 * p):
                        pairs.append((i + j, i + j + k))
            k //= 2
        p *= 2
    return pairs


_KEY_SORT_PAIRS = _sort_network(PEER_KEYS // SUBLANES)


def _top_keys(st):
    nv = PEER_KEYS // SUBLANES
    tm = st.shape[1]
    sub = lax.broadcasted_iota(I32, (SUBLANES, tm), 0)
    vals = [st[k * SUBLANES:(k + 1) * SUBLANES, :] for k in range(nv)]
    keys = [k * SUBLANES + sub for k in range(nv)]
    for i, j in _KEY_SORT_PAIRS:
        first = (vals[i] > vals[j]) | ((vals[i] == vals[j]) & (keys[i] < keys[j]))
        vals[i], vals[j] = jnp.where(first, vals[i], vals[j]), jnp.where(first, vals[j], vals[i])
        keys[i], keys[j] = jnp.where(first, keys[i], keys[j]), jnp.where(first, keys[j], keys[i])
    out_v, out_k = [], []
    for t in range(PEER_TOPK):
        m = jnp.max(vals[0], axis=0, keepdims=True)
        kmin = jnp.min(jnp.where(vals[0] == m, keys[0], jnp.int32(2 ** 30)), axis=0, keepdims=True)
        out_v.append(m)
        out_k.append(kmin)
        win = keys[0] == kmin
        for d in range(PEER_TOPK - 1 - t):
            vals[d] = jnp.where(win, vals[d + 1], vals[d])
            keys[d] = jnp.where(win, keys[d + 1], keys[d])
    return jnp.concatenate(out_v, axis=0), jnp.concatenate(out_k, axis=0)


def _top_candidates(v1, i1, v2, i2):
    tm = v1.shape[1]
    sub = lax.broadcasted_iota(I32, (SUBLANES, tm), 0)
    bc = lambda x, a: jnp.broadcast_to(x[a:a + 1, :], (SUBLANES, tm))
    vals = [jnp.where((a + 1) * (sub + 1) <= PEER_TOPK, bc(v1, a) + v2[0:SUBLANES, :], -jnp.inf)
            for a in range(PEER_TOPK)]
    eids = [bc(i1, a) * PEER_KEYS + i2[0:SUBLANES, :] for a in range(PEER_TOPK)]
    taken = jnp.zeros((SUBLANES, tm), I32)
    vals_hi = bc(v1, 0) + v2[SUBLANES:, :]
    eids_hi = bc(i1, 0) * PEER_KEYS + i2[SUBLANES:, :]
    keys_hi = SUBLANES + sub
    big = jnp.int32(2 ** 30)
    out_v, out_e = [], []
    for t in range(PEER_TOPK):
        keys = taken * PEER_TOPK + sub
        m = jnp.max(jnp.maximum(vals[0], vals_hi), axis=0, keepdims=True)
        kmin = jnp.min(jnp.minimum(jnp.where(vals[0] == m, keys, big), jnp.where(vals_hi == m, keys_hi, big)),
                       axis=0, keepdims=True)
        win = keys == kmin
        win_hi = keys_hi == kmin
        out_v.append(m)
        out_e.append(jnp.max(jnp.maximum(jnp.where(win, eids[0], -1), jnp.where(win_hi, eids_hi, -1)),
                             axis=0, keepdims=True))
        for d in range(PEER_TOPK - 1 - t):
            vals[d] = jnp.where(win, vals[d + 1], vals[d])
            eids[d] = jnp.where(win, eids[d + 1], eids[d])
        taken = jnp.where(win, taken + 1, taken)
        vals_hi = jnp.where(win_hi, -jnp.inf, vals_hi)
    return jnp.concatenate(out_v, axis=0), jnp.concatenate(out_e, axis=0)


_ROUTE_HEADS_PER_ITER = 8


def _route_kernel(h_ref, wq_ref, keys_ref, eid_ref, gate_ref, q_scr, eid_scr, gate_scr):
    tm = h_ref.shape[0]
    q = jnp.dot(h_ref[...].astype(BF16), wq_ref[...], preferred_element_type=F32)
    for c in range(2 * PEER_HEADS):
        q_scr[c] = q[:, c * PEER_HALF:(c + 1) * PEER_HALF]

    def body(it, carry):
        for off in range(_ROUTE_HEADS_PER_ITER):
            hh = it * _ROUTE_HEADS_PER_ITER + off
            tops = []
            for half in range(2):
                st = lax.dot_general(keys_ref[half], q_scr[2 * hh + half], (((1,), (1,)), ((), ())),
                                     precision=HI, preferred_element_type=F32)
                tops.append(_top_keys(st))
            (v1, i1), (v2, i2) = tops
            score, eids = _top_candidates(v1, i1, v2, i2)
            e = jnp.exp(score - jnp.max(score, axis=0, keepdims=True))
            r0 = pl.multiple_of(hh * PEER_TOPK, PEER_TOPK)
            gate_scr[pl.ds(r0, PEER_TOPK), :] = e / jnp.sum(e, axis=0, keepdims=True)
            eid_scr[pl.ds(r0, PEER_TOPK), :] = eids
        return carry

    lax.fori_loop(0, PEER_HEADS // _ROUTE_HEADS_PER_ITER, body, 0)
    gate_ref[...] = gate_scr[...].T
    eid_ref[...] = eid_scr[...].T


def _route(h, wq, keys):
    t, d = h.shape
    tm = TM_ROUTE
    nq = 2 * PEER_HEADS
    return pl.pallas_call(
        _route_kernel,
        grid=(t // tm,),
        in_specs=[pl.BlockSpec((tm, d), lambda i: (i, 0)),
                  pl.BlockSpec((d, nq * PEER_HALF), lambda i: (0, 0)),
                  pl.BlockSpec((2, PEER_KEYS, PEER_HALF), lambda i: (0, 0, 0))],
        out_specs=[pl.BlockSpec((tm, PEER_SEL), lambda i: (i, 0)),
                   pl.BlockSpec((tm, PEER_SEL), lambda i: (i, 0))],
        out_shape=[jax.ShapeDtypeStruct((t, PEER_SEL), I32),
                   jax.ShapeDtypeStruct((t, PEER_SEL), F32)],
        scratch_shapes=[pltpu.VMEM((nq, tm, PEER_HALF), F32),
                        pltpu.VMEM((PEER_SEL, tm), I32),
                        pltpu.VMEM((PEER_SEL, tm), F32)],
        compiler_params=_cparams(("parallel",), VMEM_LIMIT),
        name="peer_route",
    )(h, wq, keys)


def _peer_kernel(eid_cur_ref, eid_nxt_ref, x_ref, gate_ref, cmp_ref, exp_ref, uv_hbm, o_ref, buf, sem):
    step = pl.program_id(0)
    nsteps = pl.num_programs(0)
    g_tok = G_PEER
    rows = g_tok * PEER_SEL
    nsub = D_MODEL // LANES
    per_piece = rows // (2 * g_tok)

    def slot_wait(sl):
        pltpu.make_async_copy(uv_hbm.at[pl.ds(0, rows)], buf.at[sl], sem.at[sl]).wait()

    @pl.when(step == 0)
    def _():
        for sl in range(PEER_AHEAD):
            def body(r, carry, sl=sl):
                pltpu.make_async_copy(uv_hbm.at[eid_cur_ref[0, 0, sl * rows + r]], buf.at[sl, r], sem.at[sl]).start()
                return carry
            lax.fori_loop(0, rows, body, 0, unroll=8)

    diag = (lax.broadcasted_iota(I32, (nsub, PEER_SEL * nsub), 1) % nsub
            == lax.broadcasted_iota(I32, (nsub, PEER_SEL * nsub), 0))

    def group(sl):
        tok0 = sl * g_tok
        ahead = sl + PEER_AHEAD
        dst = ahead % PEER_SLOTS
        idx_ref = eid_cur_ref if ahead < PEER_SLOTS else eid_nxt_ref
        idx_off = dst * rows

        def prefetch(r0, n):
            for i in range(n):
                r = r0 + i
                pltpu.make_async_copy(uv_hbm.at[idx_ref[0, 0, idx_off + r]], buf.at[dst, r],
                                      sem.at[dst]).start(priority=i % 2)

        def tiles(g):
            return buf[sl, g * PEER_SEL:(g + 1) * PEER_SEL].reshape(PEER_SEL * nsub, LANES)

        slot_wait(sl)
        parts = []
        for g in range(g_tok):
            u_rows = pltpu.bitcast(tiles(g) << 16, F32).astype(BF16)
            x_row = x_ref[tok0 + g:tok0 + g + 1, :]
            x_tile = jnp.concatenate([x_row[:, s * LANES:(s + 1) * LANES] for s in range(nsub)], axis=0)
            q = lax.dot_general(x_tile.astype(BF16), u_rows, (((1,), (1,)), ((), ())),
                                preferred_element_type=F32)
            parts.append(jnp.where(diag, q, 0.0))
            prefetch(g * per_piece, per_piece)
        masked = jnp.concatenate(parts, axis=0)
        m_hi = masked.astype(BF16)
        m_lo = (masked - m_hi.astype(F32)).astype(BF16)
        part = (jnp.dot(m_hi, cmp_ref[...], preferred_element_type=F32)
                + jnp.dot(m_lo, cmp_ref[...], preferred_element_type=F32))
        act = jnp.sum(part.reshape(g_tok, nsub, PEER_SEL), axis=1)
        wgt = gate_ref[tok0:tok0 + g_tok, :] * _gelu(act)
        wrep = jnp.dot(wgt.astype(BF16), exp_ref[...], preferred_element_type=F32)
        for g in range(g_tok):
            v_rows = pltpu.bitcast(tiles(g) & jnp.uint32(0xFFFF0000), F32).astype(BF16)
            wexp = jnp.where(diag, jnp.broadcast_to(wrep[g:g + 1, :], diag.shape), 0.0).astype(BF16)
            o_tile = jnp.dot(wexp, v_rows, preferred_element_type=F32)
            for s in range(nsub):
                o_ref[tok0 + g:tok0 + g + 1, s * LANES:(s + 1) * LANES] = o_tile[s:s + 1, :]
            prefetch((g_tok + g) * per_piece, per_piece)

    for sl in range(PEER_SLOTS):
        group(sl)

    @pl.when(step == nsteps - 1)
    def _():
        for sl in range(PEER_AHEAD):
            slot_wait(sl)


def _peer(h, eid_tok, gate_tok, uv_packed):
    t, d = h.shape
    g_step = PEER_SLOTS * G_PEER
    nsteps = t // g_step
    rows = G_PEER * PEER_SEL
    nsub = d // LANES
    eid3 = eid_tok.reshape(nsteps, 1, PEER_SLOTS * rows)
    ncol = PEER_SEL * nsub
    col = jnp.arange(ncol) // nsub
    compress = (col[:, None] == jnp.arange(PEER_SEL)[None, :]).astype(F32)
    smem_blk = lambda f: pl.BlockSpec((1, 1, PEER_SLOTS * rows), f, memory_space=pltpu.SMEM)
    tile_blk = pl.BlockSpec((g_step, d), lambda i: (i, 0))
    return pl.pallas_call(
        _peer_kernel,
        grid=(nsteps,),
        in_specs=[smem_blk(lambda i: (i, 0, 0)),
                  smem_blk(lambda i: (jnp.minimum(i + 1, nsteps - 1), 0, 0)),
                  tile_blk,
                  pl.BlockSpec((g_step, PEER_SEL), lambda i: (i, 0)),
                  pl.BlockSpec((ncol, PEER_SEL), lambda i: (0, 0)),
                  pl.BlockSpec((PEER_SEL, ncol), lambda i: (0, 0)),
                  pl.BlockSpec(memory_space=pl.ANY)],
        out_specs=tile_blk,
        out_shape=jax.ShapeDtypeStruct((t, d), F32),
        scratch_shapes=[pltpu.VMEM((PEER_SLOTS, rows, nsub, LANES), U32), pltpu.SemaphoreType.DMA((PEER_SLOTS,))],
        compiler_params=_cparams(("arbitrary",), VMEM_LIMIT),
        name="peer_experts",
    )(eid3, eid3, h, gate_tok, compress.astype(BF16), compress.T.astype(BF16),
      uv_packed.reshape(-1, nsub, LANES))


_TR_PACK = 256


def _bf16_bits(x):
    bits = pltpu.bitcast(x, U32)
    return (bits + jnp.uint32(0x7FFF) + ((bits >> 16) & jnp.uint32(1))) >> 16


def _pack_kernel(u_ref, v_ref, o_ref):
    for s in range(D_MODEL // LANES):
        cols = slice(s * LANES, (s + 1) * LANES)
        o_ref[:, s, :] = _bf16_bits(u_ref[0, :, cols]) | (_bf16_bits(v_ref[0, :, cols]) << 16)


def _pack_tables(u_all, v_all, layer):
    _, n, d = u_all.shape
    tr = _TR_PACK
    src = pl.BlockSpec((1, tr, d), lambda i: (layer, i, 0))
    return pl.pallas_call(
        _pack_kernel,
        grid=(n // tr,),
        in_specs=[src, src],
        out_specs=pl.BlockSpec((tr, d // LANES, LANES), lambda i: (i, 0, 0)),
        out_shape=jax.ShapeDtypeStruct((n, d // LANES, LANES), U32),
        compiler_params=_cparams(("parallel",)),
        name="pack_tables",
    )(u_all, v_all)


def _out_kernel(h_ref, f_ref, p_ref, wp_ref, wgp_ref, g_ref, b_ref, o_ref):
    h = h_ref[...]
    ple = jnp.dot(p_ref[...].astype(BF16), wp_ref[...], preferred_element_type=F32)
    gate = _sigmoid(jnp.dot(h.astype(BF16), wgp_ref[...], preferred_element_type=F32))
    o_ref[...] = _ln(DN_ALPHA * h + f_ref[...] + ple * gate, g_ref[...], b_ref[...])


def _layer_out(h, ffn, p2, ple_w, ple_gate_w, ln_g, ln_b):
    t, d = h.shape
    tm = TM_OUT
    tok = lambda i: (i, 0)
    par = lambda i: (0, 0)
    return pl.pallas_call(
        _out_kernel,
        grid=(t // tm,),
        in_specs=[pl.BlockSpec((tm, d), tok), pl.BlockSpec((tm, d), tok),
                  pl.BlockSpec((tm, PLE_DIM), tok),
                  pl.BlockSpec((PLE_DIM, d), par), pl.BlockSpec((d, d), par),
                  pl.BlockSpec((1, d), par), pl.BlockSpec((1, d), par)],
        out_specs=pl.BlockSpec((tm, d), tok),
        out_shape=jax.ShapeDtypeStruct((t, d), F32),
        compiler_params=_cparams(("parallel",)),
        name="ple_ln2",
    )(h, ffn, p2, ple_w, ple_gate_w, ln_g.reshape(1, -1), ln_b.reshape(1, -1))


def _block_ones():
    head = jnp.arange(BRANCH) // HEAD_DIM
    return (head[:, None] == head[None, :]).astype(F32)


def _layer(h, p2, bsz, seq, layer, w_in_all, peer_u_all, peer_v_all, rwkv_mu, rwkv_w0, rwkv_w2, rwkv_a0, rwkv_a2,
           rwkv_g2, rwkv_kk, rwkv_ka, rwkv_rk, rwkv_lnx_g, rwkv_lnx_b, s5_lam_re, s5_lam_im, s5_b_re, s5_b_im,
           s5_c_re, s5_c_im, s5_d, s5_log_dt, s5_glu_w, s5_glu_b, fox_bf, conv_w, conv_b, conv_ln_g, conv_ln_b,
           w_branch, w_out, ln1_g, ln1_b, peer_wq, peer_k1, peer_k2, ple_w, ple_gate_w, ln2_g, ln2_b):
    t = bsz * seq
    d = D_MODEL
    w_packed, w_gate = _repack_w_in(w_in_all, layer)
    bd = _block_ones()
    wl = jnp.zeros((LANES, 3 * BRANCH), F32)
    wl = wl.at[0:32, 0:BRANCH].set(rwkv_w2.astype(F32))
    wl = wl.at[32:64, BRANCH:2 * BRANCH].set(rwkv_a2.astype(F32))
    wl = wl.at[64:128, 2 * BRANCH:].set(rwkv_g2.astype(F32))

    p_rwkv, p_s5_t, q_heads, k_heads, v_aug, p_ff, p_conv = _project(h, w_packed, bsz, seq)

    r, dec, k2, v, na, bb, g, bonus = _rwkv_pre(p_rwkv, rwkv_mu, rwkv_w0, rwkv_a0, rwkv_kk, rwkv_ka,
                                                 rwkv_rk.reshape(-1), wl, bd, bsz, seq)
    y_scan = _rwkv_scan(*(_to_scan(x, bsz, seq) for x in (na, dec, bb, k2, r, v)))
    y_rwkv = _rwkv_post(_from_scan(y_scan, bsz, seq), g.reshape(t, BRANCH), bonus.reshape(t, BRANCH),
                        rwkv_lnx_g, rwkv_lnx_b, bd)

    bre, bim, lre, lim, cc = _s5_params(s5_lam_re, s5_lam_im, s5_b_re, s5_b_im, s5_c_re, s5_c_im, s5_log_dt)
    y_s5_t = _s5(p_s5_t, bre, bim, lre, lim, cc, s5_d, s5_glu_w, s5_glu_b, bsz, seq)

    f_rows = p_ff[:, :HEADS].reshape(bsz, seq, HEADS).transpose(0, 2, 1).reshape(bsz * HEADS, seq)
    bias_rows = jnp.tile(fox_bf.astype(F32), bsz).reshape(bsz * HEADS, 1)
    c = _fox_cumsum(f_rows, bias_rows)
    y_fox = _fox_attn(q_heads, k_heads, v_aug,
                      c.reshape(bsz, HEADS, seq, 1), c.reshape(bsz, HEADS, seq // TK_ATT, TK_ATT))

    y_conv = _conv_mixer(p_conv, conv_w, conv_b, conv_ln_g, conv_ln_b, bsz, seq)

    h1 = _merge(h, y_rwkv, y_s5_t, y_fox, y_conv, w_gate, w_branch.astype(BF16),
                w_out.astype(BF16), ln1_g, ln1_b, bsz, seq)

    keys = jnp.stack([peer_k1, peer_k2]).astype(F32)
    eid_tok, gate_tok = _route(h1, peer_wq.astype(BF16), keys)
    ffn = _peer(h1, eid_tok, gate_tok, _pack_tables(peer_u_all, peer_v_all, layer))

    return _layer_out(h1, ffn, p2, ple_w.astype(BF16), ple_gate_w.astype(BF16), ln2_g, ln2_b)


def kernel(x, p, ln_in_g, ln_in_b, w_in, rwkv_mu, rwkv_w0, rwkv_w2, rwkv_a0, rwkv_a2, rwkv_g2, rwkv_kk, rwkv_ka, rwkv_rk, rwkv_lnx_g, rwkv_lnx_b, s5_lam_re, s5_lam_im, s5_b_re, s5_b_im, s5_c_re, s5_c_im, s5_d, s5_log_dt, s5_glu_w, s5_glu_b, fox_bf, conv_w, conv_b, conv_ln_g, conv_ln_b, w_branch, w_out, ln1_g, ln1_b, peer_wq, peer_k1, peer_k2, peer_u, peer_v, ple_w, ple_gate_w, ln2_g, ln2_b):
    bsz, seq, d = x.shape
    t = bsz * seq
    h = _layer_norm(x.reshape(t, d), ln_in_g, ln_in_b)
    per_layer = (rwkv_mu, rwkv_w0, rwkv_w2, rwkv_a0, rwkv_a2, rwkv_g2, rwkv_kk, rwkv_ka, rwkv_rk,
                 rwkv_lnx_g, rwkv_lnx_b, s5_lam_re, s5_lam_im, s5_b_re, s5_b_im, s5_c_re, s5_c_im, s5_d,
                 s5_log_dt, s5_glu_w, s5_glu_b, fox_bf, conv_w, conv_b, conv_ln_g, conv_ln_b, w_branch,
                 w_out, ln1_g, ln1_b, peer_wq, peer_k1, peer_k2, ple_w, ple_gate_w, ln2_g, ln2_b)
    for i in range(p.shape[0]):
        h = _layer(h, p[i].reshape(t, PLE_DIM), bsz, seq, i, w_in, peer_u, peer_v, *(w[i] for w in per_layer))
    return h.reshape(bsz, seq, d)
```
